```python
import math
import jax
import jax.numpy as jnp
from jax import lax
import numpy as np

D_MODEL = 1024
BATCH = 2
SEQ = 16384
DEPTH = 1

D_MIX = D_MODEL
HEAD_DIM = 64
D_ATTN = D_MIX // 2
D_CONV = D_MIX - D_ATTN
N_HEADS = D_ATTN // HEAD_DIM
N_KV = 2
HPG = N_HEADS // N_KV
KV_DIM = N_KV * HEAD_DIM
N_MIX_GROUPS = D_MIX // HEAD_DIM
CONV_WIDTH = 3
CMP_LEN = 32
CMP_STRIDE = 16
CMP_HIDDEN = 256
SLC_LEN = 64
N_SELECT = 16
WINDOW = 512
Q_BLOCK = 128
N_BUCKETS = 32
MAX_DISTANCE = 128
N_GROUPS = 8
EXPERTS_PER_GROUP = 8
N_EXPERTS = N_GROUPS * EXPERTS_PER_GROUP
TOP_K = 2
D_EXPERT = D_MODEL // 2
MOE_BLOCK = 128
FORCED_SCORE = 1e4
EPS = 1e-6
N_IN = D_ATTN + 6 * KV_DIM + 3 * N_HEADS + 3 * D_CONV

kernel_name = 'hybrid_nsa_shortconv_hmoe'


def rms_norm(x, w):
    xf = x.astype(jnp.float32)
    y = xf * lax.rsqrt(jnp.mean(xf * xf, axis=-1, keepdims=True) + EPS)
    return (y * w.astype(jnp.float32)).astype(x.dtype)


def t5_bucket(dist):
    n = jnp.maximum(dist, 0)
    max_exact = N_BUCKETS // 2
    nf = jnp.maximum(n, 1).astype(jnp.float32)
    large = max_exact + (jnp.log(nf / max_exact) / math.log(MAX_DISTANCE / max_exact)
                         * (N_BUCKETS - max_exact)).astype(jnp.int32)
    large = jnp.minimum(large, N_BUCKETS - 1)
    return jnp.where(n < max_exact, n, large)


def masked_softmax(logits, mask):
    lf = jnp.where(mask, logits.astype(jnp.float32), -jnp.inf)
    m = jnp.max(lf, axis=-1, keepdims=True)
    m = jnp.where(jnp.isfinite(m), m, 0.0)
    e = jnp.where(mask, jnp.exp(lf - m), 0.0)
    return e / jnp.maximum(jnp.sum(e, axis=-1, keepdims=True), 1e-30)


def compress(kv, pos_emb, w1, w2):
    b, s, g, dh = kv.shape
    nc = (s - CMP_LEN) // CMP_STRIDE + 1
    idx = jnp.arange(nc)[:, None] * CMP_STRIDE + jnp.arange(CMP_LEN)[None, :]
    blocks = kv[:, idx] + pos_emb[None, None, :, None, :]
    blocks = jnp.moveaxis(blocks, 3, 2).reshape(b, nc, g, CMP_LEN * dh)
    return jax.nn.silu(blocks @ w1) @ w2


def nsa_attention(q, gates, k_cmp, v_cmp, k_slc, v_slc, k_win, v_win, rel_bias):
    b, s = q.shape[0], q.shape[1]
    nc = k_cmp.shape[1]
    ns = s // SLC_LEN
    n_sel = min(N_SELECT, ns)
    nqb = s // Q_BLOCK
    scale = HEAD_DIM ** -0.5
    bias_gh = rel_bias.T.reshape(N_KV, HPG, N_BUCKETS)

    c_start = jnp.arange(nc) * CMP_STRIDE
    c_end = c_start + CMP_LEN - 1
    s_start = jnp.arange(ns) * SLC_LEN
    overlap = ((c_start[:, None] <= s_start[None, :] + SLC_LEN - 1)
               & (c_end[:, None] >= s_start[None, :])).astype(jnp.float32)

    ks_blocks = k_slc.reshape(b, ns, SLC_LEN, N_KV, HEAD_DIM).transpose(0, 3, 1, 2, 4)
    vs_blocks = v_slc.reshape(b, ns, SLC_LEN, N_KV, HEAD_DIM).transpose(0, 3, 1, 2, 4)
    pad = ((0, 0), (WINDOW, 0), (0, 0), (0, 0))
    kw_pad = jnp.pad(k_win, pad)
    vw_pad = jnp.pad(v_win, pad)

    q_all = q.reshape(b, nqb, Q_BLOCK, N_KV, HPG, HEAD_DIM).swapaxes(0, 1)
    g_all = gates.reshape(b, nqb, Q_BLOCK, N_KV, HPG, 3).swapaxes(0, 1)
    b_ix = jnp.arange(b)[:, None, None, None]
    g_ix = jnp.arange(N_KV)[None, :, None, None]
    g_ix5 = jnp.arange(N_KV)[None, :, None, None, None]
    h_ix5 = jnp.arange(HPG)[None, None, :, None, None]
    blk = jnp.arange(ns)

    def block_fn(args):
        j, qb, gb = args
        s0 = j * Q_BLOCK
        pos = s0 + jnp.arange(Q_BLOCK)

        dist_c = pos[:, None] - c_end[None, :]
        lc = jnp.einsum('bqghd,bngd->bghqn', qb, k_cmp) * scale + bias_gh[:, :, t5_bucket(dist_c)]
        p_cmp = masked_softmax(lc, dist_c >= 0)
        o_cmp = jnp.einsum('bghqn,bngd->bqghd', p_cmp.astype(v_cmp.dtype), v_cmp)

        imp = jnp.einsum('bghqn,ns->bgqs', p_cmp, overlap)
        cur = pos // SLC_LEN
        valid = blk[None, :] <= cur[:, None]
        forced = (blk[None, :] == 0) | (blk[None, :] == cur[:, None]) | (blk[None, :] == cur[:, None] - 1)
        score = jnp.where(forced, FORCED_SCORE, jnp.where(valid, imp, -1.0))
        top_val, top_idx = lax.top_k(score, n_sel)
        sel_ok = top_val >= 0.0
        ks = ks_blocks[b_ix, g_ix, top_idx].reshape(b, N_KV, Q_BLOCK, n_sel * SLC_LEN, HEAD_DIM)
        vs = vs_blocks[b_ix, g_ix, top_idx].reshape(b, N_KV, Q_BLOCK, n_sel * SLC_LEN, HEAD_DIM)
        kpos = (top_idx[..., None] * SLC_LEN + jnp.arange(SLC_LEN)).reshape(b, N_KV, Q_BLOCK, -1)
        dist_s = pos[None, None, :, None] - kpos
        mask_s = jnp.repeat(sel_ok, SLC_LEN, axis=-1) & (dist_s >= 0)
        ls = (jnp.einsum('bqghd,bgqkd->bghqk', qb, ks) * scale
              + bias_gh[g_ix5, h_ix5, t5_bucket(dist_s)[:, :, None]])
        p_s = masked_softmax(ls, mask_s[:, :, None])
        o_slc = jnp.einsum('bghqk,bgqkd->bqghd', p_s.astype(vs.dtype), vs)

        kw = lax.dynamic_slice_in_dim(kw_pad, s0, WINDOW + Q_BLOCK, axis=1)
        vw = lax.dynamic_slice_in_dim(vw_pad, s0, WINDOW + Q_BLOCK, axis=1)
        kpos_w = s0 - WINDOW + jnp.arange(WINDOW + Q_BLOCK)
        dist_w = pos[:, None] - kpos_w[None, :]
        mask_w = (dist_w >= 0) & (dist_w < WINDOW) & (kpos_w[None, :] >= 0)
        lw = jnp.einsum('bqghd,bkgd->bghqk', qb, kw) * scale + bias_gh[:, :, t5_bucket(dist_w)]
        p_w = masked_softmax(lw, mask_w)
        o_win = jnp.einsum('bghqk,bkgd->bqghd', p_w.astype(vw.dtype), vw)

        out = gb[..., 0:1] * o_cmp + gb[..., 1:2] * o_slc + gb[..., 2:3] * o_win
        return out.reshape(b, Q_BLOCK, D_ATTN)

    outs = lax.map(block_fn, (jnp.arange(nqb), q_all, g_all))
    return outs.swapaxes(0, 1).reshape(b, s, D_ATTN)


def short_conv(u, w):
    return lax.conv_general_dilated(u, w[:, None, :], window_strides=(1,),
                                    padding=[(CONV_WIDTH - 1, 0)],
                                    dimension_numbers=('NWC', 'WIO', 'NWC'),
                                    feature_group_count=u.shape[-1])


def hier_moe(h, w_group, b_group, w_expert, b_expert, w1, w3, w2):
    b, s, d = h.shape
    t = b * s
    xt = h.reshape(t, d)
    g_prob = jax.nn.softmax((xt @ w_group + b_group).astype(jnp.float32), axis=-1)
    g_top_p, g_top = lax.top_k(g_prob, 1)
    e_logits = (xt @ w_expert + b_expert).astype(jnp.float32).reshape(t, N_GROUPS, EXPERTS_PER_GROUP)
    e_in = jnp.take_along_axis(e_logits, g_top[:, :, None], axis=1)[:, 0]
    e_top_p, e_top = lax.top_k(jax.nn.softmax(e_in, axis=-1), TOP_K)
    e_top_p = e_top_p / jnp.sum(e_top_p, axis=-1, keepdims=True)
    weights = (g_top_p * e_top_p).reshape(-1)
    expert_ids = (g_top * EXPERTS_PER_GROUP + e_top).reshape(-1)
    token_ids = jnp.repeat(jnp.arange(t), TOP_K)

    a = t * TOP_K
    order = jnp.argsort(expert_ids)
    e_sorted = expert_ids[order]
    tok_sorted = token_ids[order]
    w_sorted = weights[order]
    counts = jax.ops.segment_sum(jnp.ones((a,), jnp.int32), expert_ids, num_segments=N_EXPERTS)
    padded = (counts + MOE_BLOCK - 1) // MOE_BLOCK * MOE_BLOCK
    start = jnp.cumsum(counts) - counts
    pad_end = jnp.cumsum(padded)
    pad_start = pad_end - padded
    dest = pad_start[e_sorted] + (jnp.arange(a) - start[e_sorted])
    n_blocks = (a + N_EXPERTS * (MOE_BLOCK - 1)) // MOE_BLOCK + 1
    rows = n_blocks * MOE_BLOCK
    x_disp = jnp.zeros((rows, d), xt.dtype).at[dest].set(xt[tok_sorted])
    blk_expert = jnp.minimum(jnp.searchsorted(pad_end, jnp.arange(n_blocks) * MOE_BLOCK, side='right'),
                             N_EXPERTS - 1)

    def expert_block(args):
        xb, e = args
        return (jax.nn.silu(xb @ w1[e]) * (xb @ w3[e])) @ w2[e]

    y_disp = lax.map(expert_block, (x_disp.reshape(n_blocks, MOE_BLOCK, d), blk_expert)).reshape(rows, d)
    y = jax.ops.segment_sum(y_disp[dest] * w_sorted[:, None].astype(y_disp.dtype), tok_sorted,
                            num_segments=t)
    return y.reshape(b, s, d)


def setup_inputs(seed: int = 0) -> dict:
    key = jax.random.key(seed)
    ks = jax.random.split(key, 26)
    f32 = jnp.float32

    def nrm(k, shape, scale):
        return jax.random.normal(k, shape, f32) * scale

    def gain(k, shape):
        return 1.0 + 0.02 * jax.random.normal(k, shape, f32)

    L = DEPTH
    return {
        'x': nrm(ks[0], (BATCH, SEQ, D_MODEL), 1.0),
        'c': nrm(ks[1], (BATCH, D_MODEL), 1.0),
        'w_ada': nrm(ks[2], (L, D_MODEL, 6 * D_MODEL), 0.5 * D_MODEL ** -0.5),
        'b_ada': nrm(ks[3], (L, 6 * D_MODEL), 0.01),
        'norm1_w': gain(ks[4], (L, D_MODEL)),
        'w_in': nrm(ks[5], (L, D_MODEL, N_IN), D_MODEL ** -0.5),
        'q_norm_w': gain(ks[6], (L, HEAD_DIM)),
        'k_norm_w': gain(ks[7], (L, 3, HEAD_DIM)),
        'cmp_pos_k': nrm(ks[8], (L, CMP_LEN, HEAD_DIM), 0.1),
        'cmp_pos_v': nrm(ks[9], (L, CMP_LEN, HEAD_DIM), 0.1),
        'cmp_k_w1': nrm(ks[10], (L, CMP_LEN * HEAD_DIM, CMP_HIDDEN), (CMP_LEN * HEAD_DIM) ** -0.5),
        'cmp_k_w2': nrm(ks[11], (L, CMP_HIDDEN, HEAD_DIM), CMP_HIDDEN ** -0.5),
        'cmp_v_w1': nrm(ks[12], (L, CMP_LEN * HEAD_DIM, CMP_HIDDEN), (CMP_LEN * HEAD_DIM) ** -0.5),
        'cmp_v_w2': nrm(ks[13], (L, CMP_HIDDEN, HEAD_DIM), CMP_HIDDEN ** -0.5),
        'conv_w': nrm(ks[14], (L, CONV_WIDTH, D_CONV), CONV_WIDTH ** -0.5),
        'out_norm_w': gain(ks[15], (L, D_MIX)),
        'w_out': nrm(ks[16], (L, D_MIX, D_MODEL), D_MIX ** -0.5),
        'rel_bias': nrm(ks[17], (N_BUCKETS, N_HEADS), 0.5),
        'norm2_w': gain(ks[18], (L, D_MODEL)),
        'w_group': nrm(ks[19], (L, D_MODEL, N_GROUPS), D_MODEL ** -0.5),
        'b_group': nrm(ks[20], (L, N_GROUPS), 0.01),
        'w_expert': nrm(ks[21], (L, D_MODEL, N_EXPERTS), D_MODEL ** -0.5),
        'b_expert': nrm(ks[22], (L, N_EXPERTS), 0.01),
        'w1': nrm(ks[23], (L, N_EXPERTS, D_MODEL, D_EXPERT), D_MODEL ** -0.5),
        'w3': nrm(ks[24], (L, N_EXPERTS, D_MODEL, D_EXPERT), D_MODEL ** -0.5),
        'w2': nrm(ks[25], (L, N_EXPERTS, D_EXPERT, D_MODEL), D_EXPERT ** -0.5),
    }


def reference(x, c, w_ada, b_ada, norm1_w, w_in, q_norm_w, k_norm_w, cmp_pos_k, cmp_pos_v,
              cmp_k_w1, cmp_k_w2, cmp_v_w1, cmp_v_w2, conv_w, out_norm_w, w_out, rel_bias,
              norm2_w, w_group, b_group, w_expert, b_expert, w1, w3, w2):
    b, s, _ = x.shape
    cond = jax.nn.silu(c)
    splits = [D_ATTN, D_ATTN + 6 * KV_DIM, D_ATTN + 6 * KV_DIM + 3 * N_HEADS,
              D_ATTN + 6 * KV_DIM + 3 * N_HEADS + D_CONV,
              D_ATTN + 6 * KV_DIM + 3 * N_HEADS + 2 * D_CONV]
    for l in range(DEPTH):
        mod = cond @ w_ada[l] + b_ada[l]
        sh1, sc1, g1, sh2, sc2, g2 = jnp.split(mod[:, None, :], 6, axis=-1)

        h = rms_norm(x, norm1_w[l]) * (1 + sc1) + sh1
        proj = h @ w_in[l]
        q, kvs, gate_logits, b_gate, c_gate, u = jnp.split(proj, splits, axis=-1)
        q = rms_norm(q.reshape(b, s, N_HEADS, HEAD_DIM), q_norm_w[l])
        kv = kvs.reshape(b, s, 6, N_KV, HEAD_DIM)
        k_c = rms_norm(compress(kv[:, :, 0], cmp_pos_k[l], cmp_k_w1[l], cmp_k_w2[l]), k_norm_w[l, 0])
        v_c = compress(kv[:, :, 1], cmp_pos_v[l], cmp_v_w1[l], cmp_v_w2[l])
        k_s = rms_norm(kv[:, :, 2], k_norm_w[l, 1])
        k_w = rms_norm(kv[:, :, 4], k_norm_w[l, 2])
        gates = jax.nn.sigmoid(gate_logits.reshape(b, s, N_HEADS, 3))
        o_attn = nsa_attention(q, gates, k_c, v_c, k_s, kv[:, :, 3], k_w, kv[:, :, 5], rel_bias)
        y_conv = b_gate * short_conv(c_gate * u, conv_w[l])
        mix = jnp.concatenate([o_attn, y_conv], axis=-1).reshape(b, s, N_MIX_GROUPS, HEAD_DIM)
        mix = rms_norm(mix, out_norm_w[l].reshape(N_MIX_GROUPS, HEAD_DIM)).reshape(b, s, D_MIX)
        x = x + g1 * (mix @ w_out[l])

        h2 = rms_norm(x, norm2_w[l]) * (1 + sc2) + sh2
        x = x + g2 * hier_moe(h2, w_group[l], b_group[l], w_expert[l], b_expert[l], w1[l], w3[l], w2[l])
    return x
```

```python
import functools
import math

import numpy as np
import jax
import jax.numpy as jnp
from jax import lax
from jax.experimental import pallas as pl
from jax.experimental.pallas import tpu as pltpu

F32 = jnp.float32
BF16 = jnp.bfloat16
I32 = jnp.int32

HEAD_DIM = 64
N_HEADS = 8
N_KV = 2
HPG = N_HEADS // N_KV
D_ATTN = N_HEADS * HEAD_DIM
D_CONV = 512
KV_DIM = N_KV * HEAD_DIM
CMP_LEN = 32
CMP_STRIDE = 16
CMP_HIDDEN = 256
SLC_LEN = 64
N_SELECT = 16
WINDOW = 512
N_BUCKETS = 32
MAX_DISTANCE = 128
N_GROUPS = 8
EXPERTS_PER_GROUP = 8
N_EXPERTS = N_GROUPS * EXPERTS_PER_GROUP
MOE_BLOCK = 128
FORCED_SCORE = 1e4
EPS = 1e-6

LANES = 128
QB = 512
ROWS_Q = HPG * QB
CMP_CHUNK = 128
PEN_BLOCKS = 128
HALF_KEYS = PEN_BLOCKS * SLC_LEN
NEG = -1e30
VMEM_LIMIT = 56 * 1024 * 1024


def _dot(a, b):
    return jnp.dot(a, b, preferred_element_type=F32)


def _dot_nt(a, b):
    return lax.dot_general(a, b, (((1,), (1,)), ((), ())), preferred_element_type=F32)


def _hi_lo(a):
    hi = a.astype(BF16)
    lo = (a - hi.astype(F32)).astype(BF16)
    return hi, lo


def _dot3(a, b_hi, b_lo):
    a_hi, a_lo = _hi_lo(a)
    return _dot(a_hi, b_hi) + (_dot(a_lo, b_hi) + _dot(a_hi, b_lo))


def _group_sumsq(v, gm):
    hi, lo = _hi_lo(v * v)
    return _dot(hi, gm) + _dot(lo, gm)


def _silu(v):
    return v * jax.nn.sigmoid(v)


def _ada_kernel(c_ref, w_ref, b_ref, o_ref):
    cond = _silu(c_ref[...])
    w_hi, w_lo = _hi_lo(w_ref[...])
    o_ref[...] = _dot3(cond, w_hi, w_lo) + b_ref[...]


def _ada(c, w_ada, b_ada):
    b, d = c.shape
    n = w_ada.shape[1]
    tn = n // 4
    c8 = jnp.pad(c, ((0, 8 - b), (0, 0)))
    out = pl.pallas_call(
        _ada_kernel,
        grid=(n // tn,),
        in_specs=[pl.BlockSpec((8, d), lambda i: (0, 0)),
                  pl.BlockSpec((d, tn), lambda i: (0, i)),
                  pl.BlockSpec((1, tn), lambda i: (0, i))],
        out_specs=pl.BlockSpec((8, tn), lambda i: (0, i)),
        out_shape=jax.ShapeDtypeStruct((8, n), F32),
        compiler_params=pltpu.CompilerParams(vmem_limit_bytes=VMEM_LIMIT),
        name="ada",
    )(c8, w_ada, b_ada.reshape(1, n))
    return out[:b].reshape(b, 6, d)


def _inproj_kernel(x_ref, xh_ref, mod_ref, n1w_ref, wq_ref, wkv_ref, wg_ref, wbcu_ref,
                   qnw_ref, knw_ref, convw_ref, onw_ref, gm_ref,
                   q_ref, kvc_ref, kaug_ref, vaug_ref, kwin_ref, vwaug_ref, gates_ref, yconv_ref,
                   ext_ref, *, tm):
    i = pl.program_id(1)
    mod = mod_ref[0]
    sh1 = mod[0:1]
    sc1 = mod[1:2]

    def norm_mod(v):
        ms = jnp.mean(v * v, axis=-1, keepdims=True)
        y = (v * lax.rsqrt(ms + EPS)) * n1w_ref[...]
        return (y * (1.0 + sc1) + sh1).astype(BF16)

    h = norm_mod(x_ref[0])
    gm = gm_ref[...]
    gm_kv = gm_ref[0:KV_DIM, 0:KV_DIM]

    q = _dot(h, wq_ref[...])
    qn = q * lax.rsqrt(_group_sumsq(q, gm) * (1.0 / HEAD_DIM) + EPS) * qnw_ref[...]
    qn = (qn * (HEAD_DIM ** -0.5)).astype(BF16)
    for hd in range(N_HEADS):
        q_ref[0, hd] = qn[:, hd * HEAD_DIM:(hd + 1) * HEAD_DIM]

    kv = _dot(h, wkv_ref[...])
    kvc_ref[0] = kv[:, 0:2 * KV_DIM]
    ks = kv[:, 2 * KV_DIM:3 * KV_DIM]
    vs = kv[:, 3 * KV_DIM:4 * KV_DIM]
    kw = kv[:, 4 * KV_DIM:5 * KV_DIM]
    vw = kv[:, 5 * KV_DIM:6 * KV_DIM]
    ksn = ks * lax.rsqrt(_group_sumsq(ks, gm_kv) * (1.0 / HEAD_DIM) + EPS) * knw_ref[0:1]
    kwn = kw * lax.rsqrt(_group_sumsq(kw, gm_kv) * (1.0 / HEAD_DIM) + EPS) * knw_ref[1:2]

    lane = lax.broadcasted_iota(I32, (tm, LANES), 1)
    row = lax.broadcasted_iota(I32, (tm, LANES), 0)
    blk = ((i * tm + row) // SLC_LEN) % PEN_BLOCKS
    onehot = jnp.where(lane == blk, 1.0, 0.0).astype(BF16)
    ones_col = jnp.where(lax.broadcasted_iota(I32, (tm, HEAD_DIM), 1) == 0, 1.0, 0.0).astype(BF16)
    zeros64 = jnp.zeros((tm, HEAD_DIM), BF16)
    for g in range(N_KV):
        sl = slice(g * HEAD_DIM, (g + 1) * HEAD_DIM)
        kaug_ref[0, g, :, 0:LANES] = onehot
        kaug_ref[0, g, :, LANES:LANES + HEAD_DIM] = ksn[:, sl].astype(BF16)
        kaug_ref[0, g, :, LANES + HEAD_DIM:2 * LANES] = zeros64
        vaug_ref[0, g, :, 0:HEAD_DIM] = vs[:, sl].astype(BF16)
        vaug_ref[0, g, :, HEAD_DIM:LANES] = ones_col
        kwin_ref[0, g] = kwn[:, sl].astype(BF16)
        vwaug_ref[0, g, :, 0:HEAD_DIM] = vw[:, sl].astype(BF16)
        vwaug_ref[0, g, :, HEAD_DIM:LANES] = ones_col

    gates_ref[0] = jax.nn.sigmoid(_dot(h, wg_ref[...]))

    bcu = _dot(h, wbcu_ref[...])
    bg = bcu[:, 0:D_CONV]
    cu = bcu[:, D_CONV:2 * D_CONV] * bcu[:, 2 * D_CONV:3 * D_CONV]
    hh = norm_mod(xh_ref[0])
    cuh = _dot(hh, wbcu_ref[:, D_CONV:2 * D_CONV]) * _dot(hh, wbcu_ref[:, 2 * D_CONV:3 * D_CONV])
    cuh = jnp.where(i > 0, cuh, 0.0)
    ext_ref[0:8] = cuh
    ext_ref[8:8 + tm] = cu
    cw = convw_ref[...]
    y = bg * (cw[0:1] * ext_ref[6:6 + tm] + cw[1:2] * ext_ref[7:7 + tm] + cw[2:3] * cu)
    yn = y * lax.rsqrt(_group_sumsq(y, gm) * (1.0 / HEAD_DIM) + EPS) * onw_ref[...]
    yconv_ref[0] = yn.astype(BF16)


def _inproj(x, mod, norm1_w, w_in, q_norm_w, k_norm_w, conv_w, out_norm_w, tm=512):
    b, s, d = x.shape
    o_kv = D_ATTN
    o_g = o_kv + 6 * KV_DIM
    o_b = o_g + 3 * N_HEADS
    wq = w_in[:, 0:o_kv].astype(BF16)
    wkv = w_in[:, o_kv:o_g].astype(BF16)
    wg = jnp.pad(w_in[:, o_g:o_b].reshape(d, N_KV, 3 * HPG),
                 ((0, 0), (0, 0), (0, LANES - 3 * HPG))).reshape(d, N_KV * LANES).astype(BF16)
    wbcu = w_in[:, o_b:].astype(BF16)
    qnw = jnp.tile(q_norm_w, N_HEADS).reshape(1, D_ATTN)
    knw = jnp.stack([jnp.tile(k_norm_w[1], N_KV), jnp.tile(k_norm_w[2], N_KV)])
    onw = out_norm_w[D_ATTN:].reshape(1, D_CONV)
    gidx = np.arange(D_ATTN) // HEAD_DIM
    gm = jnp.asarray((gidx[:, None] == gidx[None, :]).astype(np.float32), dtype=BF16)

    full = lambda shape: pl.BlockSpec(shape, lambda bi, i: (0,) * len(shape))
    outs = pl.pallas_call(
        functools.partial(_inproj_kernel, tm=tm),
        grid=(b, s // tm),
        in_specs=[pl.BlockSpec((1, tm, d), lambda bi, i: (bi, i, 0)),
                  pl.BlockSpec((1, 8, d), lambda bi, i: (bi, jnp.maximum(i * (tm // 8) - 1, 0), 0)),
                  pl.BlockSpec((1, 6, d), lambda bi, i: (bi, 0, 0)),
                  full((1, d)), full(wq.shape), full(wkv.shape), full(wg.shape), full(wbcu.shape),
                  full(qnw.shape), full(knw.shape), full(conv_w.shape), full(onw.shape), full(gm.shape)],
        out_specs=[pl.BlockSpec((1, N_HEADS, tm, HEAD_DIM), lambda bi, i: (bi, 0, i, 0)),
                   pl.BlockSpec((1, tm, 2 * KV_DIM), lambda bi, i: (bi, i, 0)),
                   pl.BlockSpec((1, N_KV, tm, 2 * LANES), lambda bi, i: (bi, 0, i, 0)),
                   pl.BlockSpec((1, N_KV, tm, LANES), lambda bi, i: (bi, 0, i, 0)),
                   pl.BlockSpec((1, N_KV, tm, HEAD_DIM), lambda bi, i: (bi, 0, i, 0)),
                   pl.BlockSpec((1, N_KV, tm, LANES), lambda bi, i: (bi, 0, i, 0)),
                   pl.BlockSpec((1, tm, N_KV * LANES), lambda bi, i: (bi, i, 0)),
                   pl.BlockSpec((1, tm, D_CONV), lambda bi, i: (bi, i, 0))],
        out_shape=[jax.ShapeDtypeStruct((b, N_HEADS, s, HEAD_DIM), BF16),
                   jax.ShapeDtypeStruct((b, s, 2 * KV_DIM), F32),
                   jax.ShapeDtypeStruct((b, N_KV, s, 2 * LANES), BF16),
                   jax.ShapeDtypeStruct((b, N_KV, s, LANES), BF16),
                   jax.ShapeDtypeStruct((b, N_KV, s, HEAD_DIM), BF16),
                   jax.ShapeDtypeStruct((b, N_KV, s, LANES), BF16),
                   jax.ShapeDtypeStruct((b, s, N_KV * LANES), F32),
                   jax.ShapeDtypeStruct((b, s, D_CONV), BF16)],
        scratch_shapes=[pltpu.VMEM((tm + 8, D_CONV), F32)],
        compiler_params=pltpu.CompilerParams(
            dimension_semantics=("parallel", "arbitrary"), vmem_limit_bytes=VMEM_LIMIT),
        name="inproj",
    )(x, x, mod, norm1_w.reshape(1, d), wq, wkv, wg, wbcu, qnw, knw, conv_w, onw, gm)
    return outs


def _compress_kernel(r_ref, w1_ref, w2_ref, pos_ref, nw_ref, o_ref, *, norm):
    r = r_ref[0, 0, 0]
    nr = r.shape[0]
    half = (CMP_LEN // 2) * HEAD_DIM
    w1a = w1_ref[0:half, :]
    w1b = w1_ref[half:2 * half, :]
    pos = pos_ref[...]
    cst = (_dot(jnp.broadcast_to(pos[0:1], (8, half)).astype(BF16), w1a)
           + _dot(jnp.broadcast_to(pos[1:2], (8, half)).astype(BF16), w1b))[0:1]
    upper = pltpu.roll(_dot(r, w1b), nr - 1, 0)
    hid = _dot(r, w1a) + upper + cst
    out = _dot(_silu(hid).astype(BF16), w2_ref[...])
    if norm:
        ms = jnp.mean(out * out, axis=-1, keepdims=True)
        out = out * lax.rsqrt(ms + EPS) * nw_ref[...]
    o_ref[0, 0] = out.astype(BF16)


def _compress(rows, kv_index, w1, w2, pos_emb, norm_w, norm):
    b, _, g, nr, width = rows.shape
    return pl.pallas_call(
        functools.partial(_compress_kernel, norm=norm),
        grid=(b, g),
        in_specs=[pl.BlockSpec((1, 1, 1, nr, width), lambda bi, gi: (bi, kv_index, gi, 0, 0)),
                  pl.BlockSpec(w1.shape, lambda bi, gi: (0, 0)),
                  pl.BlockSpec(w2.shape, lambda bi, gi: (0, 0)),
                  pl.BlockSpec((2, width), lambda bi, gi: (0, 0)),
                  pl.BlockSpec((1, HEAD_DIM), lambda bi, gi: (0, 0))],
        out_specs=pl.BlockSpec((1, 1, nr, HEAD_DIM), lambda bi, gi: (bi, gi, 0, 0)),
        out_shape=jax.ShapeDtypeStruct((b, g, nr, HEAD_DIM), BF16),
        compiler_params=pltpu.CompilerParams(vmem_limit_bytes=VMEM_LIMIT),
        name="compress_k" if norm else "compress_v",
    )(rows, w1.astype(BF16), w2.astype(BF16), pos_emb.reshape(2, width), norm_w.reshape(1, HEAD_DIM))


def _kvc_rows(kvc):
    b, s, _ = kvc.shape
    r = kvc.reshape(b, s, 2, N_KV, HEAD_DIM).transpose(0, 2, 3, 1, 4)
    return r.reshape(b, 2, N_KV, s // CMP_STRIDE, CMP_STRIDE * HEAD_DIM).astype(BF16)


def _bucket_np(dist):
    n = np.maximum(dist, 0)
    max_exact = N_BUCKETS // 2
    nf = np.maximum(n, 1).astype(np.float32)
    large = max_exact + (np.log(nf / np.float32(max_exact)) / np.float32(math.log(MAX_DISTANCE / max_exact))
                         * np.float32(N_BUCKETS - max_exact)).astype(np.int32)
    large = np.minimum(large, N_BUCKETS - 1)
    return np.where(n < max_exact, n, large)


def _bias_table(rel_bias, dist, valid):
    bidx = _bucket_np(np.clip(dist, 0, MAX_DISTANCE))
    rb = rel_bias - rel_bias[N_BUCKETS - 1][None, :]
    t = jnp.where(jnp.asarray(valid)[..., None], rb[bidx], NEG)
    nq, nk = dist.shape
    return t.transpose(2, 0, 1).reshape(N_KV, HPG * nq, nk)


def _cmpsel_kernel(q_ref, kc_ref, vc_ref, tab_ref, ovt_ref, ocmp_ref, pen_ref,
                   m_ref, l_ref, acc_ref, imp_ref, score_ref, sel_ref, *, nsbp, n_sel):
    j = pl.program_id(2)
    q4 = q_ref[0].reshape(ROWS_Q, HEAD_DIM)
    nq16 = QB // CMP_STRIDE
    n_chunks = (nq16 * (j + 1) - 1) // CMP_CHUNK + 1
    ni = lax.broadcasted_iota(I32, (1, CMP_CHUNK), 1)

    def chunk_start(m):
        return pl.multiple_of(nq16 * (j + 1) - CMP_CHUNK * m, nq16)

    def logits(m):
        n0p = chunk_start(m)
        s = _dot_nt(q4, kc_ref[0, 0, pl.ds(n0p, CMP_CHUNK), :])
        return jnp.where(ni + n0p >= CMP_CHUNK, s, NEG)

    s0 = logits(0) + tab_ref[0]
    m0 = jnp.max(s0, axis=1, keepdims=True)
    m_ref[...] = jnp.broadcast_to(m0, (ROWS_Q, LANES))
    l_ref[...] = jnp.broadcast_to(jnp.sum(jnp.exp(s0 - m0), axis=1, keepdims=True), (ROWS_Q, LANES))

    def stats(m, carry):
        s = logits(m)
        m_prev = m_ref[...]
        m_new = jnp.maximum(m_prev, jnp.max(s, axis=1, keepdims=True))
        l_ref[...] = (l_ref[...] * jnp.exp(m_prev - m_new)
                      + jnp.sum(jnp.exp(s - m_new), axis=1, keepdims=True))
        m_ref[...] = m_new
        return carry

    lax.fori_loop(1, n_chunks, stats, 0)

    m_fin = m_ref[...]
    inv = jnp.where(m_fin > 0.5 * NEG, 1.0 / l_ref[...], 0.0)
    acc_ref[...] = jnp.zeros_like(acc_ref)
    imp_ref[...] = jnp.zeros_like(imp_ref)

    def accumulate(m, s):
        n0p = chunk_start(m)
        p = jnp.exp(s - m_fin) * inv
        acc_ref[...] += _dot(p.astype(BF16), vc_ref[0, 0, pl.ds(n0p, CMP_CHUNK), :])
        psum = p[0:QB]
        for hd in range(1, HPG):
            psum = psum + p[hd * QB:(hd + 1) * QB]
        hi, lo = _hi_lo(psum)
        contrib = _dot_nt(ovt_ref[...], hi) + _dot_nt(ovt_ref[...], lo)
        r0 = pl.multiple_of(n0p // 4, 8)
        imp_ref[pl.ds(r0, OV_ROWS), :] += contrib

    accumulate(0, s0)

    def second(m, carry):
        accumulate(m, logits(m))
        return carry

    lax.fori_loop(1, n_chunks, second, 0)
    ocmp_ref[0, 0] = acc_ref[...].reshape(HPG, QB, HEAD_DIM)

    imp = imp_ref[IMP_PAD:IMP_PAD + nsbp, :]
    srow = lax.broadcasted_iota(I32, (nsbp, QB), 0)
    qcol = lax.broadcasted_iota(I32, (nsbp, QB), 1)
    cur = (QB // SLC_LEN) * j + qcol // SLC_LEN
    forced = (srow == 0) | (srow == cur) | (srow == cur - 1)
    score_ref[...] = jnp.where(forced, FORCED_SCORE, jnp.where(srow <= cur, imp, -1.0))
    sel_ref[...] = jnp.zeros_like(sel_ref)

    def pick(r, carry):
        sc = score_ref[...]
        mx = jnp.max(sc, axis=0, keepdims=True)
        idx = jnp.min(jnp.where(sc == mx, srow, nsbp), axis=0, keepdims=True)
        hit = srow == idx
        sel_ref[...] = jnp.where(hit, jnp.where(mx >= 0.0, 1.0, 0.0), sel_ref[...])
        score_ref[...] = jnp.where(hit, -3e38, sc)
        return carry

    lax.fori_loop(0, n_sel, pick, 0)
    pen_t = jnp.where(sel_ref[...] > 0.5, 0.0, NEG).T
    for hf in range(nsbp // PEN_BLOCKS):
        pen_ref[0, 0, hf] = pen_t[:, hf * PEN_BLOCKS:(hf + 1) * PEN_BLOCKS].astype(BF16)


IMP_PAD = CMP_CHUNK // 4
OV_ROWS = 40


def _overlap_t():
    n = np.arange(CMP_CHUNK)
    s = np.arange(OV_ROWS)
    first = (n * CMP_STRIDE) // SLC_LEN
    last = (n * CMP_STRIDE + CMP_LEN - 1) // SLC_LEN
    ov = (s[:, None] >= first[None, :]) & (s[:, None] <= last[None, :])
    return jnp.asarray(ov.astype(np.float32), dtype=BF16)


def _cmpsel(q, k_c, v_c, rel_bias):
    b, _, s, _ = q.shape
    nr = k_c.shape[2]
    nq = s // QB
    nsb = s // SLC_LEN
    nh = -(-nsb // PEN_BLOCKS)
    nsbp = nh * PEN_BLOCKS
    n_sel = min(N_SELECT, nsb)
    pad = ((0, 0), (0, 0), (CMP_CHUNK, 0), (0, 0))
    kc_p = jnp.pad(k_c, pad)
    vc_p = jnp.pad(v_c, pad)
    qi = np.arange(QB)[:, None]
    ni = np.arange(CMP_CHUNK)[None, :]
    dist = qi - CMP_STRIDE * ni + (CMP_STRIDE * CMP_CHUNK - (CMP_LEN - 1) - QB)
    tab = _bias_table(rel_bias, dist, dist >= 0)
    ovt = _overlap_t()
    return pl.pallas_call(
        functools.partial(_cmpsel_kernel, nsbp=nsbp, n_sel=n_sel),
        grid=(b, N_KV, nq),
        in_specs=[pl.BlockSpec((1, HPG, QB, HEAD_DIM), lambda bi, g, j: (bi, g, j, 0)),
                  pl.BlockSpec((1, 1, nr + CMP_CHUNK, HEAD_DIM), lambda bi, g, j: (bi, g, 0, 0)),
                  pl.BlockSpec((1, 1, nr + CMP_CHUNK, HEAD_DIM), lambda bi, g, j: (bi, g, 0, 0)),
                  pl.BlockSpec((1, ROWS_Q, CMP_CHUNK), lambda bi, g, j: (g, 0, 0)),
                  pl.BlockSpec(ovt.shape, lambda bi, g, j: (0, 0))],
        out_specs=[pl.BlockSpec((1, 1, HPG, QB, HEAD_DIM), lambda bi, g, j: (bi, g, 0, j, 0)),
                   pl.BlockSpec((1, 1, nh, QB, PEN_BLOCKS), lambda bi, g, j: (bi, g, 0, j, 0))],
        out_shape=[jax.ShapeDtypeStruct((b, N_KV, HPG, s, HEAD_DIM), F32),
                   jax.ShapeDtypeStruct((b, N_KV, nh, s, PEN_BLOCKS), BF16)],
        scratch_shapes=[pltpu.VMEM((ROWS_Q, LANES), F32), pltpu.VMEM((ROWS_Q, LANES), F32),
                        pltpu.VMEM((ROWS_Q, HEAD_DIM), F32),
                        pltpu.VMEM((IMP_PAD + nsbp + 8, QB), F32),
                        pltpu.VMEM((nsbp, QB), F32), pltpu.VMEM((nsbp, QB), F32)],
        compiler_params=pltpu.CompilerParams(
            dimension_semantics=("parallel", "parallel", "arbitrary"), vmem_limit_bytes=VMEM_LIMIT),
        name="cmpsel",
    )(q, kc_p, vc_p, tab, ovt)


def _slcwin_kernel(q_ref, pen_ref, kaug_ref, vaug_ref, kwp_ref, kwc_ref, vwp_ref, vwc_ref,
                   thi_ref, tlo_ref, ocmp_ref, gates_ref, onw_ref, o_ref,
                   lhs_ref, m_ref, acc_ref, *, nh):
    j = pl.program_id(2)
    s0 = pl.multiple_of(j * QB, QB)
    q4 = q_ref[0].reshape(ROWS_Q, HEAD_DIM)
    near = QB - LANES

    lhs_ref[:, LANES:LANES + HEAD_DIM] = q4
    lhs_ref[:, LANES + HEAD_DIM:2 * LANES] = jnp.zeros((ROWS_Q, HEAD_DIM), BF16)

    def set_pen(half):
        p = pen_ref[0, 0, half]
        for hd in range(HPG):
            lhs_ref[hd * QB:(hd + 1) * QB, 0:LANES] = p

    def reset():
        m_ref[...] = jnp.full_like(m_ref, NEG)
        acc_ref[...] = jnp.zeros_like(acc_ref)

    def update(s, v):
        m_prev = m_ref[...]
        m_new = jnp.maximum(m_prev, jnp.max(s, axis=1, keepdims=True))
        p = jnp.exp(s - jnp.concatenate([m_new] * (s.shape[1] // LANES), axis=1))
        acc_ref[...] = acc_ref[...] * jnp.exp(m_prev - m_new) + _dot(p.astype(BF16), v)
        m_ref[...] = m_new

    def add_near_bias(s):
        return jnp.concatenate([s[:, :near], s[:, near:] + tlo_ref[0]], axis=1)

    def result():
        acc = acc_ref[...]
        return acc[:, 0:HEAD_DIM] / acc[:, HEAD_DIM:HEAD_DIM + 1]

    reset()
    half_hi = s0 // HALF_KEYS
    set_pen(half_hi)
    update(_dot_nt(lhs_ref[...], kaug_ref[0, 0, pl.ds(s0, QB), :]) + thi_ref[0],
           vaug_ref[0, 0, pl.ds(s0, QB), :])

    @pl.when(j > 0)
    def _():
        sl = pl.multiple_of(s0 - QB, QB)
        half_lo = sl // HALF_KEYS

        @pl.when(half_lo != half_hi)
        def _():
            set_pen(half_lo)

        s = _dot_nt(lhs_ref[...], kaug_ref[0, 0, pl.ds(sl, QB), :])
        update(add_near_bias(s), vaug_ref[0, 0, pl.ds(sl, QB), :])

    n_far = jnp.maximum(j - 1, 0)
    chunks_per_half = HALF_KEYS // QB
    for hf in range(nh):
        lo = hf * chunks_per_half
        hi = jnp.minimum((hf + 1) * chunks_per_half, n_far)

        @pl.when(hi > lo)
        def _():
            set_pen(hf)

            def far(c, carry):
                k0 = pl.multiple_of(c * QB, QB)
                update(_dot_nt(lhs_ref[...], kaug_ref[0, 0, pl.ds(k0, QB), :]),
                       vaug_ref[0, 0, pl.ds(k0, QB), :])
                return carry

            lax.fori_loop(lo, hi, far, 0)

    o_slc = result()

    reset()
    update(_dot_nt(q4, kwc_ref[0, 0]) + thi_ref[0], vwc_ref[0, 0])

    @pl.when(j > 0)
    def _():
        s = _dot_nt(q4, kwp_ref[0, 0])
        qi = lax.broadcasted_iota(I32, (ROWS_Q, QB), 0) & (QB - 1)
        kk = lax.broadcasted_iota(I32, (ROWS_Q, QB), 1)
        s = jnp.where(kk > qi, s, NEG)
        update(add_near_bias(s), vwp_ref[0, 0])

    o_win = result()

    o_cmp = ocmp_ref[0, 0].reshape(ROWS_Q, HEAD_DIM)
    gt = gates_ref[0]
    outs = []
    for hd in range(HPG):
        rs = slice(hd * QB, (hd + 1) * QB)
        o = (gt[:, 3 * hd:3 * hd + 1] * o_cmp[rs] + gt[:, 3 * hd + 1:3 * hd + 2] * o_slc[rs]
             + gt[:, 3 * hd + 2:3 * hd + 3] * o_win[rs])
        ms = jnp.mean(o * o, axis=-1, keepdims=True)
        outs.append((o * lax.rsqrt(ms + EPS) * onw_ref[0, hd:hd + 1, :]).astype(BF16))
    o_ref[0] = jnp.concatenate(outs, axis=1)


def _slcwin(q, pen, kaug, vaug, kwin, vwaug, o_cmp, gates, rel_bias, out_norm_w):
    assert QB == WINDOW
    b, _, s, _ = q.shape
    nq = s // QB
    nh = pen.shape[2]
    qi = np.arange(QB)[:, None]
    kk = np.arange(QB)[None, :]
    t_hi = _bias_table(rel_bias, qi - kk, qi - kk >= 0)
    d_lo = qi + LANES - np.arange(LANES)[None, :]
    t_lo = _bias_table(rel_bias, d_lo, d_lo >= 0)
    onw = out_norm_w[:D_ATTN].reshape(N_KV, HPG, HEAD_DIM)
    prev = lambda bi, g, j: (bi, g, jnp.maximum(j - 1, 0), 0)
    curr = lambda bi, g, j: (bi, g, j, 0)
    return pl.pallas_call(
        functools.partial(_slcwin_kernel, nh=nh),
        grid=(b, N_KV, nq),
        in_specs=[pl.BlockSpec((1, HPG, QB, HEAD_DIM), lambda bi, g, j: (bi, g, j, 0)),
                  pl.BlockSpec((1, 1, nh, QB, PEN_BLOCKS), lambda bi, g, j: (bi, g, 0, j, 0)),
                  pl.BlockSpec((1, 1, s, 2 * LANES), lambda bi, g, j: (bi, g, 0, 0)),
                  pl.BlockSpec((1, 1, s, LANES), lambda bi, g, j: (bi, g, 0, 0)),
                  pl.BlockSpec((1, 1, QB, HEAD_DIM), prev),
                  pl.BlockSpec((1, 1, QB, HEAD_DIM), curr),
                  pl.BlockSpec((1, 1, QB, LANES), prev),
                  pl.BlockSpec((1, 1, QB, LANES), curr),
                  pl.BlockSpec((1, ROWS_Q, QB), lambda bi, g, j: (g, 0, 0)),
                  pl.BlockSpec((1, ROWS_Q, LANES), lambda bi, g, j: (g, 0, 0)),
                  pl.BlockSpec((1, 1, HPG, QB, HEAD_DIM), lambda bi, g, j: (bi, g, 0, j, 0)),
                  pl.BlockSpec((1, QB, LANES), lambda bi, g, j: (bi, j, g)),
                  pl.BlockSpec((1, HPG, HEAD_DIM), lambda bi, g, j: (g, 0, 0))],
        out_specs=pl.BlockSpec((1, QB, HPG * HEAD_DIM), lambda bi, g, j: (bi, j, g)),
        out_shape=jax.ShapeDtypeStruct((b, s, D_ATTN), BF16),
        scratch_shapes=[pltpu.VMEM((ROWS_Q, 2 * LANES), BF16),
                        pltpu.VMEM((ROWS_Q, LANES), F32),
                        pltpu.VMEM((ROWS_Q, LANES), F32)],
        compiler_params=pltpu.CompilerParams(
            dimension_semantics=("parallel", "parallel", "arbitrary"), vmem_limit_bytes=VMEM_LIMIT),
        name="slcwin",
    )(q, pen, kaug, vaug, kwin, kwin, vwaug, vwaug, t_hi, t_lo, o_cmp, gates, onw)


ROUTE_E0 = N_GROUPS
U32 = jnp.uint32
HI_MASK = 0xFFFF0000


def _outproj_kernel(mixa_ref, yc_ref, wo_ref, x_ref, mod_ref, n2w_ref, wrh_ref, wrl_ref, br_ref, tri_ref,
                    x1_ref, h2p_ref, ri_ref, rw_ref, cnt_ref, carry_ref, *, tm):
    @pl.when((pl.program_id(0) == 0) & (pl.program_id(1) == 0))
    def _():
        carry_ref[...] = jnp.zeros_like(carry_ref)

    mod = mod_ref[0]
    g1 = mod[2:3]
    sh2 = mod[3:4]
    sc2 = mod[4:5]
    d_half = wo_ref.shape[0] // 2
    mix = _dot(mixa_ref[0], wo_ref[0:d_half]) + _dot(yc_ref[0], wo_ref[d_half:2 * d_half])
    x1 = x_ref[0] + g1 * mix
    x1_ref[0] = x1
    ms = jnp.mean(x1 * x1, axis=-1, keepdims=True)
    h2 = ((x1 * lax.rsqrt(ms + EPS)) * n2w_ref[...]) * (1.0 + sc2) + sh2

    dh = h2.shape[1] // 2
    lo_w = pltpu.bitcast(h2[:, :dh].astype(BF16).astype(F32), U32)
    hi_w = pltpu.bitcast(h2[:, dh:].astype(BF16).astype(F32), U32)
    h2p_ref[0] = lax.shift_right_logical(lo_w, jnp.uint32(16)) | (hi_w & jnp.uint32(HI_MASK))

    logits = _dot3(h2, wrh_ref[...], wrl_ref[...]) + br_ref[...]
    lane = lax.broadcasted_iota(I32, (tm, LANES), 1)
    glog = jnp.where(lane < N_GROUPS, logits, NEG)
    gmax = jnp.max(glog, axis=1, keepdims=True)
    g_p = 1.0 / jnp.sum(jnp.exp(glog - gmax), axis=1, keepdims=True)
    gidx = jnp.min(jnp.where(glog == gmax, lane, LANES), axis=1, keepdims=True)
    e_lo = ROUTE_E0 + EXPERTS_PER_GROUP * gidx
    el = jnp.where(lane >= e_lo, jnp.where(lane < e_lo + EXPERTS_PER_GROUP, logits, NEG), NEG)
    m1 = jnp.max(el, axis=1, keepdims=True)
    i1 = jnp.min(jnp.where(el == m1, lane, LANES), axis=1, keepdims=True)
    el2 = jnp.where(lane == i1, NEG, el)
    m2 = jnp.max(el2, axis=1, keepdims=True)
    i2 = jnp.min(jnp.where(el2 == m2, lane, LANES), axis=1, keepdims=True)
    e2 = jnp.exp(m2 - m1)
    w1 = g_p / (1.0 + e2)
    w2 = g_p * e2 / (1.0 + e2)

    oh1 = lane == i1
    oh2 = lane == i2
    both = jnp.where(oh1, 1.0, jnp.where(oh2, 1.0, 0.0))
    base = carry_ref[0:1] + _dot(tri_ref[...], both.astype(BF16))
    r1 = jnp.sum(jnp.where(oh1, base, 0.0), axis=1, keepdims=True)
    r2 = jnp.sum(jnp.where(oh2, base, 0.0), axis=1, keepdims=True)
    carry_ref[...] = carry_ref[...] + jnp.sum(both, axis=0, keepdims=True)
    cnt_ref[...] = carry_ref[...]

    ri_ref[0] = jnp.where(lane == 0, i1 - ROUTE_E0,
                          jnp.where(lane == 1, i2 - ROUTE_E0,
                                    jnp.where(lane == 2, r1.astype(I32),
                                              jnp.where(lane == 3, r2.astype(I32), 0))))
    rw_ref[0] = jnp.where(lane == 0, w1, jnp.where(lane == 1, w2, 0.0))


def _outproj(mixa, yconv, w_out, x, mod, norm2_w, w_group, b_group, w_expert, b_expert, tm=512):
    b, s, d = x.shape
    wr = jnp.pad(jnp.concatenate([w_group, w_expert], axis=1), ((0, 0), (0, LANES - N_GROUPS - N_EXPERTS)))
    wr_hi = wr.astype(BF16)
    wr_lo = (wr - wr_hi.astype(F32)).astype(BF16)
    br = jnp.pad(jnp.concatenate([b_group, b_expert]), (0, LANES - N_GROUPS - N_EXPERTS)).reshape(1, LANES)
    tri = jnp.asarray(np.tril(np.ones((tm, tm), np.float32), -1), dtype=BF16)
    full = lambda shape: pl.BlockSpec(shape, lambda bi, i: (0,) * len(shape))
    row = lambda w: pl.BlockSpec((1, tm, w), lambda bi, i: (bi, i, 0))
    return pl.pallas_call(
        functools.partial(_outproj_kernel, tm=tm),
        grid=(b, s // tm),
        in_specs=[row(D_ATTN), row(D_CONV), full((d, d)), row(d),
                  pl.BlockSpec((1, 6, d), lambda bi, i: (bi, 0, 0)),
                  full((1, d)), full((d, LANES)), full((d, LANES)), full((1, LANES)), full((tm, tm))],
        out_specs=[row(d), row(d // 2), row(LANES), row(LANES), full((8, LANES))],
        out_shape=[jax.ShapeDtypeStruct((b, s, d), F32),
                   jax.ShapeDtypeStruct((b, s, d // 2), U32),
                   jax.ShapeDtypeStruct((b, s, LANES), I32),
                   jax.ShapeDtypeStruct((b, s, LANES), F32),
                   jax.ShapeDtypeStruct((8, LANES), F32)],
        scratch_shapes=[pltpu.VMEM((8, LANES), F32)],
        compiler_params=pltpu.CompilerParams(
            dimension_semantics=("arbitrary", "arbitrary"), vmem_limit_bytes=VMEM_LIMIT),
        name="outproj",
    )(mixa, yconv, w_out.astype(BF16), x, mod, norm2_w.reshape(1, d), wr_hi, wr_lo, br, tri)


TOK_TILE = 256


def _dest(ps_ref, ri_ref, k, r):
    return ps_ref[ri_ref[0, k, r]] + ri_ref[0, 2 + k, r]


def _dispatch_kernel(ps_ref, ri_ref, h2p_ref, xd_in_ref, xd_ref, sem):
    del xd_in_ref

    def row_copy(r, dest):
        return pltpu.make_async_copy(h2p_ref.at[pl.ds(r, 1)], xd_ref.at[pl.ds(dest, 1)], sem)

    def issue(r, carry):
        for k in range(2):
            row_copy(r, _dest(ps_ref, ri_ref, k, r)).start()
        return carry

    lax.fori_loop(0, TOK_TILE, issue, 0)

    def drain(r, carry):
        for k in range(2):
            row_copy(r, _dest(ps_ref, ri_ref, k, r)).wait()
        return carry

    lax.fori_loop(0, TOK_TILE, drain, 0)


def _dispatch(pad_start, ri_t, h2p, n_rows):
    t, w = h2p.shape
    xd0 = jnp.zeros((n_rows, w), U32)
    return pl.pallas_call(
        _dispatch_kernel,
        grid_spec=pltpu.PrefetchScalarGridSpec(
            num_scalar_prefetch=1,
            grid=(t // TOK_TILE,),
            in_specs=[pl.BlockSpec((1, 4, TOK_TILE), lambda i, ps: (i, 0, 0), memory_space=pltpu.SMEM),
                      pl.BlockSpec((TOK_TILE, w), lambda i, ps: (i, 0)),
                      pl.BlockSpec(memory_space=pl.ANY)],
            out_specs=pl.BlockSpec(memory_space=pl.ANY),
            scratch_shapes=[pltpu.SemaphoreType.DMA(())]),
        out_shape=jax.ShapeDtypeStruct((n_rows, w), U32),
        input_output_aliases={3: 0},
        compiler_params=pltpu.CompilerParams(
            dimension_semantics=("arbitrary",), vmem_limit_bytes=VMEM_LIMIT, has_side_effects=True),
        name="dispatch",
    )(pad_start, ri_t, h2p, xd0)


def _experts_kernel(be_ref, nu_ref, xd_ref, w1_ref, w3_ref, w2_ref, y_ref, w1b_ref, w3b_ref, w2b_ref):
    i = pl.program_id(0)

    @pl.when((i == 0) | (be_ref[i] != be_ref[jnp.maximum(i - 1, 0)]))
    def _():
        w1b_ref[...] = w1_ref[0].astype(BF16)
        w3b_ref[...] = w3_ref[0].astype(BF16)
        w2b_ref[...] = w2_ref[0].astype(BF16)

    @pl.when(i < nu_ref[0])
    def _():
        word = xd_ref[...]
        xa = pltpu.bitcast(lax.shift_left(word, jnp.uint32(16)), F32).astype(BF16)
        xb = pltpu.bitcast(word & jnp.uint32(HI_MASK), F32).astype(BF16)
        dh = xa.shape[1]
        a = _dot(xa, w1b_ref[0:dh]) + _dot(xb, w1b_ref[dh:2 * dh])
        c = _dot(xa, w3b_ref[0:dh]) + _dot(xb, w3b_ref[dh:2 * dh])
        y_ref[...] = _dot((_silu(a) * c).astype(BF16), w2b_ref[...])

    @pl.when(i >= nu_ref[0])
    def _():
        y_ref[...] = jnp.zeros_like(y_ref)


def _experts(blk_expert, n_used, xd, w1, w3, w2):
    n_rows, dh = xd.shape
    _, d, de = w1.shape
    n_blocks = n_rows // MOE_BLOCK
    return pl.pallas_call(
        _experts_kernel,
        grid_spec=pltpu.PrefetchScalarGridSpec(
            num_scalar_prefetch=2,
            grid=(n_blocks,),
            in_specs=[pl.BlockSpec((MOE_BLOCK, dh), lambda i, be, nu: (i, 0)),
                      pl.BlockSpec((1, d, de), lambda i, be, nu: (be[i], 0, 0)),
                      pl.BlockSpec((1, d, de), lambda i, be, nu: (be[i], 0, 0)),
                      pl.BlockSpec((1, de, d), lambda i, be, nu: (be[i], 0, 0))],
            out_specs=pl.BlockSpec((MOE_BLOCK, d), lambda i, be, nu: (i, 0)),
            scratch_shapes=[pltpu.VMEM((d, de), BF16), pltpu.VMEM((d, de), BF16), pltpu.VMEM((de, d), BF16)]),
        out_shape=jax.ShapeDtypeStruct((n_rows, d), F32),
        compiler_params=pltpu.CompilerParams(
            dimension_semantics=("arbitrary",), vmem_limit_bytes=VMEM_LIMIT),
        name="experts",
    )(blk_expert, n_used, xd, w1, w3, w2)


def _combine_kernel(ps_ref, ri_ref, rw_ref, x1_ref, mod_ref, y_ref, o_ref, buf_ref, sem):
    def row_copy(k, r, dest):
        return pltpu.make_async_copy(y_ref.at[pl.ds(dest, 1)], buf_ref.at[k, pl.ds(r, 1)], sem)

    def issue(r, carry):
        for k in range(2):
            row_copy(k, r, _dest(ps_ref, ri_ref, k, r)).start()
        return carry

    lax.fori_loop(0, TOK_TILE, issue, 0)

    def drain(r, carry):
        for k in range(2):
            row_copy(k, r, _dest(ps_ref, ri_ref, k, r)).wait()
        return carry

    lax.fori_loop(0, TOK_TILE, drain, 0)
    w = rw_ref[...]
    g2 = mod_ref[0][5:6]
    o_ref[...] = x1_ref[...] + g2 * (w[:, 0:1] * buf_ref[0] + w[:, 1:2] * buf_ref[1])


def _combine(pad_start, ri_t, rw, x1, mod, y_disp, seq):
    t, d = x1.shape
    tiles_per_seq = seq // TOK_TILE
    return pl.pallas_call(
        _combine_kernel,
        grid_spec=pltpu.PrefetchScalarGridSpec(
            num_scalar_prefetch=1,
            grid=(t // TOK_TILE,),
            in_specs=[pl.BlockSpec((1, 4, TOK_TILE), lambda i, ps: (i, 0, 0), memory_space=pltpu.SMEM),
                      pl.BlockSpec((TOK_TILE, LANES), lambda i, ps: (i, 0)),
                      pl.BlockSpec((TOK_TILE, d), lambda i, ps: (i, 0)),
                      pl.BlockSpec((1, 6, d), lambda i, ps: (i // tiles_per_seq, 0, 0)),
                      pl.BlockSpec(memory_space=pl.ANY)],
            out_specs=pl.BlockSpec((TOK_TILE, d), lambda i, ps: (i, 0)),
            scratch_shapes=[pltpu.VMEM((2, TOK_TILE, d), F32), pltpu.SemaphoreType.DMA(())]),
        out_shape=jax.ShapeDtypeStruct((t, d), F32),
        compiler_params=pltpu.CompilerParams(
            dimension_semantics=("arbitrary",), vmem_limit_bytes=VMEM_LIMIT),
        name="combine",
    )(pad_start, ri_t, rw, x1, mod, y_disp)


def _moe(x1, h2p, ri, rw, cnt, mod, w1, w3, w2):
    b, s, d = x1.shape
    t = b * s
    a = 2 * t
    counts = cnt[0, ROUTE_E0:ROUTE_E0 + N_EXPERTS].astype(I32)
    padded = (counts + MOE_BLOCK - 1) // MOE_BLOCK * MOE_BLOCK
    pad_end = jnp.cumsum(padded)
    pad_start = (pad_end - padded).astype(I32)
    n_blocks = (a + N_EXPERTS * (MOE_BLOCK - 1)) // MOE_BLOCK + 1
    blk_expert = jnp.minimum(
        jnp.searchsorted(pad_end, jnp.arange(n_blocks) * MOE_BLOCK, side='right'), N_EXPERTS - 1).astype(I32)
    n_used = (pad_end[-1:] // MOE_BLOCK).astype(I32)
    ri_t = ri.reshape(t // TOK_TILE, TOK_TILE, LANES)[:, :, 0:4].transpose(0, 2, 1)
    xd = _dispatch(pad_start, ri_t, h2p.reshape(t, d // 2), n_blocks * MOE_BLOCK)
    y_disp = _experts(blk_expert, n_used, xd, w1, w3, w2)
    out = _combine(pad_start, ri_t, rw.reshape(t, LANES), x1.reshape(t, d), mod, y_disp, s)
    return out.reshape(b, s, d)


def kernel(x, c, w_ada, b_ada, norm1_w, w_in, q_norm_w, k_norm_w, cmp_pos_k, cmp_pos_v, cmp_k_w1, cmp_k_w2,
           cmp_v_w1, cmp_v_w2, conv_w, out_norm_w, w_out, rel_bias, norm2_w, w_group, b_group, w_expert,
           b_expert, w1, w3, w2):
    for l in range(w_ada.shape[0]):
        mod = _ada(c, w_ada[l], b_ada[l])
        q, kvc, kaug, vaug, kwin, vwaug, gates, yconv = _inproj(
            x, mod, norm1_w[l], w_in[l], q_norm_w[l], k_norm_w[l], conv_w[l], out_norm_w[l])
        rows = _kvc_rows(kvc)
        k_c = _compress(rows, 0, cmp_k_w1[l], cmp_k_w2[l], cmp_pos_k[l], k_norm_w[l, 0], True)
        v_c = _compress(rows, 1, cmp_v_w1[l], cmp_v_w2[l], cmp_pos_v[l], k_norm_w[l, 0], False)
        o_cmp, pen = _cmpsel(q, k_c, v_c, rel_bias)
        mixa = _slcwin(q, pen, kaug, vaug, kwin, vwaug, o_cmp, gates, rel_bias, out_norm_w[l])
        x1, h2p, ri, rw, cnt = _outproj(mixa, yconv, w_out[l], x, mod, norm2_w[l],
                                        w_group[l], b_group[l], w_expert[l], b_expert[l])
        x = _moe(x1, h2p, ri, rw, cnt, mod, w1[l], w3[l], w2[l])
    return x
```

```python
import functools
import math

import numpy as np
import jax
import jax.numpy as jnp
from jax import lax
from jax.experimental import pallas as pl
from jax.experimental.pallas import tpu as pltpu

F32 = jnp.float32
BF16 = jnp.bfloat16
I32 = jnp.int32

HEAD_DIM = 64
N_HEADS = 8
N_KV = 2
HPG = N_HEADS // N_KV
D_ATTN = N_HEADS * HEAD_DIM
D_CONV = 512
KV_DIM = N_KV * HEAD_DIM
CMP_LEN = 32
CMP_STRIDE = 16
CMP_HIDDEN = 256
SLC_LEN = 64
N_SELECT = 16
WINDOW = 512
N_BUCKETS = 32
MAX_DISTANCE = 128
N_GROUPS = 8
EXPERTS_PER_GROUP = 8
N_EXPERTS = N_GROUPS * EXPERTS_PER_GROUP
MOE_BLOCK = 128
FORCED_SCORE = 1e4
EPS = 1e-6

LANES = 128
QB = 512
ROWS_Q = HPG * QB
CMP_CHUNK = 128
PEN_BLOCKS = 128
HALF_KEYS = PEN_BLOCKS * SLC_LEN
NEG = -1e30
PICKED = -3e38
VMEM_LIMIT = 56 * 1024 * 1024


def _dot(a, b):
    return jnp.dot(a, b, preferred_element_type=F32)


def _dot_nt(a, b):
    return lax.dot_general(a, b, (((1,), (1,)), ((), ())), preferred_element_type=F32)


def _hi_lo(a):
    hi = a.astype(BF16)
    lo = (a - hi.astype(F32)).astype(BF16)
    return hi, lo


def _dot3(a, b_hi, b_lo):
    a_hi, a_lo = _hi_lo(a)
    return _dot(a_hi, b_hi) + (_dot(a_lo, b_hi) + _dot(a_hi, b_lo))


def _group_sumsq(v, gm):
    hi, lo = _hi_lo(v * v)
    return _dot(hi, gm) + _dot(lo, gm)


def _silu(v):
    return v * jax.nn.sigmoid(v)


def _ada_kernel(c_ref, w_ref, b_ref, o_ref):
    cond = _silu(c_ref[...])
    w_hi, w_lo = _hi_lo(w_ref[...])
    o_ref[...] = _dot3(cond, w_hi, w_lo) + b_ref[...]


def _ada(c, w_ada, b_ada):
    b, d = c.shape
    n = w_ada.shape[1]
    tn = n // 4
    c8 = jnp.pad(c, ((0, 8 - b), (0, 0)))
    out = pl.pallas_call(
        _ada_kernel,
        grid=(n // tn,),
        in_specs=[pl.BlockSpec((8, d), lambda i: (0, 0)),
                  pl.BlockSpec((d, tn), lambda i: (0, i)),
                  pl.BlockSpec((1, tn), lambda i: (0, i))],
        out_specs=pl.BlockSpec((8, tn), lambda i: (0, i)),
        out_shape=jax.ShapeDtypeStruct((8, n), F32),
        compiler_params=pltpu.CompilerParams(vmem_limit_bytes=VMEM_LIMIT),
        name="ada",
    )(c8, w_ada, b_ada.reshape(1, n))
    return out[:b].reshape(b, 6, d)


def _inproj_kernel(x_ref, xh_ref, mod_ref, n1w_ref, wq_ref, wkv_ref, wg_ref, wbcu_ref,
                   qnw_ref, knw_ref, convw_ref, onw_ref, gm_ref,
                   q_ref, kvc_ref, kaug_ref, vaug_ref, kwin_ref, vwaug_ref, gates_ref, yconv_ref,
                   ext_ref, *, tm):
    i = pl.program_id(1)
    mod = mod_ref[0]
    sh1 = mod[0:1]
    sc1 = mod[1:2]

    def norm_mod(v):
        ms = jnp.mean(v * v, axis=-1, keepdims=True)
        y = (v * lax.rsqrt(ms + EPS)) * n1w_ref[...]
        return (y * (1.0 + sc1) + sh1).astype(BF16)

    h = norm_mod(x_ref[0])
    gm = gm_ref[...]
    gm_kv = gm_ref[0:KV_DIM, 0:KV_DIM]

    q = _dot(h, wq_ref[...])
    qn = q * lax.rsqrt(_group_sumsq(q, gm) * (1.0 / HEAD_DIM) + EPS) * qnw_ref[...]
    qn = (qn * (HEAD_DIM ** -0.5)).astype(BF16)
    for hd in range(N_HEADS):
        q_ref[0, hd] = qn[:, hd * HEAD_DIM:(hd + 1) * HEAD_DIM]

    kv = _dot(h, wkv_ref[...])
    kvc_ref[0] = kv[:, 0:2 * KV_DIM]
    ks = kv[:, 2 * KV_DIM:3 * KV_DIM]
    vs = kv[:, 3 * KV_DIM:4 * KV_DIM]
    kw = kv[:, 4 * KV_DIM:5 * KV_DIM]
    vw = kv[:, 5 * KV_DIM:6 * KV_DIM]
    ksn = ks * lax.rsqrt(_group_sumsq(ks, gm_kv) * (1.0 / HEAD_DIM) + EPS) * knw_ref[0:1]
    kwn = kw * lax.rsqrt(_group_sumsq(kw, gm_kv) * (1.0 / HEAD_DIM) + EPS) * knw_ref[1:2]

    lane = lax.broadcasted_iota(I32, (tm, LANES), 1)
    row = lax.broadcasted_iota(I32, (tm, LANES), 0)
    blk = ((i * tm + row) // SLC_LEN) % PEN_BLOCKS
    onehot = jnp.where(lane == blk, 1.0, 0.0).astype(BF16)
    ones_col = jnp.where(lax.broadcasted_iota(I32, (tm, HEAD_DIM), 1) == 0, 1.0, 0.0).astype(BF16)
    zeros64 = jnp.zeros((tm, HEAD_DIM), BF16)
    for g in range(N_KV):
        sl = slice(g * HEAD_DIM, (g + 1) * HEAD_DIM)
        kaug_ref[0, g, :, 0:LANES] = onehot
        kaug_ref[0, g, :, LANES:LANES + HEAD_DIM] = ksn[:, sl].astype(BF16)
        kaug_ref[0, g, :, LANES + HEAD_DIM:2 * LANES] = zeros64
        vaug_ref[0, g, :, 0:HEAD_DIM] = vs[:, sl].astype(BF16)
        vaug_ref[0, g, :, HEAD_DIM:LANES] = ones_col
        kwin_ref[0, g] = kwn[:, sl].astype(BF16)
        vwaug_ref[0, g, :, 0:HEAD_DIM] = vw[:, sl].astype(BF16)
        vwaug_ref[0, g, :, HEAD_DIM:LANES] = ones_col

    gates_ref[0] = jax.nn.sigmoid(_dot(h, wg_ref[...]))

    bcu = _dot(h, wbcu_ref[...])
    bg = bcu[:, 0:D_CONV]
    cu = bcu[:, D_CONV:2 * D_CONV] * bcu[:, 2 * D_CONV:3 * D_CONV]
    hh = norm_mod(xh_ref[0])
    cuh = _dot(hh, wbcu_ref[:, D_CONV:2 * D_CONV]) * _dot(hh, wbcu_ref[:, 2 * D_CONV:3 * D_CONV])
    cuh = jnp.where(i > 0, cuh, 0.0)
    ext_ref[0:8] = cuh
    ext_ref[8:8 + tm] = cu
    cw = convw_ref[...]
    y = bg * (cw[0:1] * ext_ref[6:6 + tm] + cw[1:2] * ext_ref[7:7 + tm] + cw[2:3] * cu)
    yn = y * lax.rsqrt(_group_sumsq(y, gm) * (1.0 / HEAD_DIM) + EPS) * onw_ref[...]
    yconv_ref[0] = yn.astype(BF16)


def _inproj(x, mod, norm1_w, w_in, q_norm_w, k_norm_w, conv_w, out_norm_w, tm=512):
    b, s, d = x.shape
    o_kv = D_ATTN
    o_g = o_kv + 6 * KV_DIM
    o_b = o_g + 3 * N_HEADS
    wq = w_in[:, 0:o_kv].astype(BF16)
    wkv = w_in[:, o_kv:o_g].astype(BF16)
    wg = jnp.pad(w_in[:, o_g:o_b].reshape(d, N_KV, 3 * HPG),
                 ((0, 0), (0, 0), (0, LANES - 3 * HPG))).reshape(d, N_KV * LANES).astype(BF16)
    wbcu = w_in[:, o_b:].astype(BF16)
    qnw = jnp.tile(q_norm_w, N_HEADS).reshape(1, D_ATTN)
    knw = jnp.stack([jnp.tile(k_norm_w[1], N_KV), jnp.tile(k_norm_w[2], N_KV)])
    onw = out_norm_w[D_ATTN:].reshape(1, D_CONV)
    gidx = np.arange(D_ATTN) // HEAD_DIM
    gm = jnp.asarray((gidx[:, None] == gidx[None, :]).astype(np.float32), dtype=BF16)

    full = lambda shape: pl.BlockSpec(shape, lambda bi, i: (0,) * len(shape))
    outs = pl.pallas_call(
        functools.partial(_inproj_kernel, tm=tm),
        grid=(b, s // tm),
        in_specs=[pl.BlockSpec((1, tm, d), lambda bi, i: (bi, i, 0)),
                  pl.BlockSpec((1, 8, d), lambda bi, i: (bi, jnp.maximum(i * (tm // 8) - 1, 0), 0)),
                  pl.BlockSpec((1, 6, d), lambda bi, i: (bi, 0, 0)),
                  full((1, d)), full(wq.shape), full(wkv.shape), full(wg.shape), full(wbcu.shape),
                  full(qnw.shape), full(knw.shape), full(conv_w.shape), full(onw.shape), full(gm.shape)],
        out_specs=[pl.BlockSpec((1, N_HEADS, tm, HEAD_DIM), lambda bi, i: (bi, 0, i, 0)),
                   pl.BlockSpec((1, tm, 2 * KV_DIM), lambda bi, i: (bi, i, 0)),
                   pl.BlockSpec((1, N_KV, tm, 2 * LANES), lambda bi, i: (bi, 0, i, 0)),
                   pl.BlockSpec((1, N_KV, tm, LANES), lambda bi, i: (bi, 0, i, 0)),
                   pl.BlockSpec((1, N_KV, tm, HEAD_DIM), lambda bi, i: (bi, 0, i, 0)),
                   pl.BlockSpec((1, N_KV, tm, LANES), lambda bi, i: (bi, 0, i, 0)),
                   pl.BlockSpec((1, tm, N_KV * LANES), lambda bi, i: (bi, i, 0)),
                   pl.BlockSpec((1, tm, D_CONV), lambda bi, i: (bi, i, 0))],
        out_shape=[jax.ShapeDtypeStruct((b, N_HEADS, s, HEAD_DIM), BF16),
                   jax.ShapeDtypeStruct((b, s, 2 * KV_DIM), F32),
                   jax.ShapeDtypeStruct((b, N_KV, s, 2 * LANES), BF16),
                   jax.ShapeDtypeStruct((b, N_KV, s, LANES), BF16),
                   jax.ShapeDtypeStruct((b, N_KV, s, HEAD_DIM), BF16),
                   jax.ShapeDtypeStruct((b, N_KV, s, LANES), BF16),
                   jax.ShapeDtypeStruct((b, s, N_KV * LANES), F32),
                   jax.ShapeDtypeStruct((b, s, D_CONV), BF16)],
        scratch_shapes=[pltpu.VMEM((tm + 8, D_CONV), F32)],
        compiler_params=pltpu.CompilerParams(
            dimension_semantics=("parallel", "arbitrary"), vmem_limit_bytes=VMEM_LIMIT),
        name="inproj",
    )(x, x, mod, norm1_w.reshape(1, d), wq, wkv, wg, wbcu, qnw, knw, conv_w, onw, gm)
    return outs


def _compress_kernel(r_ref, w1_ref, w2_ref, pos_ref, nw_ref, o_ref, *, norm):
    r = r_ref[0, 0, 0]
    nr = r.shape[0]
    half = (CMP_LEN // 2) * HEAD_DIM
    w1a = w1_ref[0:half, :]
    w1b = w1_ref[half:2 * half, :]
    pos = pos_ref[...]
    cst = (_dot(jnp.broadcast_to(pos[0:1], (8, half)).astype(BF16), w1a)
           + _dot(jnp.broadcast_to(pos[1:2], (8, half)).astype(BF16), w1b))[0:1]
    upper = pltpu.roll(_dot(r, w1b), nr - 1, 0)
    hid = _dot(r, w1a) + upper + cst
    out = _dot(_silu(hid).astype(BF16), w2_ref[...])
    if norm:
        ms = jnp.mean(out * out, axis=-1, keepdims=True)
        out = out * lax.rsqrt(ms + EPS) * nw_ref[...]
    o_ref[0, 0] = out.astype(BF16)


def _compress(rows, kv_index, w1, w2, pos_emb, norm_w, norm):
    b, _, g, nr, width = rows.shape
    return pl.pallas_call(
        functools.partial(_compress_kernel, norm=norm),
        grid=(b, g),
        in_specs=[pl.BlockSpec((1, 1, 1, nr, width), lambda bi, gi: (bi, kv_index, gi, 0, 0)),
                  pl.BlockSpec(w1.shape, lambda bi, gi: (0, 0)),
                  pl.BlockSpec(w2.shape, lambda bi, gi: (0, 0)),
                  pl.BlockSpec((2, width), lambda bi, gi: (0, 0)),
                  pl.BlockSpec((1, HEAD_DIM), lambda bi, gi: (0, 0))],
        out_specs=pl.BlockSpec((1, 1, nr, HEAD_DIM), lambda bi, gi: (bi, gi, 0, 0)),
        out_shape=jax.ShapeDtypeStruct((b, g, nr, HEAD_DIM), BF16),
        compiler_params=pltpu.CompilerParams(vmem_limit_bytes=VMEM_LIMIT),
        name="compress_k" if norm else "compress_v",
    )(rows, w1.astype(BF16), w2.astype(BF16), pos_emb.reshape(2, width), norm_w.reshape(1, HEAD_DIM))


def _kvc_rows(kvc):
    b, s, _ = kvc.shape
    r = kvc.reshape(b, s, 2, N_KV, HEAD_DIM).transpose(0, 2, 3, 1, 4)
    return r.reshape(b, 2, N_KV, s // CMP_STRIDE, CMP_STRIDE * HEAD_DIM).astype(BF16)


def _bucket_np(dist):
    n = np.maximum(dist, 0)
    max_exact = N_BUCKETS // 2
    nf = np.maximum(n, 1).astype(np.float32)
    large = max_exact + (np.log(nf / np.float32(max_exact)) / np.float32(math.log(MAX_DISTANCE / max_exact))
                         * np.float32(N_BUCKETS - max_exact)).astype(np.int32)
    large = np.minimum(large, N_BUCKETS - 1)
    return np.where(n < max_exact, n, large)


_BUCKET_START = [int(np.argmax(_bucket_np(np.arange(MAX_DISTANCE + 1)) >= k)) for k in range(N_BUCKETS)]
CMP_DIST0 = CMP_STRIDE * CMP_CHUNK - (CMP_LEN - 1) - QB


def _bias_tables_kernel(rbt_ref, thi_ref, tlo_ref, tc_ref):
    head = pl.program_id(0) * HPG + pl.program_id(1)
    far = rbt_ref[head, N_BUCKETS - 1]

    def table(dist):
        v = jnp.zeros(dist.shape, F32)
        for k in range(N_BUCKETS - 2, -1, -1):
            v = jnp.where(dist < _BUCKET_START[k + 1], rbt_ref[head, k] - far, v)
        return jnp.where(dist >= 0, v, NEG)

    def dist(shape, row_mul, col_mul, off):
        return (lax.broadcasted_iota(I32, shape, 0) * row_mul
                + lax.broadcasted_iota(I32, shape, 1) * col_mul + off)

    thi_ref[0] = table(dist((QB, QB), 1, -1, 0))
    tlo_ref[0] = table(dist((QB, LANES), 1, -1, LANES))
    tc_ref[0] = table(dist((QB, CMP_CHUNK), 1, -CMP_STRIDE, CMP_DIST0))


def _bias_tables(rel_bias):
    return pl.pallas_call(
        _bias_tables_kernel,
        grid=(N_KV, HPG),
        in_specs=[pl.BlockSpec(memory_space=pltpu.SMEM)],
        out_specs=[pl.BlockSpec((1, QB, QB), lambda g, h: (g, h, 0)),
                   pl.BlockSpec((1, QB, LANES), lambda g, h: (g, h, 0)),
                   pl.BlockSpec((1, QB, CMP_CHUNK), lambda g, h: (g, h, 0))],
        out_shape=[jax.ShapeDtypeStruct((N_KV, ROWS_Q, QB), F32),
                   jax.ShapeDtypeStruct((N_KV, ROWS_Q, LANES), F32),
                   jax.ShapeDtypeStruct((N_KV, ROWS_Q, CMP_CHUNK), F32)],
        name="bias_tables",
    )(rel_bias.T)


def _cmpsel_kernel(q_ref, kc_ref, vc_ref, tab_ref, ovt_ref, ocmp_ref, pen_ref,
                   m_ref, l_ref, acc_ref, imp_ref, score_ref, *, nsbp, n_sel):
    j = pl.program_id(2)
    q4 = q_ref[0].reshape(ROWS_Q, HEAD_DIM)
    nq16 = QB // CMP_STRIDE
    n_chunks = (nq16 * (j + 1) - 1) // CMP_CHUNK + 1
    ni = lax.broadcasted_iota(I32, (1, CMP_CHUNK), 1)

    def chunk_start(m):
        return pl.multiple_of(nq16 * (j + 1) - CMP_CHUNK * m, nq16)

    def logits(m):
        n0p = chunk_start(m)
        s = _dot_nt(q4, kc_ref[0, 0, pl.ds(n0p, CMP_CHUNK), :])
        return jnp.where(ni + n0p >= CMP_CHUNK, s, NEG)

    s0 = logits(0) + tab_ref[0]
    m_ref[...] = s0
    l_ref[...] = jnp.ones_like(l_ref)

    def stats(m, carry):
        s = logits(m)
        m_prev = m_ref[...]
        m_new = jnp.maximum(m_prev, s)
        l_ref[...] = l_ref[...] * jnp.exp(m_prev - m_new) + jnp.exp(s - m_new)
        m_ref[...] = m_new
        return carry

    lax.fori_loop(1, n_chunks, stats, 0)

    m_lane = m_ref[...]
    m_fin = jnp.broadcast_to(jnp.max(m_lane, axis=1, keepdims=True), (ROWS_Q, LANES))
    l_fin = jnp.sum(l_ref[...] * jnp.exp(m_lane - m_fin), axis=1, keepdims=True)
    inv = jnp.where(m_fin > 0.5 * NEG, 1.0 / l_fin, 0.0)
    m_ref[...] = m_fin
    l_ref[...] = inv
    acc_ref[...] = jnp.zeros_like(acc_ref)
    imp_ref[...] = jnp.zeros_like(imp_ref)

    def accumulate(m, s):
        n0p = chunk_start(m)
        p = jnp.exp(s - m_ref[...]) * l_ref[...]
        acc_ref[...] += _dot(p.astype(BF16), vc_ref[0, 0, pl.ds(n0p, CMP_CHUNK), :])
        psum = p[0:QB]
        for hd in range(1, HPG):
            psum = psum + p[hd * QB:(hd + 1) * QB]
        hi, lo = _hi_lo(psum)
        contrib = _dot_nt(ovt_ref[...], hi) + _dot_nt(ovt_ref[...], lo)
        r0 = pl.multiple_of(n0p // 4, 8)
        imp_ref[pl.ds(r0, OV_ROWS), :] += contrib

    accumulate(0, s0)

    def second(m, carry):
        accumulate(m, logits(m))
        return carry

    lax.fori_loop(1, n_chunks, second, 0)
    ocmp_ref[0, 0] = acc_ref[...].reshape(HPG, QB, HEAD_DIM)

    imp = imp_ref[IMP_PAD:IMP_PAD + nsbp, :]
    srow = lax.broadcasted_iota(I32, (nsbp, QB), 0)
    qcol = lax.broadcasted_iota(I32, (nsbp, QB), 1)
    cur = (QB // SLC_LEN) * j + qcol // SLC_LEN
    forced = (srow == 0) | (srow == cur) | (srow == cur - 1)
    score_ref[...] = jnp.where(forced, -1.0, jnp.where(srow <= cur, imp, -1.0))

    def pick(r, carry):
        sc = score_ref[...]
        mx = jnp.max(sc, axis=0, keepdims=True)
        idx = jnp.min(jnp.where(sc == mx, srow, nsbp), axis=0, keepdims=True)
        score_ref[...] = jnp.where(srow == idx, PICKED, sc)
        return carry

    lax.fori_loop(0, max(n_sel - 3, 0), pick, 0)
    pen_other = jnp.where(srow <= cur, jnp.where(score_ref[...] == PICKED, 0.0, NEG), NEG)
    pen_t = jnp.where(forced, 0.0, pen_other).T
    for hf in range(nsbp // PEN_BLOCKS):
        pen_ref[0, 0, hf] = pen_t[:, hf * PEN_BLOCKS:(hf + 1) * PEN_BLOCKS].astype(BF16)


IMP_PAD = CMP_CHUNK // 4
OV_ROWS = 40


def _overlap_t():
    n = np.arange(CMP_CHUNK)
    s = np.arange(OV_ROWS)
    first = (n * CMP_STRIDE) // SLC_LEN
    last = (n * CMP_STRIDE + CMP_LEN - 1) // SLC_LEN
    ov = (s[:, None] >= first[None, :]) & (s[:, None] <= last[None, :])
    return jnp.asarray(ov.astype(np.float32), dtype=BF16)


def _cmpsel(q, k_c, v_c, tab):
    b, _, s, _ = q.shape
    nr = k_c.shape[2]
    nq = s // QB
    nsb = s // SLC_LEN
    nh = -(-nsb // PEN_BLOCKS)
    nsbp = nh * PEN_BLOCKS
    n_sel = min(N_SELECT, nsb)
    pad = ((0, 0), (0, 0), (CMP_CHUNK, 0), (0, 0))
    kc_p = jnp.pad(k_c, pad)
    vc_p = jnp.pad(v_c, pad)
    ovt = _overlap_t()
    return pl.pallas_call(
        functools.partial(_cmpsel_kernel, nsbp=nsbp, n_sel=n_sel),
        grid=(b, N_KV, nq),
        in_specs=[pl.BlockSpec((1, HPG, QB, HEAD_DIM), lambda bi, g, j: (bi, g, j, 0)),
                  pl.BlockSpec((1, 1, nr + CMP_CHUNK, HEAD_DIM), lambda bi, g, j: (bi, g, 0, 0)),
                  pl.BlockSpec((1, 1, nr + CMP_CHUNK, HEAD_DIM), lambda bi, g, j: (bi, g, 0, 0)),
                  pl.BlockSpec((1, ROWS_Q, CMP_CHUNK), lambda bi, g, j: (g, 0, 0)),
                  pl.BlockSpec(ovt.shape, lambda bi, g, j: (0, 0))],
        out_specs=[pl.BlockSpec((1, 1, HPG, QB, HEAD_DIM), lambda bi, g, j: (bi, g, 0, j, 0)),
                   pl.BlockSpec((1, 1, nh, QB, PEN_BLOCKS), lambda bi, g, j: (bi, g, 0, j, 0))],
        out_shape=[jax.ShapeDtypeStruct((b, N_KV, HPG, s, HEAD_DIM), F32),
                   jax.ShapeDtypeStruct((b, N_KV, nh, s, PEN_BLOCKS), BF16)],
        scratch_shapes=[pltpu.VMEM((ROWS_Q, LANES), F32), pltpu.VMEM((ROWS_Q, LANES), F32),
                        pltpu.VMEM((ROWS_Q, HEAD_DIM), F32),
                        pltpu.VMEM((IMP_PAD + nsbp + 8, QB), F32),
                        pltpu.VMEM((nsbp, QB), F32)],
        compiler_params=pltpu.CompilerParams(
            dimension_semantics=("parallel", "parallel", "arbitrary"), vmem_limit_bytes=VMEM_LIMIT),
        name="cmpsel",
    )(q, kc_p, vc_p, tab, ovt)


def _slcwin_kernel(q_ref, pen_ref, kaug_ref, vaug_ref, kwp_ref, kwc_ref, vwp_ref, vwc_ref,
                   thi_ref, tlo_ref, ocmp_ref, gates_ref, onw_ref, o_ref,
                   lhs_ref, m_ref, acc_ref, *, nh):
    j = pl.program_id(2)
    s0 = pl.multiple_of(j * QB, QB)
    q4 = q_ref[0].reshape(ROWS_Q, HEAD_DIM)
    near = QB - LANES

    lhs_ref[:, LANES:LANES + HEAD_DIM] = q4
    lhs_ref[:, LANES + HEAD_DIM:2 * LANES] = jnp.zeros((ROWS_Q, HEAD_DIM), BF16)

    def set_pen(half):
        p = pen_ref[0, 0, half]
        for hd in range(HPG):
            lhs_ref[hd * QB:(hd + 1) * QB, 0:LANES] = p

    def reset():
        m_ref[...] = jnp.full_like(m_ref, NEG)
        acc_ref[...] = jnp.zeros_like(acc_ref)

    def update(s, v):
        m_prev = m_ref[...]
        m_new = jnp.maximum(m_prev, jnp.max(s, axis=1, keepdims=True))
        p = jnp.exp(s - jnp.concatenate([m_new] * (s.shape[1] // LANES), axis=1))
        acc_ref[...] = acc_ref[...] * jnp.exp(m_prev - m_new) + _dot(p.astype(BF16), v)
        m_ref[...] = m_new

    def add_near_bias(s):
        return jnp.concatenate([s[:, :near], s[:, near:] + tlo_ref[0]], axis=1)

    def result():
        acc = acc_ref[...]
        return acc[:, 0:HEAD_DIM] / acc[:, HEAD_DIM:HEAD_DIM + 1]

    reset()
    half_hi = s0 // HALF_KEYS
    set_pen(half_hi)
    update(_dot_nt(lhs_ref[...], kaug_ref[0, 0, pl.ds(s0, QB), :]) + thi_ref[0],
           vaug_ref[0, 0, pl.ds(s0, QB), :])

    @pl.when(j > 0)
    def _():
        sl = pl.multiple_of(s0 - QB, QB)
        half_lo = sl // HALF_KEYS

        @pl.when(half_lo != half_hi)
        def _():
            set_pen(half_lo)

        s = _dot_nt(lhs_ref[...], kaug_ref[0, 0, pl.ds(sl, QB), :])
        update(add_near_bias(s), vaug_ref[0, 0, pl.ds(sl, QB), :])

    n_far = jnp.maximum(j - 1, 0)
    chunks_per_half = HALF_KEYS // QB
    for hf in range(nh):
        lo = hf * chunks_per_half
        hi = jnp.minimum((hf + 1) * chunks_per_half, n_far)

        @pl.when(hi > lo)
        def _():
            set_pen(hf)

            def far(c, carry):
                k0 = pl.multiple_of(c * QB, QB)
                update(_dot_nt(lhs_ref[...], kaug_ref[0, 0, pl.ds(k0, QB), :]),
                       vaug_ref[0, 0, pl.ds(k0, QB), :])
                return carry

            lax.fori_loop(lo, hi, far, 0)

    o_slc = result()

    reset()
    update(_dot_nt(q4, kwc_ref[0, 0]) + thi_ref[0], vwc_ref[0, 0])

    @pl.when(j > 0)
    def _():
        s = _dot_nt(q4, kwp_ref[0, 0])
        qi = lax.broadcasted_iota(I32, (ROWS_Q, QB), 0) & (QB - 1)
        kk = lax.broadcasted_iota(I32, (ROWS_Q, QB), 1)
        s = jnp.where(kk > qi, s, NEG)
        update(add_near_bias(s), vwp_ref[0, 0])

    o_win = result()

    o_cmp = ocmp_ref[0, 0].reshape(ROWS_Q, HEAD_DIM)
    gt = gates_ref[0]
    outs = []
    for hd in range(HPG):
        rs = slice(hd * QB, (hd + 1) * QB)
        o = (gt[:, 3 * hd:3 * hd + 1] * o_cmp[rs] + gt[:, 3 * hd + 1:3 * hd + 2] * o_slc[rs]
             + gt[:, 3 * hd + 2:3 * hd + 3] * o_win[rs])
        ms = jnp.mean(o * o, axis=-1, keepdims=True)
        outs.append((o * lax.rsqrt(ms + EPS) * onw_ref[0, hd:hd + 1, :]).astype(BF16))
    o_ref[0] = jnp.concatenate(outs, axis=1)


def _slcwin(q, pen, kaug, vaug, kwin, vwaug, o_cmp, gates, t_hi, t_lo, out_norm_w):
    assert QB == WINDOW
    b, _, s, _ = q.shape
    nq = s // QB
    nh = pen.shape[2]
    onw = out_norm_w[:D_ATTN].reshape(N_KV, HPG, HEAD_DIM)
    prev = lambda bi, g, j: (bi, g, jnp.maximum(j - 1, 0), 0)
    curr = lambda bi, g, j: (bi, g, j, 0)
    return pl.pallas_call(
        functools.partial(_slcwin_kernel, nh=nh),
        grid=(b, N_KV, nq),
        in_specs=[pl.BlockSpec((1, HPG, QB, HEAD_DIM), lambda bi, g, j: (bi, g, j, 0)),
                  pl.BlockSpec((1, 1, nh, QB, PEN_BLOCKS), lambda bi, g, j: (bi, g, 0, j, 0)),
                  pl.BlockSpec((1, 1, s, 2 * LANES), lambda bi, g, j: (bi, g, 0, 0)),
                  pl.BlockSpec((1, 1, s, LANES), lambda bi, g, j: (bi, g, 0, 0)),
                  pl.BlockSpec((1, 1, QB, HEAD_DIM), prev),
                  pl.BlockSpec((1, 1, QB, HEAD_DIM), curr),
                  pl.BlockSpec((1, 1, QB, LANES), prev),
                  pl.BlockSpec((1, 1, QB, LANES), curr),
                  pl.BlockSpec((1, ROWS_Q, QB), lambda bi, g, j: (g, 0, 0)),
                  pl.BlockSpec((1, ROWS_Q, LANES), lambda bi, g, j: (g, 0, 0)),
                  pl.BlockSpec((1, 1, HPG, QB, HEAD_DIM), lambda bi, g, j: (bi, g, 0, j, 0)),
                  pl.BlockSpec((1, QB, LANES), lambda bi, g, j: (bi, j, g)),
                  pl.BlockSpec((1, HPG, HEAD_DIM), lambda bi, g, j: (g, 0, 0))],
        out_specs=pl.BlockSpec((1, QB, HPG * HEAD_DIM), lambda bi, g, j: (bi, j, g)),
        out_shape=jax.ShapeDtypeStruct((b, s, D_ATTN), BF16),
        scratch_shapes=[pltpu.VMEM((ROWS_Q, 2 * LANES), BF16),
                        pltpu.VMEM((ROWS_Q, LANES), F32),
                        pltpu.VMEM((ROWS_Q, LANES), F32)],
        compiler_params=pltpu.CompilerParams(
            dimension_semantics=("parallel", "parallel", "arbitrary"), vmem_limit_bytes=VMEM_LIMIT),
        name="slcwin",
    )(q, pen, kaug, vaug, kwin, kwin, vwaug, vwaug, t_hi, t_lo, o_cmp, gates, onw)


ROUTE_E0 = N_GROUPS
U32 = jnp.uint32
HI_MASK = 0xFFFF0000


def _outproj_kernel(mixa_ref, yc_ref, wo_ref, x_ref, mod_ref, n2w_ref, wrh_ref, wrl_ref, br_ref, tri_ref,
                    x1_ref, h2p_ref, ri_ref, rw_ref, cnt_ref, carry_ref, *, tm):
    @pl.when((pl.program_id(0) == 0) & (pl.program_id(1) == 0))
    def _():
        carry_ref[...] = jnp.zeros_like(carry_ref)

    mod = mod_ref[0]
    g1 = mod[2:3]
    sh2 = mod[3:4]
    sc2 = mod[4:5]
    d_half = wo_ref.shape[0] // 2
    mix = _dot(mixa_ref[0], wo_ref[0:d_half]) + _dot(yc_ref[0], wo_ref[d_half:2 * d_half])
    x1 = x_ref[0] + g1 * mix
    x1_ref[0] = x1
    ms = jnp.mean(x1 * x1, axis=-1, keepdims=True)
    h2 = ((x1 * lax.rsqrt(ms + EPS)) * n2w_ref[...]) * (1.0 + sc2) + sh2

    dh = h2.shape[1] // 2
    lo_w = pltpu.bitcast(h2[:, :dh].astype(BF16).astype(F32), U32)
    hi_w = pltpu.bitcast(h2[:, dh:].astype(BF16).astype(F32), U32)
    h2p_ref[0] = lax.shift_right_logical(lo_w, jnp.uint32(16)) | (hi_w & jnp.uint32(HI_MASK))

    logits = _dot3(h2, wrh_ref[...], wrl_ref[...]) + br_ref[...]
    lane = lax.broadcasted_iota(I32, (tm, LANES), 1)
    glog = jnp.where(lane < N_GROUPS, logits, NEG)
    gmax = jnp.max(glog, axis=1, keepdims=True)
    g_p = 1.0 / jnp.sum(jnp.exp(glog - gmax), axis=1, keepdims=True)
    gidx = jnp.min(jnp.where(glog == gmax, lane, LANES), axis=1, keepdims=True)
    e_lo = ROUTE_E0 + EXPERTS_PER_GROUP * gidx
    el = jnp.where(lane >= e_lo, jnp.where(lane < e_lo + EXPERTS_PER_GROUP, logits, NEG), NEG)
    m1 = jnp.max(el, axis=1, keepdims=True)
    i1 = jnp.min(jnp.where(el == m1, lane, LANES), axis=1, keepdims=True)
    el2 = jnp.where(lane == i1, NEG, el)
    m2 = jnp.max(el2, axis=1, keepdims=True)
    i2 = jnp.min(jnp.where(el2 == m2, lane, LANES), axis=1, keepdims=True)
    e2 = jnp.exp(m2 - m1)
    w1 = g_p / (1.0 + e2)
    w2 = g_p * e2 / (1.0 + e2)

    oh1 = lane == i1
    oh2 = lane == i2
    both = jnp.where(oh1, 1.0, jnp.where(oh2, 1.0, 0.0))
    base = carry_ref[0:1] + _dot(tri_ref[...], both.astype(BF16))
    r1 = jnp.sum(jnp.where(oh1, base, 0.0), axis=1, keepdims=True)
    r2 = jnp.sum(jnp.where(oh2, base, 0.0), axis=1, keepdims=True)
    carry_ref[...] = carry_ref[...] + jnp.sum(both, axis=0, keepdims=True)
    cnt_ref[...] = carry_ref[...]

    ri_ref[0] = jnp.where(lane == 0, i1 - ROUTE_E0,
                          jnp.where(lane == 1, i2 - ROUTE_E0,
                                    jnp.where(lane == 2, r1.astype(I32),
                                              jnp.where(lane == 3, r2.astype(I32), 0))))
    rw_ref[0] = jnp.where(lane == 0, w1, jnp.where(lane == 1, w2, 0.0))


def _outproj(mixa, yconv, w_out, x, mod, norm2_w, w_group, b_group, w_expert, b_expert, tm=512):
    b, s, d = x.shape
    wr = jnp.pad(jnp.concatenate([w_group, w_expert], axis=1), ((0, 0), (0, LANES - N_GROUPS - N_EXPERTS)))
    wr_hi = wr.astype(BF16)
    wr_lo = (wr - wr_hi.astype(F32)).astype(BF16)
    br = jnp.pad(jnp.concatenate([b_group, b_expert]), (0, LANES - N_GROUPS - N_EXPERTS)).reshape(1, LANES)
    tri = jnp.asarray(np.tril(np.ones((tm, tm), np.float32), -1), dtype=BF16)
    full = lambda shape: pl.BlockSpec(shape, lambda bi, i: (0,) * len(shape))
    row = lambda w: pl.BlockSpec((1, tm, w), lambda bi, i: (bi, i, 0))
    return pl.pallas_call(
        functools.partial(_outproj_kernel, tm=tm),
        grid=(b, s // tm),
        in_specs=[row(D_ATTN), row(D_CONV), full((d, d)), row(d),
                  pl.BlockSpec((1, 6, d), lambda bi, i: (bi, 0, 0)),
                  full((1, d)), full((d, LANES)), full((d, LANES)), full((1, LANES)), full((tm, tm))],
        out_specs=[row(d), row(d // 2), row(LANES), row(LANES), full((8, LANES))],
        out_shape=[jax.ShapeDtypeStruct((b, s, d), F32),
                   jax.ShapeDtypeStruct((b, s, d // 2), U32),
                   jax.ShapeDtypeStruct((b, s, LANES), I32),
                   jax.ShapeDtypeStruct((b, s, LANES), F32),
                   jax.ShapeDtypeStruct((8, LANES), F32)],
        scratch_shapes=[pltpu.VMEM((8, LANES), F32)],
        compiler_params=pltpu.CompilerParams(
            dimension_semantics=("arbitrary", "arbitrary"), vmem_limit_bytes=VMEM_LIMIT),
        name="outproj",
    )(mixa, yconv, w_out.astype(BF16), x, mod, norm2_w.reshape(1, d), wr_hi, wr_lo, br, tri)


TOK_TILE = 256


def _dest(ps_ref, ri_ref, k, r):
    return ps_ref[ri_ref[0, k, r]] + ri_ref[0, 2 + k, r]


def _dispatch_kernel(ps_ref, ri_ref, h2p_ref, xd_in_ref, xd_ref, sem):
    del xd_in_ref

    def row_copy(r, dest):
        return pltpu.make_async_copy(h2p_ref.at[pl.ds(r, 1)], xd_ref.at[pl.ds(dest, 1)], sem)

    def issue(r, carry):
        for k in range(2):
            row_copy(r, _dest(ps_ref, ri_ref, k, r)).start()
        return carry

    lax.fori_loop(0, TOK_TILE, issue, 0)

    def drain(r, carry):
        for k in range(2):
            row_copy(r, _dest(ps_ref, ri_ref, k, r)).wait()
        return carry

    lax.fori_loop(0, TOK_TILE, drain, 0)


def _dispatch(pad_start, ri_t, h2p, n_rows):
    t, w = h2p.shape
    xd0 = jnp.zeros((n_rows, w), U32)
    return pl.pallas_call(
        _dispatch_kernel,
        grid_spec=pltpu.PrefetchScalarGridSpec(
            num_scalar_prefetch=1,
            grid=(t // TOK_TILE,),
            in_specs=[pl.BlockSpec((1, 4, TOK_TILE), lambda i, ps: (i, 0, 0), memory_space=pltpu.SMEM),
                      pl.BlockSpec((TOK_TILE, w), lambda i, ps: (i, 0)),
                      pl.BlockSpec(memory_space=pl.ANY)],
            out_specs=pl.BlockSpec(memory_space=pl.ANY),
            scratch_shapes=[pltpu.SemaphoreType.DMA(())]),
        out_shape=jax.ShapeDtypeStruct((n_rows, w), U32),
        input_output_aliases={3: 0},
        compiler_params=pltpu.CompilerParams(
            dimension_semantics=("arbitrary",), vmem_limit_bytes=VMEM_LIMIT, has_side_effects=True),
        name="dispatch",
    )(pad_start, ri_t, h2p, xd0)


def _experts_kernel(be_ref, nu_ref, xd_ref, w1_ref, w3_ref, w2_ref, y_ref, w1b_ref, w3b_ref, w2b_ref):
    i = pl.program_id(0)

    @pl.when((i == 0) | (be_ref[i] != be_ref[jnp.maximum(i - 1, 0)]))
    def _():
        w1b_ref[...] = w1_ref[0].astype(BF16)
        w3b_ref[...] = w3_ref[0].astype(BF16)
        w2b_ref[...] = w2_ref[0].astype(BF16)

    @pl.when(i < nu_ref[0])
    def _():
        word = xd_ref[...]
        xa = pltpu.bitcast(lax.shift_left(word, jnp.uint32(16)), F32).astype(BF16)
        xb = pltpu.bitcast(word & jnp.uint32(HI_MASK), F32).astype(BF16)
        dh = xa.shape[1]
        a = _dot(xa, w1b_ref[0:dh]) + _dot(xb, w1b_ref[dh:2 * dh])
        c = _dot(xa, w3b_ref[0:dh]) + _dot(xb, w3b_ref[dh:2 * dh])
        y_ref[...] = _dot((_silu(a) * c).astype(BF16), w2b_ref[...])

    @pl.when(i >= nu_ref[0])
    def _():
        y_ref[...] = jnp.zeros_like(y_ref)


def _experts(blk_expert, n_used, xd, w1, w3, w2):
    n_rows, dh = xd.shape
    _, d, de = w1.shape
    n_blocks = n_rows // MOE_BLOCK
    return pl.pallas_call(
        _experts_kernel,
        grid_spec=pltpu.PrefetchScalarGridSpec(
            num_scalar_prefetch=2,
            grid=(n_blocks,),
            in_specs=[pl.BlockSpec((MOE_BLOCK, dh), lambda i, be, nu: (i, 0)),
                      pl.BlockSpec((1, d, de), lambda i, be, nu: (be[i], 0, 0)),
                      pl.BlockSpec((1, d, de), lambda i, be, nu: (be[i], 0, 0)),
                      pl.BlockSpec((1, de, d), lambda i, be, nu: (be[i], 0, 0))],
            out_specs=pl.BlockSpec((MOE_BLOCK, d), lambda i, be, nu: (i, 0)),
            scratch_shapes=[pltpu.VMEM((d, de), BF16), pltpu.VMEM((d, de), BF16), pltpu.VMEM((de, d), BF16)]),
        out_shape=jax.ShapeDtypeStruct((n_rows, d), F32),
        compiler_params=pltpu.CompilerParams(
            dimension_semantics=("arbitrary",), vmem_limit_bytes=VMEM_LIMIT),
        name="experts",
    )(blk_expert, n_used, xd, w1, w3, w2)


def _combine_kernel(ps_ref, ri_ref, rw_ref, x1_ref, mod_ref, y_ref, o_ref, buf_ref, sem):
    def row_copy(k, r, dest):
        return pltpu.make_async_copy(y_ref.at[pl.ds(dest, 1)], buf_ref.at[k, pl.ds(r, 1)], sem)

    def issue(r, carry):
        for k in range(2):
            row_copy(k, r, _dest(ps_ref, ri_ref, k, r)).start()
        return carry

    lax.fori_loop(0, TOK_TILE, issue, 0)

    def drain(r, carry):
        for k in range(2):
            row_copy(k, r, _dest(ps_ref, ri_ref, k, r)).wait()
        return carry

    lax.fori_loop(0, TOK_TILE, drain, 0)
    w = rw_ref[...]
    g2 = mod_ref[0][5:6]
    o_ref[...] = x1_ref[...] + g2 * (w[:, 0:1] * buf_ref[0] + w[:, 1:2] * buf_ref[1])


def _combine(pad_start, ri_t, rw, x1, mod, y_disp, seq):
    t, d = x1.shape
    tiles_per_seq = seq // TOK_TILE
    return pl.pallas_call(
        _combine_kernel,
        grid_spec=pltpu.PrefetchScalarGridSpec(
            num_scalar_prefetch=1,
            grid=(t // TOK_TILE,),
            in_specs=[pl.BlockSpec((1, 4, TOK_TILE), lambda i, ps: (i, 0, 0), memory_space=pltpu.SMEM),
                      pl.BlockSpec((TOK_TILE, LANES), lambda i, ps: (i, 0)),
                      pl.BlockSpec((TOK_TILE, d), lambda i, ps: (i, 0)),
                      pl.BlockSpec((1, 6, d), lambda i, ps: (i // tiles_per_seq, 0, 0)),
                      pl.BlockSpec(memory_space=pl.ANY)],
            out_specs=pl.BlockSpec((TOK_TILE, d), lambda i, ps: (i, 0)),
            scratch_shapes=[pltpu.VMEM((2, TOK_TILE, d), F32), pltpu.SemaphoreType.DMA(())]),
        out_shape=jax.ShapeDtypeStruct((t, d), F32),
        compiler_params=pltpu.CompilerParams(
            dimension_semantics=("arbitrary",), vmem_limit_bytes=VMEM_LIMIT),
        name="combine",
    )(pad_start, ri_t, rw, x1, mod, y_disp)


def _moe(x1, h2p, ri, rw, cnt, mod, w1, w3, w2):
    b, s, d = x1.shape
    t = b * s
    a = 2 * t
    counts = cnt[0, ROUTE_E0:ROUTE_E0 + N_EXPERTS].astype(I32)
    padded = (counts + MOE_BLOCK - 1) // MOE_BLOCK * MOE_BLOCK
    pad_end = jnp.cumsum(padded)
    pad_start = (pad_end - padded).astype(I32)
    n_blocks = (a + N_EXPERTS * (MOE_BLOCK - 1)) // MOE_BLOCK + 1
    blk_start = jnp.arange(n_blocks, dtype=I32) * MOE_BLOCK
    blk_expert = jnp.minimum(
        jnp.sum((pad_end[None, :] <= blk_start[:, None]).astype(I32), axis=1), N_EXPERTS - 1)
    n_used = (pad_end[-1:] // MOE_BLOCK).astype(I32)
    ri_t = ri.reshape(t // TOK_TILE, TOK_TILE, LANES)[:, :, 0:4].transpose(0, 2, 1)
    xd = _dispatch(pad_start, ri_t, h2p.reshape(t, d // 2), n_blocks * MOE_BLOCK)
    y_disp = _experts(blk_expert, n_used, xd, w1, w3, w2)
    out = _combine(pad_start, ri_t, rw.reshape(t, LANES), x1.reshape(t, d), mod, y_disp, s)
    return out.reshape(b, s, d)


def kernel(x, c, w_ada, b_ada, norm1_w, w_in, q_norm_w, k_norm_w, cmp_pos_k, cmp_pos_v, cmp_k_w1, cmp_k_w2,
           cmp_v_w1, cmp_v_w2, conv_w, out_norm_w, w_out, rel_bias, norm2_w, w_group, b_group, w_expert,
           b_expert, w1, w3, w2):
    t_hi, t_lo, t_c = _bias_tables(rel_bias)
    for l in range(w_ada.shape[0]):
        mod = _ada(c, w_ada[l], b_ada[l])
        q, kvc, kaug, vaug, kwin, vwaug, gates, yconv = _inproj(
            x, mod, norm1_w[l], w_in[l], q_norm_w[l], k_norm_w[l], conv_w[l], out_norm_w[l])
        rows = _kvc_rows(kvc)
        k_c = _compress(rows, 0, cmp_k_w1[l], cmp_k_w2[l], cmp_pos_k[l], k_norm_w[l, 0], True)
        v_c = _compress(rows, 1, cmp_v_w1[l], cmp_v_w2[l], cmp_pos_v[l], k_norm_w[l, 0], False)
        o_cmp, pen = _cmpsel(q, k_c, v_c, t_c)
        mixa = _slcwin(q, pen, kaug, vaug, kwin, vwaug, o_cmp, gates, t_hi, t_lo, out_norm_w[l])
        x1, h2p, ri, rw, cnt = _outproj(mixa, yconv, w_out[l], x, mod, norm2_w[l],
                                        w_group[l], b_group[l], w_expert[l], b_expert[l])
        x = _moe(x1, h2p, ri, rw, cnt, mod, w1[l], w3[l], w2[l])
    return x
```

```python
import functools
import math

import numpy as np
import jax
import jax.numpy as jnp
from jax import lax
from jax.experimental import pallas as pl
from jax.experimental.pallas import tpu as pltpu

F32 = jnp.float32
BF16 = jnp.bfloat16
I32 = jnp.int32

HEAD_DIM = 64
N_HEADS = 8
N_KV = 2
HPG = N_HEADS // N_KV
D_ATTN = N_HEADS * HEAD_DIM
D_CONV = 512
KV_DIM = N_KV * HEAD_DIM
CMP_LEN = 32
CMP_STRIDE = 16
CMP_HIDDEN = 256
SLC_LEN = 64
N_SELECT = 16
WINDOW = 512
N_BUCKETS = 32
MAX_DISTANCE = 128
N_GROUPS = 8
EXPERTS_PER_GROUP = 8
N_EXPERTS = N_GROUPS * EXPERTS_PER_GROUP
MOE_BLOCK = 256
FORCED_SCORE = 1e4
EPS = 1e-6

LANES = 128
QB = 512
ROWS_Q = HPG * QB
CMP_CHUNK = 256
PEN_BLOCKS = 128
HALF_KEYS = PEN_BLOCKS * SLC_LEN
NEG = -1e30
PICKED = -3e38
VMEM_LIMIT = 56 * 1024 * 1024


def _dot(a, b):
    return jnp.dot(a, b, preferred_element_type=F32)


def _dot_nt(a, b):
    return lax.dot_general(a, b, (((1,), (1,)), ((), ())), preferred_element_type=F32)


def _hi_lo(a):
    hi = a.astype(BF16)
    lo = (a - hi.astype(F32)).astype(BF16)
    return hi, lo


def _dot3(a, b_hi, b_lo):
    a_hi, a_lo = _hi_lo(a)
    return _dot(a_hi, b_hi) + (_dot(a_lo, b_hi) + _dot(a_hi, b_lo))


def _group_sumsq(v, gm):
    hi, lo = _hi_lo(v * v)
    return _dot(hi, gm) + _dot(lo, gm)


def _silu(v):
    return v * jax.nn.sigmoid(v)


def _ada_kernel(c_ref, w_ref, b_ref, o_ref):
    cond = _silu(c_ref[...])
    w_hi, w_lo = _hi_lo(w_ref[...])
    o_ref[...] = _dot3(cond, w_hi, w_lo) + b_ref[...]


def _ada(c, w_ada, b_ada):
    b, d = c.shape
    n = w_ada.shape[1]
    tn = n // 4
    c8 = jnp.pad(c, ((0, 8 - b), (0, 0)))
    out = pl.pallas_call(
        _ada_kernel,
        grid=(n // tn,),
        in_specs=[pl.BlockSpec((8, d), lambda i: (0, 0)),
                  pl.BlockSpec((d, tn), lambda i: (0, i)),
                  pl.BlockSpec((1, tn), lambda i: (0, i))],
        out_specs=pl.BlockSpec((8, tn), lambda i: (0, i)),
        out_shape=jax.ShapeDtypeStruct((8, n), F32),
        compiler_params=pltpu.CompilerParams(vmem_limit_bytes=VMEM_LIMIT),
        name="ada",
    )(c8, w_ada, b_ada.reshape(1, n))
    return out[:b].reshape(b, 6, d)


def _inproj_kernel(x_ref, xh_ref, mod_ref, n1w_ref, wq_ref, wkv_ref, wg_ref, wbcu_ref,
                   qnw_ref, knw_ref, convw_ref, onw_ref, gm_ref,
                   q_ref, kvc_ref, kaug_ref, vaug_ref, kwin_ref, vwaug_ref, gates_ref, yconv_ref,
                   ext_ref, *, tm):
    i = pl.program_id(1)
    mod = mod_ref[0]
    sh1 = mod[0:1]
    sc1 = mod[1:2]

    def norm_mod(v):
        ms = jnp.mean(v * v, axis=-1, keepdims=True)
        y = (v * lax.rsqrt(ms + EPS)) * n1w_ref[...]
        return (y * (1.0 + sc1) + sh1).astype(BF16)

    h = norm_mod(x_ref[0])
    gm = gm_ref[...]
    gm_kv = gm_ref[0:KV_DIM, 0:KV_DIM]

    q = _dot(h, wq_ref[...])
    qn = q * lax.rsqrt(_group_sumsq(q, gm) * (1.0 / HEAD_DIM) + EPS) * qnw_ref[...]
    qn = (qn * (HEAD_DIM ** -0.5)).astype(BF16)
    for hd in range(N_HEADS):
        q_ref[0, hd] = qn[:, hd * HEAD_DIM:(hd + 1) * HEAD_DIM]

    kv = _dot(h, wkv_ref[...])
    kvc_ref[0] = kv[:, 0:2 * KV_DIM]
    ks = kv[:, 2 * KV_DIM:3 * KV_DIM]
    vs = kv[:, 3 * KV_DIM:4 * KV_DIM]
    kw = kv[:, 4 * KV_DIM:5 * KV_DIM]
    vw = kv[:, 5 * KV_DIM:6 * KV_DIM]
    ksn = ks * lax.rsqrt(_group_sumsq(ks, gm_kv) * (1.0 / HEAD_DIM) + EPS) * knw_ref[0:1]
    kwn = kw * lax.rsqrt(_group_sumsq(kw, gm_kv) * (1.0 / HEAD_DIM) + EPS) * knw_ref[1:2]

    lane = lax.broadcasted_iota(I32, (tm, LANES), 1)
    row = lax.broadcasted_iota(I32, (tm, LANES), 0)
    blk = ((i * tm + row) // SLC_LEN) % PEN_BLOCKS
    onehot = jnp.where(lane == blk, 1.0, 0.0).astype(BF16)
    ones_col = jnp.where(lax.broadcasted_iota(I32, (tm, HEAD_DIM), 1) == 0, 1.0, 0.0).astype(BF16)
    zeros64 = jnp.zeros((tm, HEAD_DIM), BF16)
    for g in range(N_KV):
        sl = slice(g * HEAD_DIM, (g + 1) * HEAD_DIM)
        kaug_ref[0, g, :, 0:LANES] = onehot
        kaug_ref[0, g, :, LANES:LANES + HEAD_DIM] = ksn[:, sl].astype(BF16)
        kaug_ref[0, g, :, LANES + HEAD_DIM:2 * LANES] = zeros64
        vaug_ref[0, g, :, 0:HEAD_DIM] = vs[:, sl].astype(BF16)
        vaug_ref[0, g, :, HEAD_DIM:LANES] = ones_col
        kwin_ref[0, g] = kwn[:, sl].astype(BF16)
        vwaug_ref[0, g, :, 0:HEAD_DIM] = vw[:, sl].astype(BF16)
        vwaug_ref[0, g, :, HEAD_DIM:LANES] = ones_col

    gates_ref[0] = jax.nn.sigmoid(_dot(h, wg_ref[...]))

    bcu = _dot(h, wbcu_ref[...])
    bg = bcu[:, 0:D_CONV]
    cu = bcu[:, D_CONV:2 * D_CONV] * bcu[:, 2 * D_CONV:3 * D_CONV]
    hh = norm_mod(xh_ref[0])
    cuh = _dot(hh, wbcu_ref[:, D_CONV:2 * D_CONV]) * _dot(hh, wbcu_ref[:, 2 * D_CONV:3 * D_CONV])
    cuh = jnp.where(i > 0, cuh, 0.0)
    ext_ref[0:8] = cuh
    ext_ref[8:8 + tm] = cu
    cw = convw_ref[...]
    y = bg * (cw[0:1] * ext_ref[6:6 + tm] + cw[1:2] * ext_ref[7:7 + tm] + cw[2:3] * cu)
    yn = y * lax.rsqrt(_group_sumsq(y, gm) * (1.0 / HEAD_DIM) + EPS) * onw_ref[...]
    yconv_ref[0] = yn.astype(BF16)


def _inproj(x, mod, norm1_w, w_in, q_norm_w, k_norm_w, conv_w, out_norm_w, tm=512):
    b, s, d = x.shape
    o_kv = D_ATTN
    o_g = o_kv + 6 * KV_DIM
    o_b = o_g + 3 * N_HEADS
    wq = w_in[:, 0:o_kv].astype(BF16)
    wkv = w_in[:, o_kv:o_g].astype(BF16)
    wg = jnp.pad(w_in[:, o_g:o_b].reshape(d, N_KV, 3 * HPG),
                 ((0, 0), (0, 0), (0, LANES - 3 * HPG))).reshape(d, N_KV * LANES).astype(BF16)
    wbcu = w_in[:, o_b:].astype(BF16)
    qnw = jnp.tile(q_norm_w, N_HEADS).reshape(1, D_ATTN)
    knw = jnp.stack([jnp.tile(k_norm_w[1], N_KV), jnp.tile(k_norm_w[2], N_KV)])
    onw = out_norm_w[D_ATTN:].reshape(1, D_CONV)
    gidx = np.arange(D_ATTN) // HEAD_DIM
    gm = jnp.asarray((gidx[:, None] == gidx[None, :]).astype(np.float32), dtype=BF16)

    full = lambda shape: pl.BlockSpec(shape, lambda bi, i: (0,) * len(shape))
    outs = pl.pallas_call(
        functools.partial(_inproj_kernel, tm=tm),
        grid=(b, s // tm),
        in_specs=[pl.BlockSpec((1, tm, d), lambda bi, i: (bi, i, 0)),
                  pl.BlockSpec((1, 8, d), lambda bi, i: (bi, jnp.maximum(i * (tm // 8) - 1, 0), 0)),
                  pl.BlockSpec((1, 6, d), lambda bi, i: (bi, 0, 0)),
                  full((1, d)), full(wq.shape), full(wkv.shape), full(wg.shape), full(wbcu.shape),
                  full(qnw.shape), full(knw.shape), full(conv_w.shape), full(onw.shape), full(gm.shape)],
        out_specs=[pl.BlockSpec((1, N_HEADS, tm, HEAD_DIM), lambda bi, i: (bi, 0, i, 0)),
                   pl.BlockSpec((1, tm, 2 * KV_DIM), lambda bi, i: (bi, i, 0)),
                   pl.BlockSpec((1, N_KV, tm, 2 * LANES), lambda bi, i: (bi, 0, i, 0)),
                   pl.BlockSpec((1, N_KV, tm, LANES), lambda bi, i: (bi, 0, i, 0)),
                   pl.BlockSpec((1, N_KV, tm, HEAD_DIM), lambda bi, i: (bi, 0, i, 0)),
                   pl.BlockSpec((1, N_KV, tm, LANES), lambda bi, i: (bi, 0, i, 0)),
                   pl.BlockSpec((1, tm, N_KV * LANES), lambda bi, i: (bi, i, 0)),
                   pl.BlockSpec((1, tm, D_CONV), lambda bi, i: (bi, i, 0))],
        out_shape=[jax.ShapeDtypeStruct((b, N_HEADS, s, HEAD_DIM), BF16),
                   jax.ShapeDtypeStruct((b, s, 2 * KV_DIM), F32),
                   jax.ShapeDtypeStruct((b, N_KV, s, 2 * LANES), BF16),
                   jax.ShapeDtypeStruct((b, N_KV, s, LANES), BF16),
                   jax.ShapeDtypeStruct((b, N_KV, s, HEAD_DIM), BF16),
                   jax.ShapeDtypeStruct((b, N_KV, s, LANES), BF16),
                   jax.ShapeDtypeStruct((b, s, N_KV * LANES), F32),
                   jax.ShapeDtypeStruct((b, s, D_CONV), BF16)],
        scratch_shapes=[pltpu.VMEM((tm + 8, D_CONV), F32)],
        compiler_params=pltpu.CompilerParams(
            dimension_semantics=("parallel", "arbitrary"), vmem_limit_bytes=VMEM_LIMIT),
        name="inproj",
    )(x, x, mod, norm1_w.reshape(1, d), wq, wkv, wg, wbcu, qnw, knw, conv_w, onw, gm)
    return outs


def _compress_kernel(r_ref, w1_ref, w2_ref, pos_ref, nw_ref, o_ref, *, norm):
    r = r_ref[0, 0, 0]
    nr = r.shape[0]
    half = (CMP_LEN // 2) * HEAD_DIM
    w1a = w1_ref[0:half, :]
    w1b = w1_ref[half:2 * half, :]
    pos = pos_ref[...]
    cst = (_dot(jnp.broadcast_to(pos[0:1], (8, half)).astype(BF16), w1a)
           + _dot(jnp.broadcast_to(pos[1:2], (8, half)).astype(BF16), w1b))[0:1]
    upper = pltpu.roll(_dot(r, w1b), nr - 1, 0)
    hid = _dot(r, w1a) + upper + cst
    out = _dot(_silu(hid).astype(BF16), w2_ref[...])
    if norm:
        ms = jnp.mean(out * out, axis=-1, keepdims=True)
        out = out * lax.rsqrt(ms + EPS) * nw_ref[...]
    o_ref[0, 0] = out.astype(BF16)


def _compress(rows, kv_index, w1, w2, pos_emb, norm_w, norm):
    b, _, g, nr, width = rows.shape
    return pl.pallas_call(
        functools.partial(_compress_kernel, norm=norm),
        grid=(b, g),
        in_specs=[pl.BlockSpec((1, 1, 1, nr, width), lambda bi, gi: (bi, kv_index, gi, 0, 0)),
                  pl.BlockSpec(w1.shape, lambda bi, gi: (0, 0)),
                  pl.BlockSpec(w2.shape, lambda bi, gi: (0, 0)),
                  pl.BlockSpec((2, width), lambda bi, gi: (0, 0)),
                  pl.BlockSpec((1, HEAD_DIM), lambda bi, gi: (0, 0))],
        out_specs=pl.BlockSpec((1, 1, nr, HEAD_DIM), lambda bi, gi: (bi, gi, 0, 0)),
        out_shape=jax.ShapeDtypeStruct((b, g, nr, HEAD_DIM), BF16),
        compiler_params=pltpu.CompilerParams(vmem_limit_bytes=VMEM_LIMIT),
        name="compress_k" if norm else "compress_v",
    )(rows, w1.astype(BF16), w2.astype(BF16), pos_emb.reshape(2, width), norm_w.reshape(1, HEAD_DIM))


def _kvc_rows(kvc):
    b, s, _ = kvc.shape
    r = kvc.reshape(b, s, 2, N_KV, HEAD_DIM).transpose(0, 2, 3, 1, 4)
    return r.reshape(b, 2, N_KV, s // CMP_STRIDE, CMP_STRIDE * HEAD_DIM).astype(BF16)


def _bucket_np(dist):
    n = np.maximum(dist, 0)
    max_exact = N_BUCKETS // 2
    nf = np.maximum(n, 1).astype(np.float32)
    large = max_exact + (np.log(nf / np.float32(max_exact)) / np.float32(math.log(MAX_DISTANCE / max_exact))
                         * np.float32(N_BUCKETS - max_exact)).astype(np.int32)
    large = np.minimum(large, N_BUCKETS - 1)
    return np.where(n < max_exact, n, large)


_BUCKET_START = [int(np.argmax(_bucket_np(np.arange(MAX_DISTANCE + 1)) >= k)) for k in range(N_BUCKETS)]
CMP_DIST0 = CMP_STRIDE * CMP_CHUNK - (CMP_LEN - 1) - QB


def _bias_tables_kernel(rbt_ref, thi_ref, tlo_ref, tc_ref):
    head = pl.program_id(0) * HPG + pl.program_id(1)
    far = rbt_ref[head, N_BUCKETS - 1]

    def table(dist):
        v = jnp.zeros(dist.shape, F32)
        for k in range(N_BUCKETS - 2, -1, -1):
            v = jnp.where(dist < _BUCKET_START[k + 1], rbt_ref[head, k] - far, v)
        return jnp.where(dist >= 0, v, NEG)

    def dist(shape, row_mul, col_mul, off):
        return (lax.broadcasted_iota(I32, shape, 0) * row_mul
                + lax.broadcasted_iota(I32, shape, 1) * col_mul + off)

    thi_ref[0] = table(dist((QB, QB), 1, -1, 0))
    tlo_ref[0] = table(dist((QB, LANES), 1, -1, LANES))
    tc_ref[0] = table(dist((QB, CMP_CHUNK), 1, -CMP_STRIDE, CMP_DIST0))


def _bias_tables(rel_bias):
    return pl.pallas_call(
        _bias_tables_kernel,
        grid=(N_KV, HPG),
        in_specs=[pl.BlockSpec(memory_space=pltpu.SMEM)],
        out_specs=[pl.BlockSpec((1, QB, QB), lambda g, h: (g, h, 0)),
                   pl.BlockSpec((1, QB, LANES), lambda g, h: (g, h, 0)),
                   pl.BlockSpec((1, QB, CMP_CHUNK), lambda g, h: (g, h, 0))],
        out_shape=[jax.ShapeDtypeStruct((N_KV, ROWS_Q, QB), F32),
                   jax.ShapeDtypeStruct((N_KV, ROWS_Q, LANES), F32),
                   jax.ShapeDtypeStruct((N_KV, ROWS_Q, CMP_CHUNK), F32)],
        name="bias_tables",
    )(rel_bias.T)


def _cmpsel_kernel(q_ref, kc_ref, vc_ref, tab_ref, ovt_ref, ocmp_ref, pen_ref,
                   m_ref, l_ref, acc_ref, imp_ref, score_ref, *, nsbp, n_sel):
    j = pl.program_id(2)
    q4 = q_ref[0].reshape(ROWS_Q, HEAD_DIM)
    nq16 = QB // CMP_STRIDE
    n_chunks = (nq16 * (j + 1) - 1) // CMP_CHUNK + 1
    ni = lax.broadcasted_iota(I32, (1, CMP_CHUNK), 1)

    def chunk_start(m):
        return pl.multiple_of(nq16 * (j + 1) - CMP_CHUNK * m, nq16)

    def logits(m):
        n0p = chunk_start(m)
        s = _dot_nt(q4, kc_ref[0, 0, pl.ds(n0p, CMP_CHUNK), :])
        return jnp.where(ni + n0p >= CMP_CHUNK, s, NEG)

    s0 = logits(0) + tab_ref[0]
    m_ref[...] = s0
    l_ref[...] = jnp.ones_like(l_ref)

    def stats(m, carry):
        s = logits(m)
        m_prev = m_ref[...]
        m_new = jnp.maximum(m_prev, s)
        l_ref[...] = l_ref[...] * jnp.exp(m_prev - m_new) + jnp.exp(s - m_new)
        m_ref[...] = m_new
        return carry

    lax.fori_loop(1, n_chunks, stats, 0)

    m_lane = m_ref[...]
    m_fin = jnp.broadcast_to(jnp.max(m_lane, axis=1, keepdims=True), (ROWS_Q, CMP_CHUNK))
    l_fin = jnp.sum(l_ref[...] * jnp.exp(m_lane - m_fin), axis=1, keepdims=True)
    inv = jnp.where(m_fin > 0.5 * NEG, 1.0 / l_fin, 0.0)
    m_ref[...] = m_fin
    l_ref[...] = inv
    acc_ref[...] = jnp.zeros_like(acc_ref)
    imp_ref[...] = jnp.zeros_like(imp_ref)

    def accumulate(m, s):
        n0p = chunk_start(m)
        p = jnp.exp(s - m_ref[...]) * l_ref[...]
        acc_ref[...] += _dot(p.astype(BF16), vc_ref[0, 0, pl.ds(n0p, CMP_CHUNK), :])
        psum = p[0:QB]
        for hd in range(1, HPG):
            psum = psum + p[hd * QB:(hd + 1) * QB]
        hi, lo = _hi_lo(psum)
        contrib = _dot_nt(ovt_ref[...], hi) + _dot_nt(ovt_ref[...], lo)
        r0 = pl.multiple_of(n0p // 4, 8)
        imp_ref[pl.ds(r0, OV_ROWS), :] += contrib

    accumulate(0, s0)

    def second(m, carry):
        accumulate(m, logits(m))
        return carry

    lax.fori_loop(1, n_chunks, second, 0)
    ocmp_ref[0, 0] = acc_ref[...].reshape(HPG, QB, HEAD_DIM)

    imp = imp_ref[IMP_PAD:IMP_PAD + nsbp, :]
    srow = lax.broadcasted_iota(I32, (nsbp, QB), 0)
    qcol = lax.broadcasted_iota(I32, (nsbp, QB), 1)
    cur = (QB // SLC_LEN) * j + qcol // SLC_LEN
    forced = (srow == 0) | (srow == cur) | (srow == cur - 1)
    score_ref[...] = jnp.where(forced, -1.0, jnp.where(srow <= cur, imp, -1.0))

    def pick(r, carry):
        sc = score_ref[...]
        mx = jnp.max(sc, axis=0, keepdims=True)
        idx = jnp.min(jnp.where(sc == mx, srow, nsbp), axis=0, keepdims=True)
        score_ref[...] = jnp.where(srow == idx, PICKED, sc)
        return carry

    lax.fori_loop(0, max(n_sel - 3, 0), pick, 0)
    pen_other = jnp.where(srow <= cur, jnp.where(score_ref[...] == PICKED, 0.0, NEG), NEG)
    pen_t = jnp.where(forced, 0.0, pen_other).T
    for hf in range(nsbp // PEN_BLOCKS):
        pen_ref[0, 0, hf] = pen_t[:, hf * PEN_BLOCKS:(hf + 1) * PEN_BLOCKS].astype(BF16)


IMP_PAD = CMP_CHUNK // 4
OV_ROWS = IMP_PAD + 8


def _overlap_t():
    n = np.arange(CMP_CHUNK)
    s = np.arange(OV_ROWS)
    first = (n * CMP_STRIDE) // SLC_LEN
    last = (n * CMP_STRIDE + CMP_LEN - 1) // SLC_LEN
    ov = (s[:, None] >= first[None, :]) & (s[:, None] <= last[None, :])
    return jnp.asarray(ov.astype(np.float32), dtype=BF16)


def _cmpsel(q, k_c, v_c, tab):
    b, _, s, _ = q.shape
    nr = k_c.shape[2]
    nq = s // QB
    nsb = s // SLC_LEN
    nh = -(-nsb // PEN_BLOCKS)
    nsbp = nh * PEN_BLOCKS
    n_sel = min(N_SELECT, nsb)
    pad = ((0, 0), (0, 0), (CMP_CHUNK, 0), (0, 0))
    kc_p = jnp.pad(k_c, pad)
    vc_p = jnp.pad(v_c, pad)
    ovt = _overlap_t()
    return pl.pallas_call(
        functools.partial(_cmpsel_kernel, nsbp=nsbp, n_sel=n_sel),
        grid=(b, N_KV, nq),
        in_specs=[pl.BlockSpec((1, HPG, QB, HEAD_DIM), lambda bi, g, j: (bi, g, j, 0)),
                  pl.BlockSpec((1, 1, nr + CMP_CHUNK, HEAD_DIM), lambda bi, g, j: (bi, g, 0, 0)),
                  pl.BlockSpec((1, 1, nr + CMP_CHUNK, HEAD_DIM), lambda bi, g, j: (bi, g, 0, 0)),
                  pl.BlockSpec((1, ROWS_Q, CMP_CHUNK), lambda bi, g, j: (g, 0, 0)),
                  pl.BlockSpec(ovt.shape, lambda bi, g, j: (0, 0))],
        out_specs=[pl.BlockSpec((1, 1, HPG, QB, HEAD_DIM), lambda bi, g, j: (bi, g, 0, j, 0)),
                   pl.BlockSpec((1, 1, nh, QB, PEN_BLOCKS), lambda bi, g, j: (bi, g, 0, j, 0))],
        out_shape=[jax.ShapeDtypeStruct((b, N_KV, HPG, s, HEAD_DIM), F32),
                   jax.ShapeDtypeStruct((b, N_KV, nh, s, PEN_BLOCKS), BF16)],
        scratch_shapes=[pltpu.VMEM((ROWS_Q, CMP_CHUNK), F32), pltpu.VMEM((ROWS_Q, CMP_CHUNK), F32),
                        pltpu.VMEM((ROWS_Q, HEAD_DIM), F32),
                        pltpu.VMEM((IMP_PAD + nsbp + 8, QB), F32),
                        pltpu.VMEM((nsbp, QB), F32)],
        compiler_params=pltpu.CompilerParams(
            dimension_semantics=("parallel", "parallel", "arbitrary"), vmem_limit_bytes=VMEM_LIMIT),
        name="cmpsel",
    )(q, kc_p, vc_p, tab, ovt)


def _slcwin_kernel(q_ref, pen_ref, kaug_ref, vaug_ref, kwp_ref, kwc_ref, vwp_ref, vwc_ref,
                   thi_ref, tlo_ref, ocmp_ref, gates_ref, onw_ref, o_ref,
                   lhs_ref, m_ref, acc_ref, *, nh):
    j = pl.program_id(2)
    s0 = pl.multiple_of(j * QB, QB)
    q4 = q_ref[0].reshape(ROWS_Q, HEAD_DIM)
    near = QB - LANES

    lhs_ref[:, LANES:LANES + HEAD_DIM] = q4
    lhs_ref[:, LANES + HEAD_DIM:2 * LANES] = jnp.zeros((ROWS_Q, HEAD_DIM), BF16)

    def set_pen(half):
        p = pen_ref[0, 0, half]
        for hd in range(HPG):
            lhs_ref[hd * QB:(hd + 1) * QB, 0:LANES] = p

    def reset():
        m_ref[...] = jnp.full_like(m_ref, NEG)
        acc_ref[...] = jnp.zeros_like(acc_ref)

    def update(s, v):
        m_prev = m_ref[...]
        m_new = jnp.maximum(m_prev, jnp.max(s, axis=1, keepdims=True))
        p = jnp.exp(s - jnp.concatenate([m_new] * (s.shape[1] // LANES), axis=1))
        acc_ref[...] = acc_ref[...] * jnp.exp(m_prev - m_new) + _dot(p.astype(BF16), v)
        m_ref[...] = m_new

    def add_near_bias(s):
        return jnp.concatenate([s[:, :near], s[:, near:] + tlo_ref[0]], axis=1)

    def result():
        acc = acc_ref[...]
        return acc[:, 0:HEAD_DIM] / acc[:, HEAD_DIM:HEAD_DIM + 1]

    reset()
    half_hi = s0 // HALF_KEYS
    set_pen(half_hi)
    update(_dot_nt(lhs_ref[...], kaug_ref[0, 0, pl.ds(s0, QB), :]) + thi_ref[0],
           vaug_ref[0, 0, pl.ds(s0, QB), :])

    @pl.when(j > 0)
    def _():
        sl = pl.multiple_of(s0 - QB, QB)
        half_lo = sl // HALF_KEYS

        @pl.when(half_lo != half_hi)
        def _():
            set_pen(half_lo)

        s = _dot_nt(lhs_ref[...], kaug_ref[0, 0, pl.ds(sl, QB), :])
        update(add_near_bias(s), vaug_ref[0, 0, pl.ds(sl, QB), :])

    n_far = jnp.maximum(j - 1, 0)
    chunks_per_half = HALF_KEYS // QB
    assert chunks_per_half % 2 == 0

    def far_chunk(c):
        k0 = pl.multiple_of(c * QB, QB)
        update(_dot_nt(lhs_ref[...], kaug_ref[0, 0, pl.ds(k0, QB), :]),
               vaug_ref[0, 0, pl.ds(k0, QB), :])

    def enter_half(c):
        @pl.when(c % chunks_per_half == 0)
        def _():
            set_pen(c // chunks_per_half)

    def far_pair(i2, carry):
        c = 2 * i2
        enter_half(c)
        far_chunk(c)
        far_chunk(c + 1)
        return carry

    lax.fori_loop(0, n_far // 2, far_pair, 0)

    @pl.when(n_far % 2 == 1)
    def _():
        enter_half(n_far - 1)
        far_chunk(n_far - 1)

    o_slc = result()

    reset()
    update(_dot_nt(q4, kwc_ref[0, 0]) + thi_ref[0], vwc_ref[0, 0])

    @pl.when(j > 0)
    def _():
        s = _dot_nt(q4, kwp_ref[0, 0])
        qi = lax.broadcasted_iota(I32, (ROWS_Q, QB), 0) & (QB - 1)
        kk = lax.broadcasted_iota(I32, (ROWS_Q, QB), 1)
        s = jnp.where(kk > qi, s, NEG)
        update(add_near_bias(s), vwp_ref[0, 0])

    o_win = result()

    o_cmp = ocmp_ref[0, 0].reshape(ROWS_Q, HEAD_DIM)
    gt = gates_ref[0]
    outs = []
    for hd in range(HPG):
        rs = slice(hd * QB, (hd + 1) * QB)
        o = (gt[:, 3 * hd:3 * hd + 1] * o_cmp[rs] + gt[:, 3 * hd + 1:3 * hd + 2] * o_slc[rs]
             + gt[:, 3 * hd + 2:3 * hd + 3] * o_win[rs])
        ms = jnp.mean(o * o, axis=-1, keepdims=True)
        outs.append((o * lax.rsqrt(ms + EPS) * onw_ref[0, hd:hd + 1, :]).astype(BF16))
    o_ref[0] = jnp.concatenate(outs, axis=1)


def _slcwin(q, pen, kaug, vaug, kwin, vwaug, o_cmp, gates, t_hi, t_lo, out_norm_w):
    assert QB == WINDOW
    b, _, s, _ = q.shape
    nq = s // QB
    nh = pen.shape[2]
    onw = out_norm_w[:D_ATTN].reshape(N_KV, HPG, HEAD_DIM)
    prev = lambda bi, g, j: (bi, g, jnp.maximum(j - 1, 0), 0)
    curr = lambda bi, g, j: (bi, g, j, 0)
    return pl.pallas_call(
        functools.partial(_slcwin_kernel, nh=nh),
        grid=(b, N_KV, nq),
        in_specs=[pl.BlockSpec((1, HPG, QB, HEAD_DIM), lambda bi, g, j: (bi, g, j, 0)),
                  pl.BlockSpec((1, 1, nh, QB, PEN_BLOCKS), lambda bi, g, j: (bi, g, 0, j, 0)),
                  pl.BlockSpec((1, 1, s, 2 * LANES), lambda bi, g, j: (bi, g, 0, 0)),
                  pl.BlockSpec((1, 1, s, LANES), lambda bi, g, j: (bi, g, 0, 0)),
                  pl.BlockSpec((1, 1, QB, HEAD_DIM), prev),
                  pl.BlockSpec((1, 1, QB, HEAD_DIM), curr),
                  pl.BlockSpec((1, 1, QB, LANES), prev),
                  pl.BlockSpec((1, 1, QB, LANES), curr),
                  pl.BlockSpec((1, ROWS_Q, QB), lambda bi, g, j: (g, 0, 0)),
                  pl.BlockSpec((1, ROWS_Q, LANES), lambda bi, g, j: (g, 0, 0)),
                  pl.BlockSpec((1, 1, HPG, QB, HEAD_DIM), lambda bi, g, j: (bi, g, 0, j, 0)),
                  pl.BlockSpec((1, QB, LANES), lambda bi, g, j: (bi, j, g)),
                  pl.BlockSpec((1, HPG, HEAD_DIM), lambda bi, g, j: (g, 0, 0))],
        out_specs=pl.BlockSpec((1, QB, HPG * HEAD_DIM), lambda bi, g, j: (bi, j, g)),
        out_shape=jax.ShapeDtypeStruct((b, s, D_ATTN), BF16),
        scratch_shapes=[pltpu.VMEM((ROWS_Q, 2 * LANES), BF16),
                        pltpu.VMEM((ROWS_Q, LANES), F32),
                        pltpu.VMEM((ROWS_Q, LANES), F32)],
        compiler_params=pltpu.CompilerParams(
            dimension_semantics=("parallel", "parallel", "arbitrary"), vmem_limit_bytes=VMEM_LIMIT),
        name="slcwin",
    )(q, pen, kaug, vaug, kwin, kwin, vwaug, vwaug, t_hi, t_lo, o_cmp, gates, onw)


ROUTE_E0 = N_GROUPS
U32 = jnp.uint32
HI_MASK = 0xFFFF0000


def _outproj_kernel(mixa_ref, yc_ref, wo_ref, x_ref, mod_ref, n2w_ref, wrh_ref, wrl_ref, br_ref, tri_ref,
                    x1_ref, h2p_ref, ri_ref, rw_ref, cnt_ref, carry_ref, *, tm):
    @pl.when((pl.program_id(0) == 0) & (pl.program_id(1) == 0))
    def _():
        carry_ref[...] = jnp.zeros_like(carry_ref)

    mod = mod_ref[0]
    g1 = mod[2:3]
    sh2 = mod[3:4]
    sc2 = mod[4:5]
    d_half = wo_ref.shape[0] // 2
    mix = _dot(mixa_ref[0], wo_ref[0:d_half]) + _dot(yc_ref[0], wo_ref[d_half:2 * d_half])
    x1 = x_ref[0] + g1 * mix
    x1_ref[0] = x1
    ms = jnp.mean(x1 * x1, axis=-1, keepdims=True)
    h2 = ((x1 * lax.rsqrt(ms + EPS)) * n2w_ref[...]) * (1.0 + sc2) + sh2

    dh = h2.shape[1] // 2
    lo_w = pltpu.bitcast(h2[:, :dh].astype(BF16).astype(F32), U32)
    hi_w = pltpu.bitcast(h2[:, dh:].astype(BF16).astype(F32), U32)
    h2p_ref[0] = lax.shift_right_logical(lo_w, jnp.uint32(16)) | (hi_w & jnp.uint32(HI_MASK))

    logits = _dot3(h2, wrh_ref[...], wrl_ref[...]) + br_ref[...]
    lane = lax.broadcasted_iota(I32, (tm, LANES), 1)
    glog = jnp.where(lane < N_GROUPS, logits, NEG)
    gmax = jnp.max(glog, axis=1, keepdims=True)
    g_p = 1.0 / jnp.sum(jnp.exp(glog - gmax), axis=1, keepdims=True)
    gidx = jnp.min(jnp.where(glog == gmax, lane, LANES), axis=1, keepdims=True)
    e_lo = ROUTE_E0 + EXPERTS_PER_GROUP * gidx
    el = jnp.where(lane >= e_lo, jnp.where(lane < e_lo + EXPERTS_PER_GROUP, logits, NEG), NEG)
    m1 = jnp.max(el, axis=1, keepdims=True)
    i1 = jnp.min(jnp.where(el == m1, lane, LANES), axis=1, keepdims=True)
    el2 = jnp.where(lane == i1, NEG, el)
    m2 = jnp.max(el2, axis=1, keepdims=True)
    i2 = jnp.min(jnp.where(el2 == m2, lane, LANES), axis=1, keepdims=True)
    e2 = jnp.exp(m2 - m1)
    w1 = g_p / (1.0 + e2)
    w2 = g_p * e2 / (1.0 + e2)

    oh1 = lane == i1
    oh2 = lane == i2
    both = jnp.where(oh1, 1.0, jnp.where(oh2, 1.0, 0.0))
    base = carry_ref[0:1] + _dot(tri_ref[...], both.astype(BF16))
    r1 = jnp.sum(jnp.where(oh1, base, 0.0), axis=1, keepdims=True)
    r2 = jnp.sum(jnp.where(oh2, base, 0.0), axis=1, keepdims=True)
    carry_ref[...] = carry_ref[...] + jnp.sum(both, axis=0, keepdims=True)
    cnt_ref[...] = carry_ref[...]

    ri_ref[0] = jnp.where(lane == 0, i1 - ROUTE_E0,
                          jnp.where(lane == 1, i2 - ROUTE_E0,
                                    jnp.where(lane == 2, r1.astype(I32),
                                              jnp.where(lane == 3, r2.astype(I32), 0))))
    rw_ref[0] = jnp.where(lane == 0, w1, jnp.where(lane == 1, w2, 0.0))


def _outproj(mixa, yconv, w_out, x, mod, norm2_w, w_group, b_group, w_expert, b_expert, tm=512):
    b, s, d = x.shape
    wr = jnp.pad(jnp.concatenate([w_group, w_expert], axis=1), ((0, 0), (0, LANES - N_GROUPS - N_EXPERTS)))
    wr_hi = wr.astype(BF16)
    wr_lo = (wr - wr_hi.astype(F32)).astype(BF16)
    br = jnp.pad(jnp.concatenate([b_group, b_expert]), (0, LANES - N_GROUPS - N_EXPERTS)).reshape(1, LANES)
    tri = jnp.asarray(np.tril(np.ones((tm, tm), np.float32), -1), dtype=BF16)
    full = lambda shape: pl.BlockSpec(shape, lambda bi, i: (0,) * len(shape))
    row = lambda w: pl.BlockSpec((1, tm, w), lambda bi, i: (bi, i, 0))
    return pl.pallas_call(
        functools.partial(_outproj_kernel, tm=tm),
        grid=(b, s // tm),
        in_specs=[row(D_ATTN), row(D_CONV), full((d, d)), row(d),
                  pl.BlockSpec((1, 6, d), lambda bi, i: (bi, 0, 0)),
                  full((1, d)), full((d, LANES)), full((d, LANES)), full((1, LANES)), full((tm, tm))],
        out_specs=[row(d), row(d // 2), row(LANES), row(LANES), full((8, LANES))],
        out_shape=[jax.ShapeDtypeStruct((b, s, d), F32),
                   jax.ShapeDtypeStruct((b, s, d // 2), U32),
                   jax.ShapeDtypeStruct((b, s, LANES), I32),
                   jax.ShapeDtypeStruct((b, s, LANES), F32),
                   jax.ShapeDtypeStruct((8, LANES), F32)],
        scratch_shapes=[pltpu.VMEM((8, LANES), F32)],
        compiler_params=pltpu.CompilerParams(
            dimension_semantics=("arbitrary", "arbitrary"), vmem_limit_bytes=VMEM_LIMIT),
        name="outproj",
    )(mixa, yconv, w_out.astype(BF16), x, mod, norm2_w.reshape(1, d), wr_hi, wr_lo, br, tri)


TOK_TILE = 256
ROW_UNROLL = 8


def _dest(ps_ref, ri_ref, k, r):
    return ps_ref[ri_ref[0, k, r]] + ri_ref[0, 2 + k, r]


def _dispatch_kernel(ps_ref, ri_ref, h2p_ref, xd_in_ref, xd_ref, sem):
    del xd_in_ref

    def row_copy(r, dest):
        return pltpu.make_async_copy(h2p_ref.at[pl.ds(r, 1)], xd_ref.at[pl.ds(dest, 1)], sem)

    def issue(r, carry):
        for k in range(2):
            row_copy(r, _dest(ps_ref, ri_ref, k, r)).start()
        return carry

    lax.fori_loop(0, TOK_TILE, issue, 0, unroll=ROW_UNROLL)

    def drain(r, carry):
        for k in range(2):
            row_copy(r, _dest(ps_ref, ri_ref, k, r)).wait()
        return carry

    lax.fori_loop(0, TOK_TILE, drain, 0, unroll=ROW_UNROLL)


def _dispatch(pad_start, ri_t, h2p, n_rows):
    t, w = h2p.shape
    xd0 = jnp.zeros((n_rows, w), U32)
    return pl.pallas_call(
        _dispatch_kernel,
        grid_spec=pltpu.PrefetchScalarGridSpec(
            num_scalar_prefetch=1,
            grid=(t // TOK_TILE,),
            in_specs=[pl.BlockSpec((1, 4, TOK_TILE), lambda i, ps: (i, 0, 0), memory_space=pltpu.SMEM),
                      pl.BlockSpec((TOK_TILE, w), lambda i, ps: (i, 0)),
                      pl.BlockSpec(memory_space=pl.ANY)],
            out_specs=pl.BlockSpec(memory_space=pl.ANY),
            scratch_shapes=[pltpu.SemaphoreType.DMA(())]),
        out_shape=jax.ShapeDtypeStruct((n_rows, w), U32),
        input_output_aliases={3: 0},
        compiler_params=pltpu.CompilerParams(
            dimension_semantics=("arbitrary",), vmem_limit_bytes=VMEM_LIMIT, has_side_effects=True),
        name="dispatch",
    )(pad_start, ri_t, h2p, xd0)


def _experts_kernel(be_ref, nu_ref, xd_ref, w1_ref, w3_ref, w2_ref, y_ref, w1b_ref, w3b_ref, w2b_ref):
    i = pl.program_id(0)

    @pl.when((i == 0) | (be_ref[i] != be_ref[jnp.maximum(i - 1, 0)]))
    def _():
        w1b_ref[...] = w1_ref[0].astype(BF16)
        w3b_ref[...] = w3_ref[0].astype(BF16)
        w2b_ref[...] = w2_ref[0].astype(BF16)

    @pl.when(i < nu_ref[0])
    def _():
        word = xd_ref[...]
        xa = pltpu.bitcast(lax.shift_left(word, jnp.uint32(16)), F32).astype(BF16)
        xb = pltpu.bitcast(word & jnp.uint32(HI_MASK), F32).astype(BF16)
        dh = xa.shape[1]
        a = _dot(xa, w1b_ref[0:dh]) + _dot(xb, w1b_ref[dh:2 * dh])
        c = _dot(xa, w3b_ref[0:dh]) + _dot(xb, w3b_ref[dh:2 * dh])
        y_ref[...] = _dot((_silu(a) * c).astype(BF16), w2b_ref[...])

    @pl.when(i >= nu_ref[0])
    def _():
        y_ref[...] = jnp.zeros_like(y_ref)


def _experts(blk_expert, n_used, xd, w1, w3, w2):
    n_rows, dh = xd.shape
    _, d, de = w1.shape
    n_blocks = n_rows // MOE_BLOCK
    return pl.pallas_call(
        _experts_kernel,
        grid_spec=pltpu.PrefetchScalarGridSpec(
            num_scalar_prefetch=2,
            grid=(n_blocks,),
            in_specs=[pl.BlockSpec((MOE_BLOCK, dh), lambda i, be, nu: (i, 0)),
                      pl.BlockSpec((1, d, de), lambda i, be, nu: (be[i], 0, 0)),
                      pl.BlockSpec((1, d, de), lambda i, be, nu: (be[i], 0, 0)),
                      pl.BlockSpec((1, de, d), lambda i, be, nu: (be[i], 0, 0))],
            out_specs=pl.BlockSpec((MOE_BLOCK, d), lambda i, be, nu: (i, 0)),
            scratch_shapes=[pltpu.VMEM((d, de), BF16), pltpu.VMEM((d, de), BF16), pltpu.VMEM((de, d), BF16)]),
        out_shape=jax.ShapeDtypeStruct((n_rows, d), F32),
        compiler_params=pltpu.CompilerParams(
            dimension_semantics=("arbitrary",), vmem_limit_bytes=VMEM_LIMIT),
        name="experts",
    )(blk_expert, n_used, xd, w1, w3, w2)


def _combine_kernel(ps_ref, ri_ref, rin_ref, rw_ref, x1_ref, mod_ref, y_ref, o_ref, buf_ref, sems):
    i = pl.program_id(0)
    slot = i % 2

    def gather(idx_ref, slot_, start):
        def body(r, carry):
            for k in range(2):
                cp = pltpu.make_async_copy(y_ref.at[pl.ds(_dest(ps_ref, idx_ref, k, r), 1)],
                                           buf_ref.at[slot_, k, pl.ds(r, 1)], sems.at[slot_])
                if start:
                    cp.start()
                else:
                    cp.wait()
            return carry

        lax.fori_loop(0, TOK_TILE, body, 0, unroll=ROW_UNROLL)

    @pl.when(i == 0)
    def _():
        gather(ri_ref, slot, True)

    @pl.when(i + 1 < pl.num_programs(0))
    def _():
        gather(rin_ref, 1 - slot, True)

    gather(ri_ref, slot, False)
    w = rw_ref[...]
    g2 = mod_ref[0][5:6]
    o_ref[...] = x1_ref[...] + g2 * (w[:, 0:1] * buf_ref[slot, 0] + w[:, 1:2] * buf_ref[slot, 1])


def _combine(pad_start, ri_t, rw, x1, mod, y_disp, seq):
    t, d = x1.shape
    tiles_per_seq = seq // TOK_TILE
    n_tiles = t // TOK_TILE
    return pl.pallas_call(
        _combine_kernel,
        grid_spec=pltpu.PrefetchScalarGridSpec(
            num_scalar_prefetch=1,
            grid=(n_tiles,),
            in_specs=[pl.BlockSpec((1, 4, TOK_TILE), lambda i, ps: (i, 0, 0), memory_space=pltpu.SMEM),
                      pl.BlockSpec((1, 4, TOK_TILE), lambda i, ps: (jnp.minimum(i + 1, n_tiles - 1), 0, 0),
                                   memory_space=pltpu.SMEM),
                      pl.BlockSpec((TOK_TILE, LANES), lambda i, ps: (i, 0)),
                      pl.BlockSpec((TOK_TILE, d), lambda i, ps: (i, 0)),
                      pl.BlockSpec((1, 6, d), lambda i, ps: (i // tiles_per_seq, 0, 0)),
                      pl.BlockSpec(memory_space=pl.ANY)],
            out_specs=pl.BlockSpec((TOK_TILE, d), lambda i, ps: (i, 0)),
            scratch_shapes=[pltpu.VMEM((2, 2, TOK_TILE, d), F32), pltpu.SemaphoreType.DMA((2,))]),
        out_shape=jax.ShapeDtypeStruct((t, d), F32),
        compiler_params=pltpu.CompilerParams(
            dimension_semantics=("arbitrary",), vmem_limit_bytes=VMEM_LIMIT),
        name="combine",
    )(pad_start, ri_t, ri_t, rw, x1, mod, y_disp)


def _moe(x1, h2p, ri, rw, cnt, mod, w1, w3, w2):
    b, s, d = x1.shape
    t = b * s
    a = 2 * t
    counts = cnt[0, ROUTE_E0:ROUTE_E0 + N_EXPERTS].astype(I32)
    padded = (counts + MOE_BLOCK - 1) // MOE_BLOCK * MOE_BLOCK
    pad_end = jnp.cumsum(padded)
    pad_start = (pad_end - padded).astype(I32)
    n_blocks = (a + N_EXPERTS * (MOE_BLOCK - 1)) // MOE_BLOCK + 1
    blk_start = jnp.arange(n_blocks, dtype=I32) * MOE_BLOCK
    blk_expert = jnp.minimum(
        jnp.sum((pad_end[None, :] <= blk_start[:, None]).astype(I32), axis=1), N_EXPERTS - 1)
    n_used = (pad_end[-1:] // MOE_BLOCK).astype(I32)
    ri_t = ri.reshape(t // TOK_TILE, TOK_TILE, LANES)[:, :, 0:4].transpose(0, 2, 1)
    xd = _dispatch(pad_start, ri_t, h2p.reshape(t, d // 2), n_blocks * MOE_BLOCK)
    y_disp = _experts(blk_expert, n_used, xd, w1, w3, w2)
    out = _combine(pad_start, ri_t, rw.reshape(t, LANES), x1.reshape(t, d), mod, y_disp, s)
    return out.reshape(b, s, d)


def kernel(x, c, w_ada, b_ada, norm1_w, w_in, q_norm_w, k_norm_w, cmp_pos_k, cmp_pos_v, cmp_k_w1, cmp_k_w2,
           cmp_v_w1, cmp_v_w2, conv_w, out_norm_w, w_out, rel_bias, norm2_w, w_group, b_group, w_expert,
           b_expert, w1, w3, w2):
    t_hi, t_lo, t_c = _bias_tables(rel_bias)
    for l in range(w_ada.shape[0]):
        mod = _ada(c, w_ada[l], b_ada[l])
        q, kvc, kaug, vaug, kwin, vwaug, gates, yconv = _inproj(
            x, mod, norm1_w[l], w_in[l], q_norm_w[l], k_norm_w[l], conv_w[l], out_norm_w[l])
        rows = _kvc_rows(kvc)
        k_c = _compress(rows, 0, cmp_k_w1[l], cmp_k_w2[l], cmp_pos_k[l], k_norm_w[l, 0], True)
        v_c = _compress(rows, 1, cmp_v_w1[l], cmp_v_w2[l], cmp_pos_v[l], k_norm_w[l, 0], False)
        o_cmp, pen = _cmpsel(q, k_c, v_c, t_c)
        mixa = _slcwin(q, pen, kaug, vaug, kwin, vwaug, o_cmp, gates, t_hi, t_lo, out_norm_w[l])
        x1, h2p, ri, rw, cnt = _outproj(mixa, yconv, w_out[l], x, mod, norm2_w[l],
                                        w_group[l], b_group[l], w_expert[l], b_expert[l])
        x = _moe(x1, h2p, ri, rw, cnt, mod, w1[l], w3[l], w2[l])
    return x
```

```python
import functools
import math

import numpy as np
import jax
import jax.numpy as jnp
from jax import lax
from jax.experimental import pallas as pl
from jax.experimental.pallas import tpu as pltpu

F32 = jnp.float32
BF16 = jnp.bfloat16
I32 = jnp.int32

HEAD_DIM = 64
N_HEADS = 8
N_KV = 2
HPG = N_HEADS // N_KV
D_ATTN = N_HEADS * HEAD_DIM
D_CONV = 512
KV_DIM = N_KV * HEAD_DIM
CMP_LEN = 32
CMP_STRIDE = 16
CMP_HIDDEN = 256
SLC_LEN = 64
N_SELECT = 16
WINDOW = 512
N_BUCKETS = 32
MAX_DISTANCE = 128
N_GROUPS = 8
EXPERTS_PER_GROUP = 8
N_EXPERTS = N_GROUPS * EXPERTS_PER_GROUP
MOE_BLOCK = 256
FORCED_SCORE = 1e4
EPS = 1e-6

LANES = 128
QB = 512
ROWS_Q = HPG * QB
CMP_CHUNK = 256
FAR_GROUP = 4
PEN_BLOCKS = 128
HALF_KEYS = PEN_BLOCKS * SLC_LEN
NEG = -1e30
PICKED = -3e38
VMEM_LIMIT = 56 * 1024 * 1024


def _dot(a, b):
    return jnp.dot(a, b, preferred_element_type=F32)


def _dot_nt(a, b):
    return lax.dot_general(a, b, (((1,), (1,)), ((), ())), preferred_element_type=F32)


def _hi_lo(a):
    hi = a.astype(BF16)
    lo = (a - hi.astype(F32)).astype(BF16)
    return hi, lo


def _dot3(a, b_hi, b_lo):
    a_hi, a_lo = _hi_lo(a)
    return _dot(a_hi, b_hi) + (_dot(a_lo, b_hi) + _dot(a_hi, b_lo))


def _group_sumsq(v, gm):
    hi, lo = _hi_lo(v * v)
    return _dot(hi, gm) + _dot(lo, gm)


def _silu(v):
    return v * jax.nn.sigmoid(v)


def _ada_kernel(c_ref, w_ref, b_ref, o_ref):
    cond = _silu(c_ref[...])
    w_hi, w_lo = _hi_lo(w_ref[...])
    o_ref[...] = _dot3(cond, w_hi, w_lo) + b_ref[...]


def _ada(c, w_ada, b_ada):
    b, d = c.shape
    n = w_ada.shape[1]
    tn = n // 4
    c8 = jnp.pad(c, ((0, 8 - b), (0, 0)))
    out = pl.pallas_call(
        _ada_kernel,
        grid=(n // tn,),
        in_specs=[pl.BlockSpec((8, d), lambda i: (0, 0)),
                  pl.BlockSpec((d, tn), lambda i: (0, i)),
                  pl.BlockSpec((1, tn), lambda i: (0, i))],
        out_specs=pl.BlockSpec((8, tn), lambda i: (0, i)),
        out_shape=jax.ShapeDtypeStruct((8, n), F32),
        compiler_params=pltpu.CompilerParams(vmem_limit_bytes=VMEM_LIMIT),
        name="ada",
    )(c8, w_ada, b_ada.reshape(1, n))
    return out[:b].reshape(b, 6, d)


def _inproj_kernel(x_ref, xh_ref, mod_ref, n1w_ref, wq_ref, wkv_ref, wg_ref, wbcu_ref,
                   qnw_ref, knw_ref, convw_ref, onw_ref, gm_ref,
                   q_ref, kvc_ref, kaug_ref, vaug_ref, kwin_ref, vwaug_ref, gates_ref, yconv_ref,
                   ext_ref, *, tm):
    i = pl.program_id(1)
    mod = mod_ref[0]
    sh1 = mod[0:1]
    sc1 = mod[1:2]

    def norm_mod(v):
        ms = jnp.mean(v * v, axis=-1, keepdims=True)
        y = (v * lax.rsqrt(ms + EPS)) * n1w_ref[...]
        return (y * (1.0 + sc1) + sh1).astype(BF16)

    h = norm_mod(x_ref[0])
    gm = gm_ref[...]
    gm_kv = gm_ref[0:KV_DIM, 0:KV_DIM]

    q = _dot(h, wq_ref[...])
    qn = q * lax.rsqrt(_group_sumsq(q, gm) * (1.0 / HEAD_DIM) + EPS) * qnw_ref[...]
    qn = (qn * (HEAD_DIM ** -0.5)).astype(BF16)
    for hd in range(N_HEADS):
        q_ref[0, hd] = qn[:, hd * HEAD_DIM:(hd + 1) * HEAD_DIM]

    kv = _dot(h, wkv_ref[...])
    kvc_ref[0] = kv[:, 0:2 * KV_DIM]
    ks = kv[:, 2 * KV_DIM:3 * KV_DIM]
    vs = kv[:, 3 * KV_DIM:4 * KV_DIM]
    kw = kv[:, 4 * KV_DIM:5 * KV_DIM]
    vw = kv[:, 5 * KV_DIM:6 * KV_DIM]
    ksn = ks * lax.rsqrt(_group_sumsq(ks, gm_kv) * (1.0 / HEAD_DIM) + EPS) * knw_ref[0:1]
    kwn = kw * lax.rsqrt(_group_sumsq(kw, gm_kv) * (1.0 / HEAD_DIM) + EPS) * knw_ref[1:2]

    lane = lax.broadcasted_iota(I32, (tm, LANES), 1)
    row = lax.broadcasted_iota(I32, (tm, LANES), 0)
    blk = ((i * tm + row) // SLC_LEN) % PEN_BLOCKS
    onehot = jnp.where(lane == blk, 1.0, 0.0).astype(BF16)
    ones_col = jnp.where(lax.broadcasted_iota(I32, (tm, HEAD_DIM), 1) == 0, 1.0, 0.0).astype(BF16)
    zeros64 = jnp.zeros((tm, HEAD_DIM), BF16)
    for g in range(N_KV):
        sl = slice(g * HEAD_DIM, (g + 1) * HEAD_DIM)
        kaug_ref[0, g, :, 0:LANES] = onehot
        kaug_ref[0, g, :, LANES:LANES + HEAD_DIM] = ksn[:, sl].astype(BF16)
        kaug_ref[0, g, :, LANES + HEAD_DIM:2 * LANES] = zeros64
        vaug_ref[0, g, :, 0:HEAD_DIM] = vs[:, sl].astype(BF16)
        vaug_ref[0, g, :, HEAD_DIM:LANES] = ones_col
        kwin_ref[0, g] = kwn[:, sl].astype(BF16)
        vwaug_ref[0, g, :, 0:HEAD_DIM] = vw[:, sl].astype(BF16)
        vwaug_ref[0, g, :, HEAD_DIM:LANES] = ones_col

    gates_ref[0] = jax.nn.sigmoid(_dot(h, wg_ref[...]))

    bcu = _dot(h, wbcu_ref[...])
    bg = bcu[:, 0:D_CONV]
    cu = bcu[:, D_CONV:2 * D_CONV] * bcu[:, 2 * D_CONV:3 * D_CONV]
    hh = norm_mod(xh_ref[0])
    cuh = _dot(hh, wbcu_ref[:, D_CONV:2 * D_CONV]) * _dot(hh, wbcu_ref[:, 2 * D_CONV:3 * D_CONV])
    cuh = jnp.where(i > 0, cuh, 0.0)
    ext_ref[0:8] = cuh
    ext_ref[8:8 + tm] = cu
    cw = convw_ref[...]
    y = bg * (cw[0:1] * ext_ref[6:6 + tm] + cw[1:2] * ext_ref[7:7 + tm] + cw[2:3] * cu)
    yn = y * lax.rsqrt(_group_sumsq(y, gm) * (1.0 / HEAD_DIM) + EPS) * onw_ref[...]
    yconv_ref[0] = yn.astype(BF16)


def _inproj(x, mod, norm1_w, w_in, q_norm_w, k_norm_w, conv_w, out_norm_w, tm=512):
    b, s, d = x.shape
    o_kv = D_ATTN
    o_g = o_kv + 6 * KV_DIM
    o_b = o_g + 3 * N_HEADS
    wq = w_in[:, 0:o_kv].astype(BF16)
    wkv = w_in[:, o_kv:o_g].astype(BF16)
    wg = jnp.pad(w_in[:, o_g:o_b].reshape(d, N_KV, 3 * HPG),
                 ((0, 0), (0, 0), (0, LANES - 3 * HPG))).reshape(d, N_KV * LANES).astype(BF16)
    wbcu = w_in[:, o_b:].astype(BF16)
    qnw = jnp.tile(q_norm_w, N_HEADS).reshape(1, D_ATTN)
    knw = jnp.stack([jnp.tile(k_norm_w[1], N_KV), jnp.tile(k_norm_w[2], N_KV)])
    onw = out_norm_w[D_ATTN:].reshape(1, D_CONV)
    gidx = np.arange(D_ATTN) // HEAD_DIM
    gm = jnp.asarray((gidx[:, None] == gidx[None, :]).astype(np.float32), dtype=BF16)

    full = lambda shape: pl.BlockSpec(shape, lambda bi, i: (0,) * len(shape))
    outs = pl.pallas_call(
        functools.partial(_inproj_kernel, tm=tm),
        grid=(b, s // tm),
        in_specs=[pl.BlockSpec((1, tm, d), lambda bi, i: (bi, i, 0)),
                  pl.BlockSpec((1, 8, d), lambda bi, i: (bi, jnp.maximum(i * (tm // 8) - 1, 0), 0)),
                  pl.BlockSpec((1, 6, d), lambda bi, i: (bi, 0, 0)),
                  full((1, d)), full(wq.shape), full(wkv.shape), full(wg.shape), full(wbcu.shape),
                  full(qnw.shape), full(knw.shape), full(conv_w.shape), full(onw.shape), full(gm.shape)],
        out_specs=[pl.BlockSpec((1, N_HEADS, tm, HEAD_DIM), lambda bi, i: (bi, 0, i, 0)),
                   pl.BlockSpec((1, tm, 2 * KV_DIM), lambda bi, i: (bi, i, 0)),
                   pl.BlockSpec((1, N_KV, tm, 2 * LANES), lambda bi, i: (bi, 0, i, 0)),
                   pl.BlockSpec((1, N_KV, tm, LANES), lambda bi, i: (bi, 0, i, 0)),
                   pl.BlockSpec((1, N_KV, tm, HEAD_DIM), lambda bi, i: (bi, 0, i, 0)),
                   pl.BlockSpec((1, N_KV, tm, LANES), lambda bi, i: (bi, 0, i, 0)),
                   pl.BlockSpec((1, tm, N_KV * LANES), lambda bi, i: (bi, i, 0)),
                   pl.BlockSpec((1, tm, D_CONV), lambda bi, i: (bi, i, 0))],
        out_shape=[jax.ShapeDtypeStruct((b, N_HEADS, s, HEAD_DIM), BF16),
                   jax.ShapeDtypeStruct((b, s, 2 * KV_DIM), F32),
                   jax.ShapeDtypeStruct((b, N_KV, s, 2 * LANES), BF16),
                   jax.ShapeDtypeStruct((b, N_KV, s, LANES), BF16),
                   jax.ShapeDtypeStruct((b, N_KV, s, HEAD_DIM), BF16),
                   jax.ShapeDtypeStruct((b, N_KV, s, LANES), BF16),
                   jax.ShapeDtypeStruct((b, s, N_KV * LANES), F32),
                   jax.ShapeDtypeStruct((b, s, D_CONV), BF16)],
        scratch_shapes=[pltpu.VMEM((tm + 8, D_CONV), F32)],
        compiler_params=pltpu.CompilerParams(
            dimension_semantics=("parallel", "arbitrary"), vmem_limit_bytes=VMEM_LIMIT),
        name="inproj",
    )(x, x, mod, norm1_w.reshape(1, d), wq, wkv, wg, wbcu, qnw, knw, conv_w, onw, gm)
    return outs


def _compress_kernel(r_ref, w1_ref, w2_ref, pos_ref, nw_ref, o_ref, *, norm):
    r = r_ref[0, 0, 0]
    nr = r.shape[0]
    half = (CMP_LEN // 2) * HEAD_DIM
    w1a = w1_ref[0:half, :]
    w1b = w1_ref[half:2 * half, :]
    pos = pos_ref[...]
    cst = (_dot(jnp.broadcast_to(pos[0:1], (8, half)).astype(BF16), w1a)
           + _dot(jnp.broadcast_to(pos[1:2], (8, half)).astype(BF16), w1b))[0:1]
    upper = pltpu.roll(_dot(r, w1b), nr - 1, 0)
    hid = _dot(r, w1a) + upper + cst
    out = _dot(_silu(hid).astype(BF16), w2_ref[...])
    if norm:
        ms = jnp.mean(out * out, axis=-1, keepdims=True)
        out = out * lax.rsqrt(ms + EPS) * nw_ref[...]
    o_ref[0, 0] = out.astype(BF16)


def _compress(rows, kv_index, w1, w2, pos_emb, norm_w, norm):
    b, _, g, nr, width = rows.shape
    return pl.pallas_call(
        functools.partial(_compress_kernel, norm=norm),
        grid=(b, g),
        in_specs=[pl.BlockSpec((1, 1, 1, nr, width), lambda bi, gi: (bi, kv_index, gi, 0, 0)),
                  pl.BlockSpec(w1.shape, lambda bi, gi: (0, 0)),
                  pl.BlockSpec(w2.shape, lambda bi, gi: (0, 0)),
                  pl.BlockSpec((2, width), lambda bi, gi: (0, 0)),
                  pl.BlockSpec((1, HEAD_DIM), lambda bi, gi: (0, 0))],
        out_specs=pl.BlockSpec((1, 1, nr, HEAD_DIM), lambda bi, gi: (bi, gi, 0, 0)),
        out_shape=jax.ShapeDtypeStruct((b, g, nr, HEAD_DIM), BF16),
        compiler_params=pltpu.CompilerParams(vmem_limit_bytes=VMEM_LIMIT),
        name="compress_k" if norm else "compress_v",
    )(rows, w1.astype(BF16), w2.astype(BF16), pos_emb.reshape(2, width), norm_w.reshape(1, HEAD_DIM))


def _kvc_rows(kvc):
    b, s, _ = kvc.shape
    r = kvc.reshape(b, s, 2, N_KV, HEAD_DIM).transpose(0, 2, 3, 1, 4)
    return r.reshape(b, 2, N_KV, s // CMP_STRIDE, CMP_STRIDE * HEAD_DIM).astype(BF16)


def _bucket_np(dist):
    n = np.maximum(dist, 0)
    max_exact = N_BUCKETS // 2
    nf = np.maximum(n, 1).astype(np.float32)
    large = max_exact + (np.log(nf / np.float32(max_exact)) / np.float32(math.log(MAX_DISTANCE / max_exact))
                         * np.float32(N_BUCKETS - max_exact)).astype(np.int32)
    large = np.minimum(large, N_BUCKETS - 1)
    return np.where(n < max_exact, n, large)


_BUCKET_START = [int(np.argmax(_bucket_np(np.arange(MAX_DISTANCE + 1)) >= k)) for k in range(N_BUCKETS)]
CMP_DIST0 = CMP_STRIDE * CMP_CHUNK - (CMP_LEN - 1) - QB


def _bias_tables_kernel(rbt_ref, thi_ref, tlo_ref, tc_ref):
    head = pl.program_id(0) * HPG + pl.program_id(1)
    far = rbt_ref[head, N_BUCKETS - 1]

    def table(dist):
        v = jnp.zeros(dist.shape, F32)
        for k in range(N_BUCKETS - 2, -1, -1):
            v = jnp.where(dist < _BUCKET_START[k + 1], rbt_ref[head, k] - far, v)
        return jnp.where(dist >= 0, v, NEG)

    def dist(shape, row_mul, col_mul, off):
        return (lax.broadcasted_iota(I32, shape, 0) * row_mul
                + lax.broadcasted_iota(I32, shape, 1) * col_mul + off)

    thi_ref[0] = table(dist((QB, QB), 1, -1, 0))
    tlo_ref[0] = table(dist((QB, LANES), 1, -1, LANES))
    tc_ref[0] = table(dist((QB, CMP_CHUNK), 1, -CMP_STRIDE, CMP_DIST0))


def _bias_tables(rel_bias):
    return pl.pallas_call(
        _bias_tables_kernel,
        grid=(N_KV, HPG),
        in_specs=[pl.BlockSpec(memory_space=pltpu.SMEM)],
        out_specs=[pl.BlockSpec((1, QB, QB), lambda g, h: (g, h, 0)),
                   pl.BlockSpec((1, QB, LANES), lambda g, h: (g, h, 0)),
                   pl.BlockSpec((1, QB, CMP_CHUNK), lambda g, h: (g, h, 0))],
        out_shape=[jax.ShapeDtypeStruct((N_KV, ROWS_Q, QB), F32),
                   jax.ShapeDtypeStruct((N_KV, ROWS_Q, LANES), F32),
                   jax.ShapeDtypeStruct((N_KV, ROWS_Q, CMP_CHUNK), F32)],
        name="bias_tables",
    )(rel_bias.T)


def _cmpsel_kernel(q_ref, kc_ref, vc_ref, tab_ref, ovt_ref, ocmp_ref, pen_ref,
                   m_ref, l_ref, acc_ref, imp_ref, score_ref, *, nsbp, n_sel):
    j = pl.program_id(2)
    q4 = q_ref[0].reshape(ROWS_Q, HEAD_DIM)
    nq16 = QB // CMP_STRIDE
    n_chunks = (nq16 * (j + 1) - 1) // CMP_CHUNK + 1
    ni = lax.broadcasted_iota(I32, (1, CMP_CHUNK), 1)

    def chunk_start(m):
        return pl.multiple_of(nq16 * (j + 1) - CMP_CHUNK * m, nq16)

    def logits(m):
        n0p = chunk_start(m)
        s = _dot_nt(q4, kc_ref[0, 0, pl.ds(n0p, CMP_CHUNK), :])
        return jnp.where(ni + n0p >= CMP_CHUNK, s, NEG)

    s0 = logits(0) + tab_ref[0]
    m_ref[...] = s0
    l_ref[...] = jnp.ones_like(l_ref)

    def stats(m, carry):
        s = logits(m)
        m_prev = m_ref[...]
        m_new = jnp.maximum(m_prev, s)
        l_ref[...] = l_ref[...] * jnp.exp(m_prev - m_new) + jnp.exp(s - m_new)
        m_ref[...] = m_new
        return carry

    lax.fori_loop(1, n_chunks, stats, 0)

    m_lane = m_ref[...]
    m_fin = jnp.broadcast_to(jnp.max(m_lane, axis=1, keepdims=True), (ROWS_Q, CMP_CHUNK))
    l_fin = jnp.sum(l_ref[...] * jnp.exp(m_lane - m_fin), axis=1, keepdims=True)
    inv = jnp.where(m_fin > 0.5 * NEG, 1.0 / l_fin, 0.0)
    m_ref[...] = m_fin
    l_ref[...] = inv
    acc_ref[...] = jnp.zeros_like(acc_ref)
    imp_ref[...] = jnp.zeros_like(imp_ref)

    def accumulate(m, s):
        n0p = chunk_start(m)
        p = jnp.exp(s - m_ref[...]) * l_ref[...]
        acc_ref[...] += _dot(p.astype(BF16), vc_ref[0, 0, pl.ds(n0p, CMP_CHUNK), :])
        psum = p[0:QB]
        for hd in range(1, HPG):
            psum = psum + p[hd * QB:(hd + 1) * QB]
        hi, lo = _hi_lo(psum)
        contrib = _dot_nt(ovt_ref[...], hi) + _dot_nt(ovt_ref[...], lo)
        r0 = pl.multiple_of(n0p // 4, 8)
        imp_ref[pl.ds(r0, OV_ROWS), :] += contrib

    accumulate(0, s0)

    def second(m, carry):
        accumulate(m, logits(m))
        return carry

    lax.fori_loop(1, n_chunks, second, 0)
    ocmp_ref[0, 0] = acc_ref[...].reshape(HPG, QB, HEAD_DIM)

    imp = imp_ref[IMP_PAD:IMP_PAD + nsbp, :]
    srow = lax.broadcasted_iota(I32, (nsbp, QB), 0)
    qcol = lax.broadcasted_iota(I32, (nsbp, QB), 1)
    cur = (QB // SLC_LEN) * j + qcol // SLC_LEN
    forced = (srow == 0) | (srow == cur) | (srow == cur - 1)
    score_ref[...] = jnp.where(forced, -1.0, jnp.where(srow <= cur, imp, -1.0))

    quarter = nsbp // 4
    case = ((QB // SLC_LEN) * (j + 1) - 1) // quarter
    for k in range(4):
        rows = quarter * (k + 1)

        @pl.when(case == k)
        def _(rows=rows):
            srow_k = lax.broadcasted_iota(I32, (rows, QB), 0)

            def pick(r, carry):
                sc = score_ref[0:rows]
                mx = jnp.max(sc, axis=0, keepdims=True)
                idx = jnp.min(jnp.where(sc == mx, srow_k, nsbp), axis=0, keepdims=True)
                score_ref[0:rows] = jnp.where(srow_k == idx, PICKED, sc)
                return carry

            lax.fori_loop(0, max(n_sel - 3, 0), pick, 0)
    pen_other = jnp.where(srow <= cur, jnp.where(score_ref[...] == PICKED, 0.0, NEG), NEG)
    pen_t = jnp.where(forced, 0.0, pen_other).T
    for hf in range(nsbp // PEN_BLOCKS):
        pen_ref[0, 0, hf] = pen_t[:, hf * PEN_BLOCKS:(hf + 1) * PEN_BLOCKS].astype(BF16)


IMP_PAD = CMP_CHUNK // 4
OV_ROWS = IMP_PAD + 8


def _overlap_t():
    n = np.arange(CMP_CHUNK)
    s = np.arange(OV_ROWS)
    first = (n * CMP_STRIDE) // SLC_LEN
    last = (n * CMP_STRIDE + CMP_LEN - 1) // SLC_LEN
    ov = (s[:, None] >= first[None, :]) & (s[:, None] <= last[None, :])
    return jnp.asarray(ov.astype(np.float32), dtype=BF16)


def _cmpsel(q, k_c, v_c, tab):
    b, _, s, _ = q.shape
    nr = k_c.shape[2]
    nq = s // QB
    nsb = s // SLC_LEN
    nh = -(-nsb // PEN_BLOCKS)
    nsbp = nh * PEN_BLOCKS
    n_sel = min(N_SELECT, nsb)
    pad = ((0, 0), (0, 0), (CMP_CHUNK, 0), (0, 0))
    kc_p = jnp.pad(k_c, pad)
    vc_p = jnp.pad(v_c, pad)
    ovt = _overlap_t()
    return pl.pallas_call(
        functools.partial(_cmpsel_kernel, nsbp=nsbp, n_sel=n_sel),
        grid=(b, N_KV, nq),
        in_specs=[pl.BlockSpec((1, HPG, QB, HEAD_DIM), lambda bi, g, j: (bi, g, j, 0)),
                  pl.BlockSpec((1, 1, nr + CMP_CHUNK, HEAD_DIM), lambda bi, g, j: (bi, g, 0, 0)),
                  pl.BlockSpec((1, 1, nr + CMP_CHUNK, HEAD_DIM), lambda bi, g, j: (bi, g, 0, 0)),
                  pl.BlockSpec((1, ROWS_Q, CMP_CHUNK), lambda bi, g, j: (g, 0, 0)),
                  pl.BlockSpec(ovt.shape, lambda bi, g, j: (0, 0))],
        out_specs=[pl.BlockSpec((1, 1, HPG, QB, HEAD_DIM), lambda bi, g, j: (bi, g, 0, j, 0)),
                   pl.BlockSpec((1, 1, nh, QB, PEN_BLOCKS), lambda bi, g, j: (bi, g, 0, j, 0))],
        out_shape=[jax.ShapeDtypeStruct((b, N_KV, HPG, s, HEAD_DIM), F32),
                   jax.ShapeDtypeStruct((b, N_KV, nh, s, PEN_BLOCKS), BF16)],
        scratch_shapes=[pltpu.VMEM((ROWS_Q, CMP_CHUNK), F32), pltpu.VMEM((ROWS_Q, CMP_CHUNK), F32),
                        pltpu.VMEM((ROWS_Q, HEAD_DIM), F32),
                        pltpu.VMEM((IMP_PAD + nsbp + 8, QB), F32),
                        pltpu.VMEM((nsbp, QB), F32)],
        compiler_params=pltpu.CompilerParams(
            dimension_semantics=("parallel", "parallel", "arbitrary"), vmem_limit_bytes=VMEM_LIMIT),
        name="cmpsel",
    )(q, kc_p, vc_p, tab, ovt)


def _slcwin_kernel(q_ref, pen_ref, kaug_ref, vaug_ref, kwp_ref, kwc_ref, vwp_ref, vwc_ref,
                   thi_ref, tlo_ref, ocmp_ref, gates_ref, onw_ref, o_ref,
                   lhs_ref, m_ref, acc_ref, *, nh):
    j = pl.program_id(2)
    s0 = pl.multiple_of(j * QB, QB)
    q4 = q_ref[0].reshape(ROWS_Q, HEAD_DIM)
    near = QB - LANES

    lhs_ref[:, LANES:LANES + HEAD_DIM] = q4
    lhs_ref[:, LANES + HEAD_DIM:2 * LANES] = jnp.zeros((ROWS_Q, HEAD_DIM), BF16)

    def set_pen(half):
        p = pen_ref[0, 0, half]
        for hd in range(HPG):
            lhs_ref[hd * QB:(hd + 1) * QB, 0:LANES] = p

    def reset():
        m_ref[...] = jnp.full_like(m_ref, NEG)
        acc_ref[...] = jnp.zeros_like(acc_ref)

    def update(s, v):
        m_prev = m_ref[...]
        m_new = jnp.maximum(m_prev, jnp.max(s, axis=1, keepdims=True))
        p = jnp.exp(s - jnp.concatenate([m_new] * (s.shape[1] // LANES), axis=1))
        acc_ref[...] = acc_ref[...] * jnp.exp(m_prev - m_new) + _dot(p.astype(BF16), v)
        m_ref[...] = m_new

    def add_near_bias(s):
        return jnp.concatenate([s[:, :near], s[:, near:] + tlo_ref[0]], axis=1)

    def result():
        acc = acc_ref[...]
        return acc[:, 0:HEAD_DIM] / acc[:, HEAD_DIM:HEAD_DIM + 1]

    reset()
    half_hi = s0 // HALF_KEYS
    set_pen(half_hi)
    update(_dot_nt(lhs_ref[...], kaug_ref[0, 0, pl.ds(s0, QB), :]) + thi_ref[0],
           vaug_ref[0, 0, pl.ds(s0, QB), :])

    @pl.when(j > 0)
    def _():
        sl = pl.multiple_of(s0 - QB, QB)
        half_lo = sl // HALF_KEYS

        @pl.when(half_lo != half_hi)
        def _():
            set_pen(half_lo)

        s = _dot_nt(lhs_ref[...], kaug_ref[0, 0, pl.ds(sl, QB), :])
        update(add_near_bias(s), vaug_ref[0, 0, pl.ds(sl, QB), :])

    n_far = jnp.maximum(j - 1, 0)
    chunks_per_half = HALF_KEYS // QB
    assert chunks_per_half % FAR_GROUP == 0

    def far_chunk(c):
        k0 = pl.multiple_of(c * QB, QB)
        update(_dot_nt(lhs_ref[...], kaug_ref[0, 0, pl.ds(k0, QB), :]),
               vaug_ref[0, 0, pl.ds(k0, QB), :])

    def enter_half(c):
        @pl.when(c % chunks_per_half == 0)
        def _():
            set_pen(c // chunks_per_half)

    def far_group(i4, carry):
        c = FAR_GROUP * i4
        enter_half(c)
        for u in range(FAR_GROUP):
            far_chunk(c + u)
        return carry

    n_groups = n_far // FAR_GROUP
    lax.fori_loop(0, n_groups, far_group, 0)

    def far_single(c, carry):
        enter_half(c)
        far_chunk(c)
        return carry

    lax.fori_loop(n_groups * FAR_GROUP, n_far, far_single, 0)

    o_slc = result()

    reset()
    update(_dot_nt(q4, kwc_ref[0, 0]) + thi_ref[0], vwc_ref[0, 0])

    @pl.when(j > 0)
    def _():
        s = _dot_nt(q4, kwp_ref[0, 0])
        qi = lax.broadcasted_iota(I32, (ROWS_Q, QB), 0) & (QB - 1)
        kk = lax.broadcasted_iota(I32, (ROWS_Q, QB), 1)
        s = jnp.where(kk > qi, s, NEG)
        update(add_near_bias(s), vwp_ref[0, 0])

    o_win = result()

    o_cmp = ocmp_ref[0, 0].reshape(ROWS_Q, HEAD_DIM)
    gt = gates_ref[0]
    outs = []
    for hd in range(HPG):
        rs = slice(hd * QB, (hd + 1) * QB)
        o = (gt[:, 3 * hd:3 * hd + 1] * o_cmp[rs] + gt[:, 3 * hd + 1:3 * hd + 2] * o_slc[rs]
             + gt[:, 3 * hd + 2:3 * hd + 3] * o_win[rs])
        ms = jnp.mean(o * o, axis=-1, keepdims=True)
        outs.append((o * lax.rsqrt(ms + EPS) * onw_ref[0, hd:hd + 1, :]).astype(BF16))
    o_ref[0] = jnp.concatenate(outs, axis=1)


def _slcwin(q, pen, kaug, vaug, kwin, vwaug, o_cmp, gates, t_hi, t_lo, out_norm_w):
    assert QB == WINDOW
    b, _, s, _ = q.shape
    nq = s // QB
    nh = pen.shape[2]
    onw = out_norm_w[:D_ATTN].reshape(N_KV, HPG, HEAD_DIM)
    prev = lambda bi, g, j: (bi, g, jnp.maximum(j - 1, 0), 0)
    curr = lambda bi, g, j: (bi, g, j, 0)
    return pl.pallas_call(
        functools.partial(_slcwin_kernel, nh=nh),
        grid=(b, N_KV, nq),
        in_specs=[pl.BlockSpec((1, HPG, QB, HEAD_DIM), lambda bi, g, j: (bi, g, j, 0)),
                  pl.BlockSpec((1, 1, nh, QB, PEN_BLOCKS), lambda bi, g, j: (bi, g, 0, j, 0)),
                  pl.BlockSpec((1, 1, s, 2 * LANES), lambda bi, g, j: (bi, g, 0, 0)),
                  pl.BlockSpec((1, 1, s, LANES), lambda bi, g, j: (bi, g, 0, 0)),
                  pl.BlockSpec((1, 1, QB, HEAD_DIM), prev),
                  pl.BlockSpec((1, 1, QB, HEAD_DIM), curr),
                  pl.BlockSpec((1, 1, QB, LANES), prev),
                  pl.BlockSpec((1, 1, QB, LANES), curr),
                  pl.BlockSpec((1, ROWS_Q, QB), lambda bi, g, j: (g, 0, 0)),
                  pl.BlockSpec((1, ROWS_Q, LANES), lambda bi, g, j: (g, 0, 0)),
                  pl.BlockSpec((1, 1, HPG, QB, HEAD_DIM), lambda bi, g, j: (bi, g, 0, j, 0)),
                  pl.BlockSpec((1, QB, LANES), lambda bi, g, j: (bi, j, g)),
                  pl.BlockSpec((1, HPG, HEAD_DIM), lambda bi, g, j: (g, 0, 0))],
        out_specs=pl.BlockSpec((1, QB, HPG * HEAD_DIM), lambda bi, g, j: (bi, j, g)),
        out_shape=jax.ShapeDtypeStruct((b, s, D_ATTN), BF16),
        scratch_shapes=[pltpu.VMEM((ROWS_Q, 2 * LANES), BF16),
                        pltpu.VMEM((ROWS_Q, LANES), F32),
                        pltpu.VMEM((ROWS_Q, LANES), F32)],
        compiler_params=pltpu.CompilerParams(
            dimension_semantics=("parallel", "parallel", "arbitrary"), vmem_limit_bytes=VMEM_LIMIT),
        name="slcwin",
    )(q, pen, kaug, vaug, kwin, kwin, vwaug, vwaug, t_hi, t_lo, o_cmp, gates, onw)


ROUTE_E0 = N_GROUPS
U32 = jnp.uint32
HI_MASK = 0xFFFF0000


def _outproj_kernel(mixa_ref, yc_ref, wo_ref, x_ref, mod_ref, n2w_ref, wrh_ref, wrl_ref, br_ref, tri_ref,
                    x1_ref, h2p_ref, ri_ref, rw_ref, cnt_ref, carry_ref, rect_ref, *, tm):
    @pl.when((pl.program_id(0) == 0) & (pl.program_id(1) == 0))
    def _():
        carry_ref[...] = jnp.zeros_like(carry_ref)

    mod = mod_ref[0]
    g1 = mod[2:3]
    sh2 = mod[3:4]
    sc2 = mod[4:5]
    d_half = wo_ref.shape[0] // 2
    mix = _dot(mixa_ref[0], wo_ref[0:d_half]) + _dot(yc_ref[0], wo_ref[d_half:2 * d_half])
    x1 = x_ref[0] + g1 * mix
    x1_ref[0] = x1
    ms = jnp.mean(x1 * x1, axis=-1, keepdims=True)
    h2 = ((x1 * lax.rsqrt(ms + EPS)) * n2w_ref[...]) * (1.0 + sc2) + sh2

    dh = h2.shape[1] // 2
    lo_w = pltpu.bitcast(h2[:, :dh].astype(BF16).astype(F32), U32)
    hi_w = pltpu.bitcast(h2[:, dh:].astype(BF16).astype(F32), U32)
    h2p_ref[0] = lax.shift_right_logical(lo_w, jnp.uint32(16)) | (hi_w & jnp.uint32(HI_MASK))

    logits = _dot3(h2, wrh_ref[...], wrl_ref[...]) + br_ref[...]
    lane = lax.broadcasted_iota(I32, (tm, LANES), 1)
    glog = jnp.where(lane < N_GROUPS, logits, NEG)
    gmax = jnp.max(glog, axis=1, keepdims=True)
    g_p = 1.0 / jnp.sum(jnp.exp(glog - gmax), axis=1, keepdims=True)
    gidx = jnp.min(jnp.where(glog == gmax, lane, LANES), axis=1, keepdims=True)
    e_lo = ROUTE_E0 + EXPERTS_PER_GROUP * gidx
    el = jnp.where(lane >= e_lo, jnp.where(lane < e_lo + EXPERTS_PER_GROUP, logits, NEG), NEG)
    m1 = jnp.max(el, axis=1, keepdims=True)
    i1 = jnp.min(jnp.where(el == m1, lane, LANES), axis=1, keepdims=True)
    el2 = jnp.where(lane == i1, NEG, el)
    m2 = jnp.max(el2, axis=1, keepdims=True)
    i2 = jnp.min(jnp.where(el2 == m2, lane, LANES), axis=1, keepdims=True)
    e2 = jnp.exp(m2 - m1)
    w1 = g_p / (1.0 + e2)
    w2 = g_p * e2 / (1.0 + e2)

    oh1 = lane == i1
    oh2 = lane == i2
    both = jnp.where(oh1, 1.0, jnp.where(oh2, 1.0, 0.0))
    base = carry_ref[0:1] + _dot(tri_ref[...], both.astype(BF16))
    r1 = jnp.sum(jnp.where(oh1, base, 0.0), axis=1, keepdims=True)
    r2 = jnp.sum(jnp.where(oh2, base, 0.0), axis=1, keepdims=True)
    carry_ref[...] = carry_ref[...] + jnp.sum(both, axis=0, keepdims=True)
    cnt_ref[...] = carry_ref[...]

    rec = jnp.where(lane == 0, (i1 - ROUTE_E0).astype(F32),
                    jnp.where(lane == 1, (i2 - ROUTE_E0).astype(F32),
                              jnp.where(lane == 2, r1, jnp.where(lane == 3, r2, 0.0))))
    rect_ref[...] = rec.T
    ri_ref[0, 0] = rect_ref[0:8, :].astype(I32)
    rw_ref[0] = jnp.where(lane == 0, w1, jnp.where(lane == 1, w2, 0.0))


def _outproj(mixa, yconv, w_out, x, mod, norm2_w, w_group, b_group, w_expert, b_expert):
    b, s, d = x.shape
    tm = TOK_TILE
    wr = jnp.pad(jnp.concatenate([w_group, w_expert], axis=1), ((0, 0), (0, LANES - N_GROUPS - N_EXPERTS)))
    wr_hi = wr.astype(BF16)
    wr_lo = (wr - wr_hi.astype(F32)).astype(BF16)
    br = jnp.pad(jnp.concatenate([b_group, b_expert]), (0, LANES - N_GROUPS - N_EXPERTS)).reshape(1, LANES)
    tri = jnp.asarray(np.tril(np.ones((tm, tm), np.float32), -1), dtype=BF16)
    full = lambda shape: pl.BlockSpec(shape, lambda bi, i: (0,) * len(shape))
    row = lambda w: pl.BlockSpec((1, tm, w), lambda bi, i: (bi, i, 0))
    return pl.pallas_call(
        functools.partial(_outproj_kernel, tm=tm),
        grid=(b, s // tm),
        in_specs=[row(D_ATTN), row(D_CONV), full((d, d)), row(d),
                  pl.BlockSpec((1, 6, d), lambda bi, i: (bi, 0, 0)),
                  full((1, d)), full((d, LANES)), full((d, LANES)), full((1, LANES)), full((tm, tm))],
        out_specs=[row(d), row(d // 2), pl.BlockSpec((1, 1, 8, tm), lambda bi, i: (bi, i, 0, 0)),
                   row(LANES), full((8, LANES))],
        out_shape=[jax.ShapeDtypeStruct((b, s, d), F32),
                   jax.ShapeDtypeStruct((b, s, d // 2), U32),
                   jax.ShapeDtypeStruct((b, s // tm, 8, tm), I32),
                   jax.ShapeDtypeStruct((b, s, LANES), F32),
                   jax.ShapeDtypeStruct((8, LANES), F32)],
        scratch_shapes=[pltpu.VMEM((8, LANES), F32), pltpu.VMEM((LANES, tm), F32)],
        compiler_params=pltpu.CompilerParams(
            dimension_semantics=("arbitrary", "arbitrary"), vmem_limit_bytes=VMEM_LIMIT),
        name="outproj",
    )(mixa, yconv, w_out.astype(BF16), x, mod, norm2_w.reshape(1, d), wr_hi, wr_lo, br, tri)


TOK_TILE = 512
ROW_UNROLL = 8


def _dest_kernel(ps_ref, ri_ref, d_ref):
    e = ri_ref[0, 0]
    acc = pltpu.roll(e, 6, 0)
    for x in range(N_EXPERTS):
        acc = jnp.where(e == x, acc + ps_ref[x], acc)
    d_ref[0, 0] = acc


def _dest_rows(pad_start, ri_c):
    b, nt, _, tm = ri_c.shape
    return pl.pallas_call(
        _dest_kernel,
        grid_spec=pltpu.PrefetchScalarGridSpec(
            num_scalar_prefetch=1,
            grid=(b, nt),
            in_specs=[pl.BlockSpec((1, 1, 8, tm), lambda bi, i, ps: (bi, i, 0, 0))],
            out_specs=pl.BlockSpec((1, 1, 8, tm), lambda bi, i, ps: (bi, i, 0, 0))),
        out_shape=jax.ShapeDtypeStruct((b, nt, 8, tm), I32),
        name="dest_rows",
    )(pad_start, ri_c)


def _for_each_row(fn):
    for base in range(0, TOK_TILE, LANES):
        def group(g, carry, base=base):
            r0 = base + pl.multiple_of(g * ROW_UNROLL, ROW_UNROLL)
            for u in range(ROW_UNROLL):
                for k in range(2):
                    fn(r0 + u, k)
            return carry

        lax.fori_loop(0, LANES // ROW_UNROLL, group, 0)


def _dispatch_kernel(dest_ref, h2p_ref, xd_in_ref, xd_ref, sem):
    del xd_in_ref

    def row_copy(r, k):
        return pltpu.make_async_copy(h2p_ref.at[pl.ds(r, 1)], xd_ref.at[pl.ds(dest_ref[0, k, r], 1)], sem)

    _for_each_row(lambda r, k: row_copy(r, k).start())
    _for_each_row(lambda r, k: row_copy(r, k).wait())


def _dispatch(dest_t, h2p, n_rows):
    t, w = h2p.shape
    xd0 = jnp.zeros((n_rows, w), U32)
    return pl.pallas_call(
        _dispatch_kernel,
        grid=(t // TOK_TILE,),
        in_specs=[pl.BlockSpec((1, 8, TOK_TILE), lambda i: (i, 0, 0), memory_space=pltpu.SMEM),
                  pl.BlockSpec((TOK_TILE, w), lambda i: (i, 0)),
                  pl.BlockSpec(memory_space=pl.ANY)],
        out_specs=pl.BlockSpec(memory_space=pl.ANY),
        scratch_shapes=[pltpu.SemaphoreType.DMA(())],
        out_shape=jax.ShapeDtypeStruct((n_rows, w), U32),
        input_output_aliases={2: 0},
        compiler_params=pltpu.CompilerParams(
            dimension_semantics=("arbitrary",), vmem_limit_bytes=VMEM_LIMIT, has_side_effects=True),
        name="dispatch",
    )(dest_t, h2p, xd0)


def _experts_kernel(be_ref, nu_ref, xd_ref, w1_ref, w3_ref, w2_ref, y_ref, w1b_ref, w3b_ref, w2b_ref):
    i = pl.program_id(0)

    @pl.when((i == 0) | (be_ref[i] != be_ref[jnp.maximum(i - 1, 0)]))
    def _():
        w1b_ref[...] = w1_ref[0].astype(BF16)
        w3b_ref[...] = w3_ref[0].astype(BF16)
        w2b_ref[...] = w2_ref[0].astype(BF16)

    @pl.when(i < nu_ref[0])
    def _():
        word = xd_ref[...]
        xa = pltpu.bitcast(lax.shift_left(word, jnp.uint32(16)), F32).astype(BF16)
        xb = pltpu.bitcast(word & jnp.uint32(HI_MASK), F32).astype(BF16)
        dh = xa.shape[1]
        a = _dot(xa, w1b_ref[0:dh]) + _dot(xb, w1b_ref[dh:2 * dh])
        c = _dot(xa, w3b_ref[0:dh]) + _dot(xb, w3b_ref[dh:2 * dh])
        y_ref[...] = _dot((_silu(a) * c).astype(BF16), w2b_ref[...])

    @pl.when(i >= nu_ref[0])
    def _():
        y_ref[...] = jnp.zeros_like(y_ref)


def _experts(blk_expert, n_used, xd, w1, w3, w2):
    n_rows, dh = xd.shape
    _, d, de = w1.shape
    n_blocks = n_rows // MOE_BLOCK
    return pl.pallas_call(
        _experts_kernel,
        grid_spec=pltpu.PrefetchScalarGridSpec(
            num_scalar_prefetch=2,
            grid=(n_blocks,),
            in_specs=[pl.BlockSpec((MOE_BLOCK, dh), lambda i, be, nu: (i, 0)),
                      pl.BlockSpec((1, d, de), lambda i, be, nu: (be[i], 0, 0)),
                      pl.BlockSpec((1, d, de), lambda i, be, nu: (be[i], 0, 0)),
                      pl.BlockSpec((1, de, d), lambda i, be, nu: (be[i], 0, 0))],
            out_specs=pl.BlockSpec((MOE_BLOCK, d), lambda i, be, nu: (i, 0)),
            scratch_shapes=[pltpu.VMEM((d, de), BF16), pltpu.VMEM((d, de), BF16), pltpu.VMEM((de, d), BF16)]),
        out_shape=jax.ShapeDtypeStruct((n_rows, d), F32),
        compiler_params=pltpu.CompilerParams(
            dimension_semantics=("arbitrary",), vmem_limit_bytes=VMEM_LIMIT),
        name="experts",
    )(blk_expert, n_used, xd, w1, w3, w2)


def _combine_kernel(dest_ref, dnext_ref, rw_ref, x1_ref, mod_ref, y_ref, o_ref, buf_ref, sems):
    i = pl.program_id(0)
    slot = i % 2

    def gather(idx_ref, slot_, start):
        def one(r, k):
            cp = pltpu.make_async_copy(y_ref.at[pl.ds(idx_ref[0, k, r], 1)],
                                       buf_ref.at[slot_, k, pl.ds(r, 1)], sems.at[slot_])
            if start:
                cp.start()
            else:
                cp.wait()

        _for_each_row(one)

    @pl.when(i == 0)
    def _():
        gather(dest_ref, slot, True)

    @pl.when(i + 1 < pl.num_programs(0))
    def _():
        gather(dnext_ref, 1 - slot, True)

    gather(dest_ref, slot, False)
    w = rw_ref[...]
    g2 = mod_ref[0][5:6]
    o_ref[...] = x1_ref[...] + g2 * (w[:, 0:1] * buf_ref[slot, 0] + w[:, 1:2] * buf_ref[slot, 1])


def _combine(dest_t, rw, x1, mod, y_disp, seq):
    t, d = x1.shape
    tiles_per_seq = seq // TOK_TILE
    n_tiles = t // TOK_TILE
    return pl.pallas_call(
        _combine_kernel,
        grid=(n_tiles,),
        in_specs=[pl.BlockSpec((1, 8, TOK_TILE), lambda i: (i, 0, 0), memory_space=pltpu.SMEM),
                  pl.BlockSpec((1, 8, TOK_TILE), lambda i: (jnp.minimum(i + 1, n_tiles - 1), 0, 0),
                               memory_space=pltpu.SMEM),
                  pl.BlockSpec((TOK_TILE, LANES), lambda i: (i, 0)),
                  pl.BlockSpec((TOK_TILE, d), lambda i: (i, 0)),
                  pl.BlockSpec((1, 6, d), lambda i: (i // tiles_per_seq, 0, 0)),
                  pl.BlockSpec(memory_space=pl.ANY)],
        out_specs=pl.BlockSpec((TOK_TILE, d), lambda i: (i, 0)),
        scratch_shapes=[pltpu.VMEM((2, 2, TOK_TILE, d), F32), pltpu.SemaphoreType.DMA((2,))],
        out_shape=jax.ShapeDtypeStruct((t, d), F32),
        compiler_params=pltpu.CompilerParams(
            dimension_semantics=("arbitrary",), vmem_limit_bytes=VMEM_LIMIT),
        name="combine",
    )(dest_t, dest_t, rw, x1, mod, y_disp)


def _moe(x1, h2p, ri, rw, cnt, mod, w1, w3, w2):
    b, s, d = x1.shape
    t = b * s
    a = 2 * t
    counts = cnt[0, ROUTE_E0:ROUTE_E0 + N_EXPERTS].astype(I32)
    padded = (counts + MOE_BLOCK - 1) // MOE_BLOCK * MOE_BLOCK
    pad_end = jnp.cumsum(padded)
    pad_start = (pad_end - padded).astype(I32)
    n_blocks = (a + N_EXPERTS * (MOE_BLOCK - 1)) // MOE_BLOCK + 1
    blk_start = jnp.arange(n_blocks, dtype=I32) * MOE_BLOCK
    blk_expert = jnp.minimum(
        jnp.sum((pad_end[None, :] <= blk_start[:, None]).astype(I32), axis=1), N_EXPERTS - 1)
    n_used = (pad_end[-1:] // MOE_BLOCK).astype(I32)
    dest_t = _dest_rows(pad_start, ri).reshape(t // TOK_TILE, 8, TOK_TILE)
    xd = _dispatch(dest_t, h2p.reshape(t, d // 2), n_blocks * MOE_BLOCK)
    y_disp = _experts(blk_expert, n_used, xd, w1, w3, w2)
    out = _combine(dest_t, rw.reshape(t, LANES), x1.reshape(t, d), mod, y_disp, s)
    return out.reshape(b, s, d)


def kernel(x, c, w_ada, b_ada, norm1_w, w_in, q_norm_w, k_norm_w, cmp_pos_k, cmp_pos_v, cmp_k_w1, cmp_k_w2,
           cmp_v_w1, cmp_v_w2, conv_w, out_norm_w, w_out, rel_bias, norm2_w, w_group, b_group, w_expert,
           b_expert, w1, w3, w2):
    t_hi, t_lo, t_c = _bias_tables(rel_bias)
    for l in range(w_ada.shape[0]):
        mod = _ada(c, w_ada[l], b_ada[l])
        q, kvc, kaug, vaug, kwin, vwaug, gates, yconv = _inproj(
            x, mod, norm1_w[l], w_in[l], q_norm_w[l], k_norm_w[l], conv_w[l], out_norm_w[l])
        rows = _kvc_rows(kvc)
        k_c = _compress(rows, 0, cmp_k_w1[l], cmp_k_w2[l], cmp_pos_k[l], k_norm_w[l, 0], True)
        v_c = _compress(rows, 1, cmp_v_w1[l], cmp_v_w2[l], cmp_pos_v[l], k_norm_w[l, 0], False)
        o_cmp, pen = _cmpsel(q, k_c, v_c, t_c)
        mixa = _slcwin(q, pen, kaug, vaug, kwin, vwaug, o_cmp, gates, t_hi, t_lo, out_norm_w[l])
        x1, h2p, ri, rw, cnt = _outproj(mixa, yconv, w_out[l], x, mod, norm2_w[l],
                                        w_group[l], b_group[l], w_expert[l], b_expert[l])
        x = _moe(x1, h2p, ri, rw, cnt, mod, w1[l], w3[l], w2[l])
    return x
```

```python
import functools
import math

import numpy as np
import jax
import jax.numpy as jnp
from jax import lax
from jax.experimental import pallas as pl
from jax.experimental.pallas import tpu as pltpu

F32 = jnp.float32
BF16 = jnp.bfloat16
F8 = jnp.float8_e4m3fn
P_SCALE = 256.0
F8_TARGET = 240.0
I32 = jnp.int32

HEAD_DIM = 64
N_HEADS = 8
N_KV = 2
HPG = N_HEADS // N_KV
D_ATTN = N_HEADS * HEAD_DIM
D_CONV = 512
KV_DIM = N_KV * HEAD_DIM
CMP_LEN = 32
CMP_STRIDE = 16
CMP_HIDDEN = 256
SLC_LEN = 64
N_SELECT = 16
WINDOW = 512
N_BUCKETS = 32
MAX_DISTANCE = 128
N_GROUPS = 8
EXPERTS_PER_GROUP = 8
N_EXPERTS = N_GROUPS * EXPERTS_PER_GROUP
MOE_BLOCK = 256
FORCED_SCORE = 1e4
EPS = 1e-6

LANES = 128
QB = 512
ROWS_Q = HPG * QB
CMP_CHUNK = 256
FAR_GROUP = 4
PEN_BLOCKS = 128
HALF_KEYS = PEN_BLOCKS * SLC_LEN
NEG = -1e30
PICKED = -3e38
VMEM_LIMIT = 56 * 1024 * 1024


def _dot(a, b):
    return jnp.dot(a, b, preferred_element_type=F32)


def _dot_nt(a, b):
    return lax.dot_general(a, b, (((1,), (1,)), ((), ())), preferred_element_type=F32)


def _hi_lo(a):
    hi = a.astype(BF16)
    lo = (a - hi.astype(F32)).astype(BF16)
    return hi, lo


def _dot3(a, b_hi, b_lo):
    a_hi, a_lo = _hi_lo(a)
    return _dot(a_hi, b_hi) + (_dot(a_lo, b_hi) + _dot(a_hi, b_lo))


def _group_sumsq(v, gm):
    hi, lo = _hi_lo(v * v)
    return _dot(hi, gm) + _dot(lo, gm)


def _silu(v):
    return v * jax.nn.sigmoid(v)


def _ada_kernel(c_ref, w_ref, b_ref, o_ref):
    cond = _silu(c_ref[...])
    w_hi, w_lo = _hi_lo(w_ref[...])
    o_ref[...] = _dot3(cond, w_hi, w_lo) + b_ref[...]


def _ada(c, w_ada, b_ada):
    b, d = c.shape
    n = w_ada.shape[1]
    tn = n // 4
    c8 = jnp.pad(c, ((0, 8 - b), (0, 0)))
    out = pl.pallas_call(
        _ada_kernel,
        grid=(n // tn,),
        in_specs=[pl.BlockSpec((8, d), lambda i: (0, 0)),
                  pl.BlockSpec((d, tn), lambda i: (0, i)),
                  pl.BlockSpec((1, tn), lambda i: (0, i))],
        out_specs=pl.BlockSpec((8, tn), lambda i: (0, i)),
        out_shape=jax.ShapeDtypeStruct((8, n), F32),
        compiler_params=pltpu.CompilerParams(vmem_limit_bytes=VMEM_LIMIT),
        name="ada",
    )(c8, w_ada, b_ada.reshape(1, n))
    return out[:b].reshape(b, 6, d)


def _inproj_kernel(x_ref, xh_ref, mod_ref, n1w_ref, wq_ref, wkv_ref, wg_ref, wbcu_ref,
                   qnw_ref, knw_ref, convw_ref, onw_ref, gm_ref,
                   q_ref, kvc_ref, kaug_ref, vaug_ref, kwin_ref, vwaug_ref, gates_ref, yconv_ref,
                   ext_ref, *, tm):
    i = pl.program_id(1)
    mod = mod_ref[0]
    sh1 = mod[0:1]
    sc1 = mod[1:2]

    def norm_mod(v):
        ms = jnp.mean(v * v, axis=-1, keepdims=True)
        y = (v * lax.rsqrt(ms + EPS)) * n1w_ref[...]
        return (y * (1.0 + sc1) + sh1).astype(BF16)

    h = norm_mod(x_ref[0])
    gm = gm_ref[...]
    gm_kv = gm_ref[0:KV_DIM, 0:KV_DIM]

    q = _dot(h, wq_ref[...])
    qn = q * lax.rsqrt(_group_sumsq(q, gm) * (1.0 / HEAD_DIM) + EPS) * qnw_ref[...]
    qn = (qn * (HEAD_DIM ** -0.5)).astype(BF16)
    for hd in range(N_HEADS):
        q_ref[0, hd] = qn[:, hd * HEAD_DIM:(hd + 1) * HEAD_DIM]

    kv = _dot(h, wkv_ref[...])
    kvc_ref[0] = kv[:, 0:2 * KV_DIM]
    ks = kv[:, 2 * KV_DIM:3 * KV_DIM]
    vs = kv[:, 3 * KV_DIM:4 * KV_DIM]
    kw = kv[:, 4 * KV_DIM:5 * KV_DIM]
    vw = kv[:, 5 * KV_DIM:6 * KV_DIM]
    ksn = ks * lax.rsqrt(_group_sumsq(ks, gm_kv) * (1.0 / HEAD_DIM) + EPS) * knw_ref[0:1]
    kwn = kw * lax.rsqrt(_group_sumsq(kw, gm_kv) * (1.0 / HEAD_DIM) + EPS) * knw_ref[1:2]

    lane = lax.broadcasted_iota(I32, (tm, LANES), 1)
    row = lax.broadcasted_iota(I32, (tm, LANES), 0)
    blk = ((i * tm + row) // SLC_LEN) % PEN_BLOCKS
    onehot = jnp.where(lane == blk, 1.0, 0.0).astype(BF16)
    ones_col = jnp.where(lax.broadcasted_iota(I32, (tm, HEAD_DIM), 1) == 0, 1.0, 0.0).astype(BF16)
    zeros64 = jnp.zeros((tm, HEAD_DIM), BF16)
    for g in range(N_KV):
        sl = slice(g * HEAD_DIM, (g + 1) * HEAD_DIM)
        kaug_ref[0, g, :, 0:LANES] = onehot
        kaug_ref[0, g, :, LANES:LANES + HEAD_DIM] = ksn[:, sl].astype(BF16)
        kaug_ref[0, g, :, LANES + HEAD_DIM:2 * LANES] = zeros64
        vaug_ref[0, g, :, 0:HEAD_DIM] = vs[:, sl].astype(BF16)
        vaug_ref[0, g, :, HEAD_DIM:LANES] = ones_col
        kwin_ref[0, g] = kwn[:, sl].astype(BF16)
        vwaug_ref[0, g, :, 0:HEAD_DIM] = vw[:, sl].astype(BF16)
        vwaug_ref[0, g, :, HEAD_DIM:LANES] = ones_col

    gates_ref[0] = jax.nn.sigmoid(_dot(h, wg_ref[...]))

    bcu = _dot(h, wbcu_ref[...])
    bg = bcu[:, 0:D_CONV]
    cu = bcu[:, D_CONV:2 * D_CONV] * bcu[:, 2 * D_CONV:3 * D_CONV]
    hh = norm_mod(xh_ref[0])
    cuh = _dot(hh, wbcu_ref[:, D_CONV:2 * D_CONV]) * _dot(hh, wbcu_ref[:, 2 * D_CONV:3 * D_CONV])
    cuh = jnp.where(i > 0, cuh, 0.0)
    ext_ref[0:8] = cuh
    ext_ref[8:8 + tm] = cu
    cw = convw_ref[...]
    y = bg * (cw[0:1] * ext_ref[6:6 + tm] + cw[1:2] * ext_ref[7:7 + tm] + cw[2:3] * cu)
    yn = y * lax.rsqrt(_group_sumsq(y, gm) * (1.0 / HEAD_DIM) + EPS) * onw_ref[...]
    yconv_ref[0] = yn.astype(BF16)


def _inproj(x, mod, norm1_w, w_in, q_norm_w, k_norm_w, conv_w, out_norm_w, tm=512):
    b, s, d = x.shape
    o_kv = D_ATTN
    o_g = o_kv + 6 * KV_DIM
    o_b = o_g + 3 * N_HEADS
    wq = w_in[:, 0:o_kv].astype(BF16)
    wkv = w_in[:, o_kv:o_g].astype(BF16)
    wg = jnp.pad(w_in[:, o_g:o_b].reshape(d, N_KV, 3 * HPG),
                 ((0, 0), (0, 0), (0, LANES - 3 * HPG))).reshape(d, N_KV * LANES).astype(BF16)
    wbcu = w_in[:, o_b:].astype(BF16)
    qnw = jnp.tile(q_norm_w, N_HEADS).reshape(1, D_ATTN)
    knw = jnp.stack([jnp.tile(k_norm_w[1], N_KV), jnp.tile(k_norm_w[2], N_KV)])
    onw = out_norm_w[D_ATTN:].reshape(1, D_CONV)
    gidx = np.arange(D_ATTN) // HEAD_DIM
    gm = jnp.asarray((gidx[:, None] == gidx[None, :]).astype(np.float32), dtype=BF16)

    full = lambda shape: pl.BlockSpec(shape, lambda bi, i: (0,) * len(shape))
    outs = pl.pallas_call(
        functools.partial(_inproj_kernel, tm=tm),
        grid=(b, s // tm),
        in_specs=[pl.BlockSpec((1, tm, d), lambda bi, i: (bi, i, 0)),
                  pl.BlockSpec((1, 8, d), lambda bi, i: (bi, jnp.maximum(i * (tm // 8) - 1, 0), 0)),
                  pl.BlockSpec((1, 6, d), lambda bi, i: (bi, 0, 0)),
                  full((1, d)), full(wq.shape), full(wkv.shape), full(wg.shape), full(wbcu.shape),
                  full(qnw.shape), full(knw.shape), full(conv_w.shape), full(onw.shape), full(gm.shape)],
        out_specs=[pl.BlockSpec((1, N_HEADS, tm, HEAD_DIM), lambda bi, i: (bi, 0, i, 0)),
                   pl.BlockSpec((1, tm, 2 * KV_DIM), lambda bi, i: (bi, i, 0)),
                   pl.BlockSpec((1, N_KV, tm, 2 * LANES), lambda bi, i: (bi, 0, i, 0)),
                   pl.BlockSpec((1, N_KV, tm, LANES), lambda bi, i: (bi, 0, i, 0)),
                   pl.BlockSpec((1, N_KV, tm, HEAD_DIM), lambda bi, i: (bi, 0, i, 0)),
                   pl.BlockSpec((1, N_KV, tm, LANES), lambda bi, i: (bi, 0, i, 0)),
                   pl.BlockSpec((1, tm, N_KV * LANES), lambda bi, i: (bi, i, 0)),
                   pl.BlockSpec((1, tm, D_CONV), lambda bi, i: (bi, i, 0))],
        out_shape=[jax.ShapeDtypeStruct((b, N_HEADS, s, HEAD_DIM), BF16),
                   jax.ShapeDtypeStruct((b, s, 2 * KV_DIM), F32),
                   jax.ShapeDtypeStruct((b, N_KV, s, 2 * LANES), BF16),
                   jax.ShapeDtypeStruct((b, N_KV, s, LANES), BF16),
                   jax.ShapeDtypeStruct((b, N_KV, s, HEAD_DIM), BF16),
                   jax.ShapeDtypeStruct((b, N_KV, s, LANES), BF16),
                   jax.ShapeDtypeStruct((b, s, N_KV * LANES), F32),
                   jax.ShapeDtypeStruct((b, s, D_CONV), BF16)],
        scratch_shapes=[pltpu.VMEM((tm + 8, D_CONV), F32)],
        compiler_params=pltpu.CompilerParams(
            dimension_semantics=("parallel", "arbitrary"), vmem_limit_bytes=VMEM_LIMIT),
        name="inproj",
    )(x, x, mod, norm1_w.reshape(1, d), wq, wkv, wg, wbcu, qnw, knw, conv_w, onw, gm)
    return outs


def _compress_kernel(r_ref, w1_ref, w2_ref, pos_ref, nw_ref, o_ref, *, norm):
    r = r_ref[0, 0, 0]
    nr = r.shape[0]
    half = (CMP_LEN // 2) * HEAD_DIM
    w1a = w1_ref[0:half, :]
    w1b = w1_ref[half:2 * half, :]
    pos = pos_ref[...]
    cst = (_dot(jnp.broadcast_to(pos[0:1], (8, half)).astype(BF16), w1a)
           + _dot(jnp.broadcast_to(pos[1:2], (8, half)).astype(BF16), w1b))[0:1]
    upper = pltpu.roll(_dot(r, w1b), nr - 1, 0)
    hid = _dot(r, w1a) + upper + cst
    out = _dot(_silu(hid).astype(BF16), w2_ref[...])
    if norm:
        ms = jnp.mean(out * out, axis=-1, keepdims=True)
        out = out * lax.rsqrt(ms + EPS) * nw_ref[...]
    o_ref[0, 0] = out.astype(BF16)


def _compress(rows, kv_index, w1, w2, pos_emb, norm_w, norm):
    b, _, g, nr, width = rows.shape
    return pl.pallas_call(
        functools.partial(_compress_kernel, norm=norm),
        grid=(b, g),
        in_specs=[pl.BlockSpec((1, 1, 1, nr, width), lambda bi, gi: (bi, kv_index, gi, 0, 0)),
                  pl.BlockSpec(w1.shape, lambda bi, gi: (0, 0)),
                  pl.BlockSpec(w2.shape, lambda bi, gi: (0, 0)),
                  pl.BlockSpec((2, width), lambda bi, gi: (0, 0)),
                  pl.BlockSpec((1, HEAD_DIM), lambda bi, gi: (0, 0))],
        out_specs=pl.BlockSpec((1, 1, nr, HEAD_DIM), lambda bi, gi: (bi, gi, 0, 0)),
        out_shape=jax.ShapeDtypeStruct((b, g, nr, HEAD_DIM), BF16),
        compiler_params=pltpu.CompilerParams(vmem_limit_bytes=VMEM_LIMIT),
        name="compress_k" if norm else "compress_v",
    )(rows, w1.astype(BF16), w2.astype(BF16), pos_emb.reshape(2, width), norm_w.reshape(1, HEAD_DIM))


def _kvc_rows(kvc):
    b, s, _ = kvc.shape
    r = kvc.reshape(b, s, 2, N_KV, HEAD_DIM).transpose(0, 2, 3, 1, 4)
    return r.reshape(b, 2, N_KV, s // CMP_STRIDE, CMP_STRIDE * HEAD_DIM).astype(BF16)


def _bucket_np(dist):
    n = np.maximum(dist, 0)
    max_exact = N_BUCKETS // 2
    nf = np.maximum(n, 1).astype(np.float32)
    large = max_exact + (np.log(nf / np.float32(max_exact)) / np.float32(math.log(MAX_DISTANCE / max_exact))
                         * np.float32(N_BUCKETS - max_exact)).astype(np.int32)
    large = np.minimum(large, N_BUCKETS - 1)
    return np.where(n < max_exact, n, large)


_BUCKET_START = [int(np.argmax(_bucket_np(np.arange(MAX_DISTANCE + 1)) >= k)) for k in range(N_BUCKETS)]
CMP_DIST0 = CMP_STRIDE * CMP_CHUNK - (CMP_LEN - 1) - QB


def _bias_tables_kernel(rbt_ref, thi_ref, tlo_ref, tc_ref):
    head = pl.program_id(0) * HPG + pl.program_id(1)
    far = rbt_ref[head, N_BUCKETS - 1]

    def table(dist):
        v = jnp.zeros(dist.shape, F32)
        for k in range(N_BUCKETS - 2, -1, -1):
            v = jnp.where(dist < _BUCKET_START[k + 1], rbt_ref[head, k] - far, v)
        return jnp.where(dist >= 0, v, NEG)

    def dist(shape, row_mul, col_mul, off):
        return (lax.broadcasted_iota(I32, shape, 0) * row_mul
                + lax.broadcasted_iota(I32, shape, 1) * col_mul + off)

    thi_ref[0] = table(dist((QB, QB), 1, -1, 0))
    tlo_ref[0] = table(dist((QB, LANES), 1, -1, LANES))
    tc_ref[0] = table(dist((QB, CMP_CHUNK), 1, -CMP_STRIDE, CMP_DIST0))


def _bias_tables(rel_bias):
    return pl.pallas_call(
        _bias_tables_kernel,
        grid=(N_KV, HPG),
        in_specs=[pl.BlockSpec(memory_space=pltpu.SMEM)],
        out_specs=[pl.BlockSpec((1, QB, QB), lambda g, h: (g, h, 0)),
                   pl.BlockSpec((1, QB, LANES), lambda g, h: (g, h, 0)),
                   pl.BlockSpec((1, QB, CMP_CHUNK), lambda g, h: (g, h, 0))],
        out_shape=[jax.ShapeDtypeStruct((N_KV, ROWS_Q, QB), F32),
                   jax.ShapeDtypeStruct((N_KV, ROWS_Q, LANES), F32),
                   jax.ShapeDtypeStruct((N_KV, ROWS_Q, CMP_CHUNK), F32)],
        name="bias_tables",
    )(rel_bias.T)


def _cmpsel_kernel(q_ref, kc_ref, vc_ref, tab_ref, ovt_ref, ocmp_ref, pen_ref,
                   m_ref, l_ref, acc_ref, imp_ref, score_ref, *, nsbp, n_sel):
    j = pl.program_id(2)
    q4 = q_ref[0].reshape(ROWS_Q, HEAD_DIM)
    nq16 = QB // CMP_STRIDE
    n_chunks = (nq16 * (j + 1) - 1) // CMP_CHUNK + 1
    ni = lax.broadcasted_iota(I32, (1, CMP_CHUNK), 1)

    def chunk_start(m):
        return pl.multiple_of(nq16 * (j + 1) - CMP_CHUNK * m, nq16)

    def logits(m):
        n0p = chunk_start(m)
        s = _dot_nt(q4, kc_ref[0, 0, pl.ds(n0p, CMP_CHUNK), :])
        return jnp.where(ni + n0p >= CMP_CHUNK, s, NEG)

    s0 = logits(0) + tab_ref[0]
    m_ref[...] = s0
    l_ref[...] = jnp.ones_like(l_ref)

    def stats(m, carry):
        s = logits(m)
        m_prev = m_ref[...]
        m_new = jnp.maximum(m_prev, s)
        l_ref[...] = l_ref[...] * jnp.exp(m_prev - m_new) + jnp.exp(s - m_new)
        m_ref[...] = m_new
        return carry

    lax.fori_loop(1, n_chunks, stats, 0)

    m_lane = m_ref[...]
    m_fin = jnp.broadcast_to(jnp.max(m_lane, axis=1, keepdims=True), (ROWS_Q, CMP_CHUNK))
    l_fin = jnp.sum(l_ref[...] * jnp.exp(m_lane - m_fin), axis=1, keepdims=True)
    inv = jnp.where(m_fin > 0.5 * NEG, 1.0 / l_fin, 0.0)
    m_ref[...] = m_fin
    l_ref[...] = inv
    acc_ref[...] = jnp.zeros_like(acc_ref)
    imp_ref[...] = jnp.zeros_like(imp_ref)

    def accumulate(m, s):
        n0p = chunk_start(m)
        p = jnp.exp(s - m_ref[...]) * l_ref[...]
        acc_ref[...] += _dot(p.astype(BF16), vc_ref[0, 0, pl.ds(n0p, CMP_CHUNK), :])
        psum = p[0:QB]
        for hd in range(1, HPG):
            psum = psum + p[hd * QB:(hd + 1) * QB]
        hi, lo = _hi_lo(psum)
        contrib = _dot_nt(ovt_ref[...], hi) + _dot_nt(ovt_ref[...], lo)
        r0 = pl.multiple_of(n0p // 4, 8)
        imp_ref[pl.ds(r0, OV_ROWS), :] += contrib

    accumulate(0, s0)

    def second(m, carry):
        accumulate(m, logits(m))
        return carry

    lax.fori_loop(1, n_chunks, second, 0)
    ocmp_ref[0, 0] = acc_ref[...].reshape(HPG, QB, HEAD_DIM)

    imp = imp_ref[IMP_PAD:IMP_PAD + nsbp, :]
    srow = lax.broadcasted_iota(I32, (nsbp, QB), 0)
    qcol = lax.broadcasted_iota(I32, (nsbp, QB), 1)
    cur = (QB // SLC_LEN) * j + qcol // SLC_LEN
    forced = (srow == 0) | (srow == cur) | (srow == cur - 1)
    score_ref[...] = jnp.where(forced, -1.0, jnp.where(srow <= cur, imp, -1.0))

    quarter = nsbp // 4
    case = ((QB // SLC_LEN) * (j + 1) - 1) // quarter
    for k in range(4):
        rows = quarter * (k + 1)

        @pl.when(case == k)
        def _(rows=rows):
            srow_k = lax.broadcasted_iota(I32, (rows, QB), 0)

            def pick(r, carry):
                sc = score_ref[0:rows]
                mx = jnp.max(sc, axis=0, keepdims=True)
                idx = jnp.min(jnp.where(sc == mx, srow_k, nsbp), axis=0, keepdims=True)
                score_ref[0:rows] = jnp.where(srow_k == idx, PICKED, sc)
                return carry

            lax.fori_loop(0, max(n_sel - 3, 0), pick, 0)
    pen_other = jnp.where(srow <= cur, jnp.where(score_ref[...] == PICKED, 0.0, NEG), NEG)
    pen_t = jnp.where(forced, 0.0, pen_other).T
    for hf in range(nsbp // PEN_BLOCKS):
        pen_ref[0, 0, hf] = pen_t[:, hf * PEN_BLOCKS:(hf + 1) * PEN_BLOCKS].astype(BF16)


IMP_PAD = CMP_CHUNK // 4
OV_ROWS = IMP_PAD + 8


def _overlap_t():
    n = np.arange(CMP_CHUNK)
    s = np.arange(OV_ROWS)
    first = (n * CMP_STRIDE) // SLC_LEN
    last = (n * CMP_STRIDE + CMP_LEN - 1) // SLC_LEN
    ov = (s[:, None] >= first[None, :]) & (s[:, None] <= last[None, :])
    return jnp.asarray(ov.astype(np.float32), dtype=BF16)


def _cmpsel(q, k_c, v_c, tab):
    b, _, s, _ = q.shape
    nr = k_c.shape[2]
    nq = s // QB
    nsb = s // SLC_LEN
    nh = -(-nsb // PEN_BLOCKS)
    nsbp = nh * PEN_BLOCKS
    n_sel = min(N_SELECT, nsb)
    pad = ((0, 0), (0, 0), (CMP_CHUNK, 0), (0, 0))
    kc_p = jnp.pad(k_c, pad)
    vc_p = jnp.pad(v_c, pad)
    ovt = _overlap_t()
    return pl.pallas_call(
        functools.partial(_cmpsel_kernel, nsbp=nsbp, n_sel=n_sel),
        grid=(b, N_KV, nq),
        in_specs=[pl.BlockSpec((1, HPG, QB, HEAD_DIM), lambda bi, g, j: (bi, g, j, 0)),
                  pl.BlockSpec((1, 1, nr + CMP_CHUNK, HEAD_DIM), lambda bi, g, j: (bi, g, 0, 0)),
                  pl.BlockSpec((1, 1, nr + CMP_CHUNK, HEAD_DIM), lambda bi, g, j: (bi, g, 0, 0)),
                  pl.BlockSpec((1, ROWS_Q, CMP_CHUNK), lambda bi, g, j: (g, 0, 0)),
                  pl.BlockSpec(ovt.shape, lambda bi, g, j: (0, 0))],
        out_specs=[pl.BlockSpec((1, 1, HPG, QB, HEAD_DIM), lambda bi, g, j: (bi, g, 0, j, 0)),
                   pl.BlockSpec((1, 1, nh, QB, PEN_BLOCKS), lambda bi, g, j: (bi, g, 0, j, 0))],
        out_shape=[jax.ShapeDtypeStruct((b, N_KV, HPG, s, HEAD_DIM), F32),
                   jax.ShapeDtypeStruct((b, N_KV, nh, s, PEN_BLOCKS), BF16)],
        scratch_shapes=[pltpu.VMEM((ROWS_Q, CMP_CHUNK), F32), pltpu.VMEM((ROWS_Q, CMP_CHUNK), F32),
                        pltpu.VMEM((ROWS_Q, HEAD_DIM), F32),
                        pltpu.VMEM((IMP_PAD + nsbp + 8, QB), F32),
                        pltpu.VMEM((nsbp, QB), F32)],
        compiler_params=pltpu.CompilerParams(
            dimension_semantics=("parallel", "parallel", "arbitrary"), vmem_limit_bytes=VMEM_LIMIT),
        name="cmpsel",
    )(q, kc_p, vc_p, tab, ovt)


def _to_f8_kernel(v_ref, o_ref, sc_ref):
    v = v_ref[0, 0].astype(F32)
    lane = lax.broadcasted_iota(I32, (1, LANES), 1)
    cmax = jnp.max(jnp.abs(v), axis=0, keepdims=True)
    scale = jnp.where(lane < HEAD_DIM, jnp.maximum(cmax, 1e-30) * (1.0 / F8_TARGET), 1.0)
    o_ref[0, 0] = (v / scale).astype(F8)
    sc_ref[0, 0] = jnp.broadcast_to(scale, (8, LANES))


def _to_f8(vaug):
    b, g, s, w = vaug.shape
    return pl.pallas_call(
        _to_f8_kernel,
        grid=(b, g),
        in_specs=[pl.BlockSpec((1, 1, s, w), lambda bi, gi: (bi, gi, 0, 0))],
        out_specs=[pl.BlockSpec((1, 1, s, w), lambda bi, gi: (bi, gi, 0, 0)),
                   pl.BlockSpec((1, 1, 8, w), lambda bi, gi: (bi, gi, 0, 0))],
        out_shape=[jax.ShapeDtypeStruct((b, g, s, w), F8), jax.ShapeDtypeStruct((b, g, 8, w), F32)],
        compiler_params=pltpu.CompilerParams(vmem_limit_bytes=VMEM_LIMIT),
        name="to_f8",
    )(vaug)


def _slcwin_kernel(q_ref, pen_ref, kaug_ref, vaug_ref, vsc_ref, kwp_ref, kwc_ref, vwp_ref, vwc_ref,
                   vwsc_ref, thi_ref, tlo_ref, ocmp_ref, gates_ref, onw_ref, o_ref,
                   lhs_ref, m_ref, acc_ref):
    j = pl.program_id(2)
    q4 = q_ref[0].reshape(ROWS_Q, HEAD_DIM)
    near = QB - LANES

    lhs_ref[:, LANES:LANES + HEAD_DIM] = q4
    lhs_ref[:, LANES + HEAD_DIM:2 * LANES] = jnp.zeros((ROWS_Q, HEAD_DIM), BF16)

    def set_pen(half):
        p = pen_ref[0, 0, half]
        for hd in range(HPG):
            lhs_ref[hd * QB:(hd + 1) * QB, 0:LANES] = p

    def reset():
        m_ref[...] = jnp.full_like(m_ref, NEG)
        acc_ref[...] = jnp.zeros_like(acc_ref)

    def update(s, v):
        sb = s.astype(BF16)
        m_prev = m_ref[...]
        m_new = jnp.maximum(m_prev, jnp.max(sb, axis=1, keepdims=True).astype(F32))
        x = sb - jnp.concatenate([m_new.astype(BF16)] * (s.shape[1] // LANES), axis=1)
        p = (jnp.exp(x) * P_SCALE).astype(F8)
        acc_ref[...] = acc_ref[...] * jnp.exp(m_prev - m_new) + _dot(p, v)
        m_ref[...] = m_new

    def add_near_bias(s):
        return jnp.concatenate([s[:, :near], s[:, near:] + tlo_ref[0]], axis=1)

    def result(scale_ref):
        acc = acc_ref[...]
        return acc[:, 0:HEAD_DIM] * scale_ref[0, 0, 0:1, 0:HEAD_DIM] / acc[:, HEAD_DIM:HEAD_DIM + 1]

    def slc_chunk(c, bias):
        k0 = pl.multiple_of(c * QB, QB)
        s = _dot_nt(lhs_ref[...], kaug_ref[0, 0, pl.ds(k0, QB), :])
        update(bias(s), vaug_ref[0, 0, pl.ds(k0, QB), :])

    chunks_per_half = HALF_KEYS // QB
    assert chunks_per_half % FAR_GROUP == 0
    reset()
    half_hi = j // chunks_per_half
    set_pen(half_hi)
    slc_chunk(j, lambda s: s + thi_ref[0])

    @pl.when(j > 0)
    def _():
        half_lo = (j - 1) // chunks_per_half

        @pl.when(half_lo != half_hi)
        def _():
            set_pen(half_lo)

        slc_chunk(j - 1, add_near_bias)

    n_far = jnp.maximum(j - 1, 0)

    def enter_half(c):
        @pl.when(c % chunks_per_half == 0)
        def _():
            set_pen(c // chunks_per_half)

    def far_group(i4, carry):
        c = FAR_GROUP * i4
        enter_half(c)
        for u in range(FAR_GROUP):
            slc_chunk(c + u, lambda s: s)
        return carry

    n_groups = n_far // FAR_GROUP
    lax.fori_loop(0, n_groups, far_group, 0)

    def far_single(c, carry):
        enter_half(c)
        slc_chunk(c, lambda s: s)
        return carry

    lax.fori_loop(n_groups * FAR_GROUP, n_far, far_single, 0)

    o_slc = result(vsc_ref)

    reset()
    update(_dot_nt(q4, kwc_ref[0, 0]) + thi_ref[0], vwc_ref[0, 0])

    @pl.when(j > 0)
    def _():
        s = _dot_nt(q4, kwp_ref[0, 0])
        qi = lax.broadcasted_iota(I32, (ROWS_Q, QB), 0) & (QB - 1)
        kk = lax.broadcasted_iota(I32, (ROWS_Q, QB), 1)
        s = jnp.where(kk > qi, s, NEG)
        update(add_near_bias(s), vwp_ref[0, 0])

    o_win = result(vwsc_ref)

    o_cmp = ocmp_ref[0, 0].reshape(ROWS_Q, HEAD_DIM)
    gt = gates_ref[0]
    outs = []
    for hd in range(HPG):
        rs = slice(hd * QB, (hd + 1) * QB)
        o = (gt[:, 3 * hd:3 * hd + 1] * o_cmp[rs] + gt[:, 3 * hd + 1:3 * hd + 2] * o_slc[rs]
             + gt[:, 3 * hd + 2:3 * hd + 3] * o_win[rs])
        ms = jnp.mean(o * o, axis=-1, keepdims=True)
        outs.append((o * lax.rsqrt(ms + EPS) * onw_ref[0, hd:hd + 1, :]).astype(BF16))
    o_ref[0] = jnp.concatenate(outs, axis=1)


def _slcwin(q, pen, kaug, vaug, vscale, kwin, vwaug, vwscale, o_cmp, gates, t_hi, t_lo, out_norm_w):
    assert QB == WINDOW
    b, _, s, _ = q.shape
    nq = s // QB
    nh = pen.shape[2]
    onw = out_norm_w[:D_ATTN].reshape(N_KV, HPG, HEAD_DIM)
    prev = lambda bi, g, j: (bi, g, jnp.maximum(j - 1, 0), 0)
    curr = lambda bi, g, j: (bi, g, j, 0)
    per_group = lambda bi, g, j: (bi, g, 0, 0)
    return pl.pallas_call(
        _slcwin_kernel,
        grid=(b, N_KV, nq),
        in_specs=[pl.BlockSpec((1, HPG, QB, HEAD_DIM), lambda bi, g, j: (bi, g, j, 0)),
                  pl.BlockSpec((1, 1, nh, QB, PEN_BLOCKS), lambda bi, g, j: (bi, g, 0, j, 0)),
                  pl.BlockSpec((1, 1, s, 2 * LANES), lambda bi, g, j: (bi, g, 0, 0)),
                  pl.BlockSpec((1, 1, s, LANES), lambda bi, g, j: (bi, g, 0, 0)),
                  pl.BlockSpec((1, 1, 8, LANES), per_group),
                  pl.BlockSpec((1, 1, QB, HEAD_DIM), prev),
                  pl.BlockSpec((1, 1, QB, HEAD_DIM), curr),
                  pl.BlockSpec((1, 1, QB, LANES), prev),
                  pl.BlockSpec((1, 1, QB, LANES), curr),
                  pl.BlockSpec((1, 1, 8, LANES), per_group),
                  pl.BlockSpec((1, ROWS_Q, QB), lambda bi, g, j: (g, 0, 0)),
                  pl.BlockSpec((1, ROWS_Q, LANES), lambda bi, g, j: (g, 0, 0)),
                  pl.BlockSpec((1, 1, HPG, QB, HEAD_DIM), lambda bi, g, j: (bi, g, 0, j, 0)),
                  pl.BlockSpec((1, QB, LANES), lambda bi, g, j: (bi, j, g)),
                  pl.BlockSpec((1, HPG, HEAD_DIM), lambda bi, g, j: (g, 0, 0))],
        out_specs=pl.BlockSpec((1, QB, HPG * HEAD_DIM), lambda bi, g, j: (bi, j, g)),
        out_shape=jax.ShapeDtypeStruct((b, s, D_ATTN), BF16),
        scratch_shapes=[pltpu.VMEM((ROWS_Q, 2 * LANES), BF16),
                        pltpu.VMEM((ROWS_Q, LANES), F32),
                        pltpu.VMEM((ROWS_Q, LANES), F32)],
        compiler_params=pltpu.CompilerParams(
            dimension_semantics=("parallel", "parallel", "arbitrary"), vmem_limit_bytes=VMEM_LIMIT),
        name="slcwin",
    )(q, pen, kaug, vaug, vscale, kwin, kwin, vwaug, vwaug, vwscale, t_hi, t_lo, o_cmp, gates, onw)


ROUTE_E0 = N_GROUPS
U32 = jnp.uint32
HI_MASK = 0xFFFF0000


def _outproj_kernel(mixa_ref, yc_ref, wo_ref, x_ref, mod_ref, n2w_ref, wrh_ref, wrl_ref, br_ref, tri_ref,
                    x1_ref, h2p_ref, ri_ref, rw_ref, cnt_ref, carry_ref, rect_ref, *, tm):
    @pl.when((pl.program_id(0) == 0) & (pl.program_id(1) == 0))
    def _():
        carry_ref[...] = jnp.zeros_like(carry_ref)

    mod = mod_ref[0]
    g1 = mod[2:3]
    sh2 = mod[3:4]
    sc2 = mod[4:5]
    d_half = wo_ref.shape[0] // 2
    mix = _dot(mixa_ref[0], wo_ref[0:d_half]) + _dot(yc_ref[0], wo_ref[d_half:2 * d_half])
    x1 = x_ref[0] + g1 * mix
    x1_ref[0] = x1
    ms = jnp.mean(x1 * x1, axis=-1, keepdims=True)
    h2 = ((x1 * lax.rsqrt(ms + EPS)) * n2w_ref[...]) * (1.0 + sc2) + sh2

    dh = h2.shape[1] // 2
    lo_w = pltpu.bitcast(h2[:, :dh].astype(BF16).astype(F32), U32)
    hi_w = pltpu.bitcast(h2[:, dh:].astype(BF16).astype(F32), U32)
    h2p_ref[0] = lax.shift_right_logical(lo_w, jnp.uint32(16)) | (hi_w & jnp.uint32(HI_MASK))

    logits = _dot3(h2, wrh_ref[...], wrl_ref[...]) + br_ref[...]
    lane = lax.broadcasted_iota(I32, (tm, LANES), 1)
    glog = jnp.where(lane < N_GROUPS, logits, NEG)
    gmax = jnp.max(glog, axis=1, keepdims=True)
    g_p = 1.0 / jnp.sum(jnp.exp(glog - gmax), axis=1, keepdims=True)
    gidx = jnp.min(jnp.where(glog == gmax, lane, LANES), axis=1, keepdims=True)
    e_lo = ROUTE_E0 + EXPERTS_PER_GROUP * gidx
    el = jnp.where(lane >= e_lo, jnp.where(lane < e_lo + EXPERTS_PER_GROUP, logits, NEG), NEG)
    m1 = jnp.max(el, axis=1, keepdims=True)
    i1 = jnp.min(jnp.where(el == m1, lane, LANES), axis=1, keepdims=True)
    el2 = jnp.where(lane == i1, NEG, el)
    m2 = jnp.max(el2, axis=1, keepdims=True)
    i2 = jnp.min(jnp.where(el2 == m2, lane, LANES), axis=1, keepdims=True)
    e2 = jnp.exp(m2 - m1)
    w1 = g_p / (1.0 + e2)
    w2 = g_p * e2 / (1.0 + e2)

    oh1 = lane == i1
    oh2 = lane == i2
    both = jnp.where(oh1, 1.0, jnp.where(oh2, 1.0, 0.0))
    base = carry_ref[0:1] + _dot(tri_ref[...], both.astype(BF16))
    r1 = jnp.sum(jnp.where(oh1, base, 0.0), axis=1, keepdims=True)
    r2 = jnp.sum(jnp.where(oh2, base, 0.0), axis=1, keepdims=True)
    carry_ref[...] = carry_ref[...] + jnp.sum(both, axis=0, keepdims=True)
    cnt_ref[...] = carry_ref[...]

    rec = jnp.where(lane == 0, (i1 - ROUTE_E0).astype(F32),
                    jnp.where(lane == 1, (i2 - ROUTE_E0).astype(F32),
                              jnp.where(lane == 2, r1, jnp.where(lane == 3, r2, 0.0))))
    rect_ref[...] = rec.T
    ri_ref[0, 0] = rect_ref[0:8, :].astype(I32)
    rw_ref[0] = jnp.where(lane == 0, w1, jnp.where(lane == 1, w2, 0.0))


def _outproj(mixa, yconv, w_out, x, mod, norm2_w, w_group, b_group, w_expert, b_expert):
    b, s, d = x.shape
    tm = TOK_TILE
    wr = jnp.pad(jnp.concatenate([w_group, w_expert], axis=1), ((0, 0), (0, LANES - N_GROUPS - N_EXPERTS)))
    wr_hi = wr.astype(BF16)
    wr_lo = (wr - wr_hi.astype(F32)).astype(BF16)
    br = jnp.pad(jnp.concatenate([b_group, b_expert]), (0, LANES - N_GROUPS - N_EXPERTS)).reshape(1, LANES)
    tri = jnp.asarray(np.tril(np.ones((tm, tm), np.float32), -1), dtype=BF16)
    full = lambda shape: pl.BlockSpec(shape, lambda bi, i: (0,) * len(shape))
    row = lambda w: pl.BlockSpec((1, tm, w), lambda bi, i: (bi, i, 0))
    return pl.pallas_call(
        functools.partial(_outproj_kernel, tm=tm),
        grid=(b, s // tm),
        in_specs=[row(D_ATTN), row(D_CONV), full((d, d)), row(d),
                  pl.BlockSpec((1, 6, d), lambda bi, i: (bi, 0, 0)),
                  full((1, d)), full((d, LANES)), full((d, LANES)), full((1, LANES)), full((tm, tm))],
        out_specs=[row(d), row(d // 2), pl.BlockSpec((1, 1, 8, tm), lambda bi, i: (bi, i, 0, 0)),
                   row(LANES), full((8, LANES))],
        out_shape=[jax.ShapeDtypeStruct((b, s, d), F32),
                   jax.ShapeDtypeStruct((b, s, d // 2), U32),
                   jax.ShapeDtypeStruct((b, s // tm, 8, tm), I32),
                   jax.ShapeDtypeStruct((b, s, LANES), F32),
                   jax.ShapeDtypeStruct((8, LANES), F32)],
        scratch_shapes=[pltpu.VMEM((8, LANES), F32), pltpu.VMEM((LANES, tm), F32)],
        compiler_params=pltpu.CompilerParams(
            dimension_semantics=("arbitrary", "arbitrary"), vmem_limit_bytes=VMEM_LIMIT),
        name="outproj",
    )(mixa, yconv, w_out.astype(BF16), x, mod, norm2_w.reshape(1, d), wr_hi, wr_lo, br, tri)


TOK_TILE = 512
ROW_UNROLL = 8


def _dest_kernel(ps_ref, ri_ref, d_ref):
    e = ri_ref[0, 0]
    acc = pltpu.roll(e, 6, 0)
    for x in range(N_EXPERTS):
        acc = jnp.where(e == x, acc + ps_ref[x], acc)
    d_ref[0, 0] = acc


def _dest_rows(pad_start, ri_c):
    b, nt, _, tm = ri_c.shape
    return pl.pallas_call(
        _dest_kernel,
        grid_spec=pltpu.PrefetchScalarGridSpec(
            num_scalar_prefetch=1,
            grid=(b, nt),
            in_specs=[pl.BlockSpec((1, 1, 8, tm), lambda bi, i, ps: (bi, i, 0, 0))],
            out_specs=pl.BlockSpec((1, 1, 8, tm), lambda bi, i, ps: (bi, i, 0, 0))),
        out_shape=jax.ShapeDtypeStruct((b, nt, 8, tm), I32),
        name="dest_rows",
    )(pad_start, ri_c)


def _for_each_row(fn):
    for base in range(0, TOK_TILE, LANES):
        def group(g, carry, base=base):
            r0 = base + pl.multiple_of(g * ROW_UNROLL, ROW_UNROLL)
            for u in range(ROW_UNROLL):
                for k in range(2):
                    fn(r0 + u, k)
            return carry

        lax.fori_loop(0, LANES // ROW_UNROLL, group, 0)


def _dispatch_kernel(dest_ref, h2p_ref, xd_in_ref, xd_ref, sem):
    del xd_in_ref

    def row_copy(r, k):
        return pltpu.make_async_copy(h2p_ref.at[pl.ds(r, 1)], xd_ref.at[pl.ds(dest_ref[0, k, r], 1)], sem)

    _for_each_row(lambda r, k: row_copy(r, k).start())
    _for_each_row(lambda r, k: row_copy(r, k).wait())


def _dispatch(dest_t, h2p, n_rows):
    t, w = h2p.shape
    xd0 = jnp.zeros((n_rows, w), U32)
    return pl.pallas_call(
        _dispatch_kernel,
        grid=(t // TOK_TILE,),
        in_specs=[pl.BlockSpec((1, 8, TOK_TILE), lambda i: (i, 0, 0), memory_space=pltpu.SMEM),
                  pl.BlockSpec((TOK_TILE, w), lambda i: (i, 0)),
                  pl.BlockSpec(memory_space=pl.ANY)],
        out_specs=pl.BlockSpec(memory_space=pl.ANY),
        scratch_shapes=[pltpu.SemaphoreType.DMA(())],
        out_shape=jax.ShapeDtypeStruct((n_rows, w), U32),
        input_output_aliases={2: 0},
        compiler_params=pltpu.CompilerParams(
            dimension_semantics=("arbitrary",), vmem_limit_bytes=VMEM_LIMIT, has_side_effects=True),
        name="dispatch",
    )(dest_t, h2p, xd0)


def _experts_kernel(be_ref, nu_ref, xd_ref, w1_ref, w3_ref, w2_ref, y_ref, w1b_ref, w3b_ref, w2b_ref):
    i = pl.program_id(0)

    @pl.when((i == 0) | (be_ref[i] != be_ref[jnp.maximum(i - 1, 0)]))
    def _():
        w1b_ref[...] = w1_ref[0].astype(BF16)
        w3b_ref[...] = w3_ref[0].astype(BF16)
        w2b_ref[...] = w2_ref[0].astype(BF16)

    @pl.when(i < nu_ref[0])
    def _():
        word = xd_ref[...]
        xa = pltpu.bitcast(lax.shift_left(word, jnp.uint32(16)), F32).astype(BF16)
        xb = pltpu.bitcast(word & jnp.uint32(HI_MASK), F32).astype(BF16)
        dh = xa.shape[1]
        a = _dot(xa, w1b_ref[0:dh]) + _dot(xb, w1b_ref[dh:2 * dh])
        c = _dot(xa, w3b_ref[0:dh]) + _dot(xb, w3b_ref[dh:2 * dh])
        y_ref[...] = _dot((_silu(a) * c).astype(BF16), w2b_ref[...])

    @pl.when(i >= nu_ref[0])
    def _():
        y_ref[...] = jnp.zeros_like(y_ref)


def _experts(blk_expert, n_used, xd, w1, w3, w2):
    n_rows, dh = xd.shape
    _, d, de = w1.shape
    n_blocks = n_rows // MOE_BLOCK
    return pl.pallas_call(
        _experts_kernel,
        grid_spec=pltpu.PrefetchScalarGridSpec(
            num_scalar_prefetch=2,
            grid=(n_blocks,),
            in_specs=[pl.BlockSpec((MOE_BLOCK, dh), lambda i, be, nu: (i, 0)),
                      pl.BlockSpec((1, d, de), lambda i, be, nu: (be[i], 0, 0)),
                      pl.BlockSpec((1, d, de), lambda i, be, nu: (be[i], 0, 0)),
                      pl.BlockSpec((1, de, d), lambda i, be, nu: (be[i], 0, 0))],
            out_specs=pl.BlockSpec((MOE_BLOCK, d), lambda i, be, nu: (i, 0)),
            scratch_shapes=[pltpu.VMEM((d, de), BF16), pltpu.VMEM((d, de), BF16), pltpu.VMEM((de, d), BF16)]),
        out_shape=jax.ShapeDtypeStruct((n_rows, d), F32),
        compiler_params=pltpu.CompilerParams(
            dimension_semantics=("arbitrary",), vmem_limit_bytes=VMEM_LIMIT),
        name="experts",
    )(blk_expert, n_used, xd, w1, w3, w2)


def _combine_kernel(dest_ref, dnext_ref, rw_ref, x1_ref, mod_ref, y_ref, o_ref, buf_ref, sems):
    i = pl.program_id(0)
    slot = i % 2

    def gather(idx_ref, slot_, start):
        def one(r, k):
            cp = pltpu.make_async_copy(y_ref.at[pl.ds(idx_ref[0, k, r], 1)],
                                       buf_ref.at[slot_, k, pl.ds(r, 1)], sems.at[slot_])
            if start:
                cp.start()
            else:
                cp.wait()

        _for_each_row(one)

    @pl.when(i == 0)
    def _():
        gather(dest_ref, slot, True)

    @pl.when(i + 1 < pl.num_programs(0))
    def _():
        gather(dnext_ref, 1 - slot, True)

    gather(dest_ref, slot, False)
    w = rw_ref[...]
    g2 = mod_ref[0][5:6]
    o_ref[...] = x1_ref[...] + g2 * (w[:, 0:1] * buf_ref[slot, 0] + w[:, 1:2] * buf_ref[slot, 1])


def _combine(dest_t, rw, x1, mod, y_disp, seq):
    t, d = x1.shape
    tiles_per_seq = seq // TOK_TILE
    n_tiles = t // TOK_TILE
    return pl.pallas_call(
        _combine_kernel,
        grid=(n_tiles,),
        in_specs=[pl.BlockSpec((1, 8, TOK_TILE), lambda i: (i, 0, 0), memory_space=pltpu.SMEM),
                  pl.BlockSpec((1, 8, TOK_TILE), lambda i: (jnp.minimum(i + 1, n_tiles - 1), 0, 0),
                               memory_space=pltpu.SMEM),
                  pl.BlockSpec((TOK_TILE, LANES), lambda i: (i, 0)),
                  pl.BlockSpec((TOK_TILE, d), lambda i: (i, 0)),
                  pl.BlockSpec((1, 6, d), lambda i: (i // tiles_per_seq, 0, 0)),
                  pl.BlockSpec(memory_space=pl.ANY)],
        out_specs=pl.BlockSpec((TOK_TILE, d), lambda i: (i, 0)),
        scratch_shapes=[pltpu.VMEM((2, 2, TOK_TILE, d), F32), pltpu.SemaphoreType.DMA((2,))],
        out_shape=jax.ShapeDtypeStruct((t, d), F32),
        compiler_params=pltpu.CompilerParams(
            dimension_semantics=("arbitrary",), vmem_limit_bytes=VMEM_LIMIT),
        name="combine",
    )(dest_t, dest_t, rw, x1, mod, y_disp)


def _moe(x1, h2p, ri, rw, cnt, mod, w1, w3, w2):
    b, s, d = x1.shape
    t = b * s
    a = 2 * t
    counts = cnt[0, ROUTE_E0:ROUTE_E0 + N_EXPERTS].astype(I32)
    padded = (counts + MOE_BLOCK - 1) // MOE_BLOCK * MOE_BLOCK
    pad_end = jnp.cumsum(padded)
    pad_start = (pad_end - padded).astype(I32)
    n_blocks = (a + N_EXPERTS * (MOE_BLOCK - 1)) // MOE_BLOCK + 1
    blk_start = jnp.arange(n_blocks, dtype=I32) * MOE_BLOCK
    blk_expert = jnp.minimum(
        jnp.sum((pad_end[None, :] <= blk_start[:, None]).astype(I32), axis=1), N_EXPERTS - 1)
    n_used = (pad_end[-1:] // MOE_BLOCK).astype(I32)
    dest_t = _dest_rows(pad_start, ri).reshape(t // TOK_TILE, 8, TOK_TILE)
    xd = _dispatch(dest_t, h2p.reshape(t, d // 2), n_blocks * MOE_BLOCK)
    y_disp = _experts(blk_expert, n_used, xd, w1, w3, w2)
    out = _combine(dest_t, rw.reshape(t, LANES), x1.reshape(t, d), mod, y_disp, s)
    return out.reshape(b, s, d)


def kernel(x, c, w_ada, b_ada, norm1_w, w_in, q_norm_w, k_norm_w, cmp_pos_k, cmp_pos_v, cmp_k_w1, cmp_k_w2,
           cmp_v_w1, cmp_v_w2, conv_w, out_norm_w, w_out, rel_bias, norm2_w, w_group, b_group, w_expert,
           b_expert, w1, w3, w2):
    t_hi, t_lo, t_c = _bias_tables(rel_bias)
    for l in range(w_ada.shape[0]):
        mod = _ada(c, w_ada[l], b_ada[l])
        q, kvc, kaug, vaug, kwin, vwaug, gates, yconv = _inproj(
            x, mod, norm1_w[l], w_in[l], q_norm_w[l], k_norm_w[l], conv_w[l], out_norm_w[l])
        vaug, vscale = _to_f8(vaug)
        vwaug, vwscale = _to_f8(vwaug)
        rows = _kvc_rows(kvc)
        k_c = _compress(rows, 0, cmp_k_w1[l], cmp_k_w2[l], cmp_pos_k[l], k_norm_w[l, 0], True)
        v_c = _compress(rows, 1, cmp_v_w1[l], cmp_v_w2[l], cmp_pos_v[l], k_norm_w[l, 0], False)
        o_cmp, pen = _cmpsel(q, k_c, v_c, t_c)
        mixa = _slcwin(q, pen, kaug, vaug, vscale, kwin, vwaug, vwscale, o_cmp, gates, t_hi, t_lo,
                       out_norm_w[l])
        x1, h2p, ri, rw, cnt = _outproj(mixa, yconv, w_out[l], x, mod, norm2_w[l],
                                        w_group[l], b_group[l], w_expert[l], b_expert[l])
        x = _moe(x1, h2p, ri, rw, cnt, mod, w1[l], w3[l], w2[l])
    return x
```

```python
import functools
import math

import numpy as np
import jax
import jax.numpy as jnp
from jax import lax
from jax.experimental import pallas as pl
from jax.experimental.pallas import tpu as pltpu

F32 = jnp.float32
BF16 = jnp.bfloat16
F8 = jnp.float8_e4m3fn
P_SCALE = 256.0
F8_TARGET = 240.0
I32 = jnp.int32

HEAD_DIM = 64
N_HEADS = 8
N_KV = 2
HPG = N_HEADS // N_KV
D_ATTN = N_HEADS * HEAD_DIM
D_CONV = 512
KV_DIM = N_KV * HEAD_DIM
CMP_LEN = 32
CMP_STRIDE = 16
CMP_HIDDEN = 256
SLC_LEN = 64
N_SELECT = 16
WINDOW = 512
N_BUCKETS = 32
MAX_DISTANCE = 128
N_GROUPS = 8
EXPERTS_PER_GROUP = 8
N_EXPERTS = N_GROUPS * EXPERTS_PER_GROUP
MOE_BLOCK = 512
FORCED_SCORE = 1e4
EPS = 1e-6

LANES = 128
QB = 512
ROWS_Q = HPG * QB
CMP_CHUNK = 256
FAR_GROUP = 8
PEN_BLOCKS = 128
HALF_KEYS = PEN_BLOCKS * SLC_LEN
NEG = -1e30
PICKED = -3e38
VMEM_LIMIT = 56 * 1024 * 1024


def _dot(a, b):
    return jnp.dot(a, b, preferred_element_type=F32)


def _dot_nt(a, b):
    return lax.dot_general(a, b, (((1,), (1,)), ((), ())), preferred_element_type=F32)


def _hi_lo(a):
    hi = a.astype(BF16)
    lo = (a - hi.astype(F32)).astype(BF16)
    return hi, lo


def _dot3(a, b_hi, b_lo):
    a_hi, a_lo = _hi_lo(a)
    return _dot(a_hi, b_hi) + (_dot(a_lo, b_hi) + _dot(a_hi, b_lo))


def _group_sumsq(v, gm):
    hi, lo = _hi_lo(v * v)
    return _dot(hi, gm) + _dot(lo, gm)


def _silu(v):
    return v * jax.nn.sigmoid(v)


def _ada_kernel(c_ref, w_ref, b_ref, o_ref):
    cond = _silu(c_ref[...])
    w_hi, w_lo = _hi_lo(w_ref[...])
    o_ref[...] = _dot3(cond, w_hi, w_lo) + b_ref[...]


def _ada(c, w_ada, b_ada):
    b, d = c.shape
    n = w_ada.shape[1]
    tn = n // 4
    c8 = jnp.pad(c, ((0, 8 - b), (0, 0)))
    out = pl.pallas_call(
        _ada_kernel,
        grid=(n // tn,),
        in_specs=[pl.BlockSpec((8, d), lambda i: (0, 0)),
                  pl.BlockSpec((d, tn), lambda i: (0, i)),
                  pl.BlockSpec((1, tn), lambda i: (0, i))],
        out_specs=pl.BlockSpec((8, tn), lambda i: (0, i)),
        out_shape=jax.ShapeDtypeStruct((8, n), F32),
        compiler_params=pltpu.CompilerParams(vmem_limit_bytes=VMEM_LIMIT),
        name="ada",
    )(c8, w_ada, b_ada.reshape(1, n))
    return out[:b].reshape(b, 6, d)


def _inproj_kernel(x_ref, xh_ref, mod_ref, n1w_ref, wq_ref, wkv_ref, wg_ref, wbcu_ref,
                   qnw_ref, knw_ref, convw_ref, onw_ref, gm_ref,
                   q_ref, kvc_ref, kaug_ref, vaug_ref, kwin_ref, vwaug_ref, gates_ref, yconv_ref,
                   ext_ref, *, tm):
    i = pl.program_id(1)
    mod = mod_ref[0]
    sh1 = mod[0:1]
    sc1 = mod[1:2]

    def norm_mod(v):
        ms = jnp.mean(v * v, axis=-1, keepdims=True)
        y = (v * lax.rsqrt(ms + EPS)) * n1w_ref[...]
        return (y * (1.0 + sc1) + sh1).astype(BF16)

    h = norm_mod(x_ref[0])
    gm = gm_ref[...]
    gm_kv = gm_ref[0:KV_DIM, 0:KV_DIM]

    q = _dot(h, wq_ref[...])
    qn = q * lax.rsqrt(_group_sumsq(q, gm) * (1.0 / HEAD_DIM) + EPS) * qnw_ref[...]
    qn = (qn * (HEAD_DIM ** -0.5)).astype(BF16)
    for hd in range(N_HEADS):
        q_ref[0, hd] = qn[:, hd * HEAD_DIM:(hd + 1) * HEAD_DIM]

    kv = _dot(h, wkv_ref[...])
    kvc_ref[0] = kv[:, 0:2 * KV_DIM]
    ks = kv[:, 2 * KV_DIM:3 * KV_DIM]
    vs = kv[:, 3 * KV_DIM:4 * KV_DIM]
    kw = kv[:, 4 * KV_DIM:5 * KV_DIM]
    vw = kv[:, 5 * KV_DIM:6 * KV_DIM]
    ksn = ks * lax.rsqrt(_group_sumsq(ks, gm_kv) * (1.0 / HEAD_DIM) + EPS) * knw_ref[0:1]
    kwn = kw * lax.rsqrt(_group_sumsq(kw, gm_kv) * (1.0 / HEAD_DIM) + EPS) * knw_ref[1:2]

    lane = lax.broadcasted_iota(I32, (tm, LANES), 1)
    row = lax.broadcasted_iota(I32, (tm, LANES), 0)
    blk = ((i * tm + row) // SLC_LEN) % PEN_BLOCKS
    onehot = jnp.where(lane == blk, 1.0, 0.0).astype(BF16)
    ones_col = jnp.where(lax.broadcasted_iota(I32, (tm, HEAD_DIM), 1) == 0, 1.0, 0.0).astype(BF16)
    zeros64 = jnp.zeros((tm, HEAD_DIM), BF16)
    for g in range(N_KV):
        sl = slice(g * HEAD_DIM, (g + 1) * HEAD_DIM)
        kaug_ref[0, g, :, 0:LANES] = onehot
        kaug_ref[0, g, :, LANES:LANES + HEAD_DIM] = ksn[:, sl].astype(BF16)
        kaug_ref[0, g, :, LANES + HEAD_DIM:2 * LANES] = zeros64
        vaug_ref[0, g, :, 0:HEAD_DIM] = vs[:, sl].astype(BF16)
        vaug_ref[0, g, :, HEAD_DIM:LANES] = ones_col
        kwin_ref[0, g] = kwn[:, sl].astype(BF16)
        vwaug_ref[0, g, :, 0:HEAD_DIM] = vw[:, sl].astype(BF16)
        vwaug_ref[0, g, :, HEAD_DIM:LANES] = ones_col

    gates_ref[0] = jax.nn.sigmoid(_dot(h, wg_ref[...]))

    bcu = _dot(h, wbcu_ref[...])
    bg = bcu[:, 0:D_CONV]
    cu = bcu[:, D_CONV:2 * D_CONV] * bcu[:, 2 * D_CONV:3 * D_CONV]
    hh = norm_mod(xh_ref[0])
    cuh = _dot(hh, wbcu_ref[:, D_CONV:2 * D_CONV]) * _dot(hh, wbcu_ref[:, 2 * D_CONV:3 * D_CONV])
    cuh = jnp.where(i > 0, cuh, 0.0)
    ext_ref[0:8] = cuh
    ext_ref[8:8 + tm] = cu
    cw = convw_ref[...]
    y = bg * (cw[0:1] * ext_ref[6:6 + tm] + cw[1:2] * ext_ref[7:7 + tm] + cw[2:3] * cu)
    yn = y * lax.rsqrt(_group_sumsq(y, gm) * (1.0 / HEAD_DIM) + EPS) * onw_ref[...]
    yconv_ref[0] = yn.astype(BF16)


def _inproj(x, mod, norm1_w, w_in, q_norm_w, k_norm_w, conv_w, out_norm_w, tm=512):
    b, s, d = x.shape
    o_kv = D_ATTN
    o_g = o_kv + 6 * KV_DIM
    o_b = o_g + 3 * N_HEADS
    wq = w_in[:, 0:o_kv].astype(BF16)
    wkv = w_in[:, o_kv:o_g].astype(BF16)
    wg = jnp.pad(w_in[:, o_g:o_b].reshape(d, N_KV, 3 * HPG),
                 ((0, 0), (0, 0), (0, LANES - 3 * HPG))).reshape(d, N_KV * LANES).astype(BF16)
    wbcu = w_in[:, o_b:].astype(BF16)
    qnw = jnp.tile(q_norm_w, N_HEADS).reshape(1, D_ATTN)
    knw = jnp.stack([jnp.tile(k_norm_w[1], N_KV), jnp.tile(k_norm_w[2], N_KV)])
    onw = out_norm_w[D_ATTN:].reshape(1, D_CONV)
    gidx = np.arange(D_ATTN) // HEAD_DIM
    gm = jnp.asarray((gidx[:, None] == gidx[None, :]).astype(np.float32), dtype=BF16)

    full = lambda shape: pl.BlockSpec(shape, lambda bi, i: (0,) * len(shape))
    outs = pl.pallas_call(
        functools.partial(_inproj_kernel, tm=tm),
        grid=(b, s // tm),
        in_specs=[pl.BlockSpec((1, tm, d), lambda bi, i: (bi, i, 0)),
                  pl.BlockSpec((1, 8, d), lambda bi, i: (bi, jnp.maximum(i * (tm // 8) - 1, 0), 0)),
                  pl.BlockSpec((1, 6, d), lambda bi, i: (bi, 0, 0)),
                  full((1, d)), full(wq.shape), full(wkv.shape), full(wg.shape), full(wbcu.shape),
                  full(qnw.shape), full(knw.shape), full(conv_w.shape), full(onw.shape), full(gm.shape)],
        out_specs=[pl.BlockSpec((1, N_HEADS, tm, HEAD_DIM), lambda bi, i: (bi, 0, i, 0)),
                   pl.BlockSpec((1, tm, 2 * KV_DIM), lambda bi, i: (bi, i, 0)),
                   pl.BlockSpec((1, N_KV, tm, 2 * LANES), lambda bi, i: (bi, 0, i, 0)),
                   pl.BlockSpec((1, N_KV, tm, LANES), lambda bi, i: (bi, 0, i, 0)),
                   pl.BlockSpec((1, N_KV, tm, HEAD_DIM), lambda bi, i: (bi, 0, i, 0)),
                   pl.BlockSpec((1, N_KV, tm, LANES), lambda bi, i: (bi, 0, i, 0)),
                   pl.BlockSpec((1, tm, N_KV * LANES), lambda bi, i: (bi, i, 0)),
                   pl.BlockSpec((1, tm, D_CONV), lambda bi, i: (bi, i, 0))],
        out_shape=[jax.ShapeDtypeStruct((b, N_HEADS, s, HEAD_DIM), BF16),
                   jax.ShapeDtypeStruct((b, s, 2 * KV_DIM), F32),
                   jax.ShapeDtypeStruct((b, N_KV, s, 2 * LANES), BF16),
                   jax.ShapeDtypeStruct((b, N_KV, s, LANES), BF16),
                   jax.ShapeDtypeStruct((b, N_KV, s, HEAD_DIM), BF16),
                   jax.ShapeDtypeStruct((b, N_KV, s, LANES), BF16),
                   jax.ShapeDtypeStruct((b, s, N_KV * LANES), F32),
                   jax.ShapeDtypeStruct((b, s, D_CONV), BF16)],
        scratch_shapes=[pltpu.VMEM((tm + 8, D_CONV), F32)],
        compiler_params=pltpu.CompilerParams(
            dimension_semantics=("parallel", "arbitrary"), vmem_limit_bytes=VMEM_LIMIT),
        name="inproj",
    )(x, x, mod, norm1_w.reshape(1, d), wq, wkv, wg, wbcu, qnw, knw, conv_w, onw, gm)
    return outs


def _compress_kernel(r_ref, w1_ref, w2_ref, pos_ref, nw_ref, o_ref, *, norm):
    r = r_ref[0, 0, 0]
    nr = r.shape[0]
    half = (CMP_LEN // 2) * HEAD_DIM
    w1a = w1_ref[0:half, :]
    w1b = w1_ref[half:2 * half, :]
    pos = pos_ref[...]
    cst = (_dot(jnp.broadcast_to(pos[0:1], (8, half)).astype(BF16), w1a)
           + _dot(jnp.broadcast_to(pos[1:2], (8, half)).astype(BF16), w1b))[0:1]
    upper = pltpu.roll(_dot(r, w1b), nr - 1, 0)
    hid = _dot(r, w1a) + upper + cst
    out = _dot(_silu(hid).astype(BF16), w2_ref[...])
    if norm:
        ms = jnp.mean(out * out, axis=-1, keepdims=True)
        out = out * lax.rsqrt(ms + EPS) * nw_ref[...]
        o_ref[0, 0] = out.astype(BF16)
    else:
        o_ref[0, 0, :, 0:HEAD_DIM] = out.astype(BF16)
        o_ref[0, 0, :, HEAD_DIM:LANES] = jnp.where(
            lax.broadcasted_iota(I32, (nr, HEAD_DIM), 1) == 0, 1.0, 0.0).astype(BF16)


def _compress(rows, kv_index, w1, w2, pos_emb, norm_w, norm):
    b, _, g, nr, width = rows.shape
    out_w = HEAD_DIM if norm else LANES
    return pl.pallas_call(
        functools.partial(_compress_kernel, norm=norm),
        grid=(b, g),
        in_specs=[pl.BlockSpec((1, 1, 1, nr, width), lambda bi, gi: (bi, kv_index, gi, 0, 0)),
                  pl.BlockSpec(w1.shape, lambda bi, gi: (0, 0)),
                  pl.BlockSpec(w2.shape, lambda bi, gi: (0, 0)),
                  pl.BlockSpec((2, width), lambda bi, gi: (0, 0)),
                  pl.BlockSpec((1, HEAD_DIM), lambda bi, gi: (0, 0))],
        out_specs=pl.BlockSpec((1, 1, nr, out_w), lambda bi, gi: (bi, gi, 0, 0)),
        out_shape=jax.ShapeDtypeStruct((b, g, nr, out_w), BF16),
        compiler_params=pltpu.CompilerParams(vmem_limit_bytes=VMEM_LIMIT),
        name="compress_k" if norm else "compress_v",
    )(rows, w1.astype(BF16), w2.astype(BF16), pos_emb.reshape(2, width), norm_w.reshape(1, HEAD_DIM))


def _kvc_rows(kvc):
    b, s, _ = kvc.shape
    r = kvc.reshape(b, s, 2, N_KV, HEAD_DIM).transpose(0, 2, 3, 1, 4)
    return r.reshape(b, 2, N_KV, s // CMP_STRIDE, CMP_STRIDE * HEAD_DIM).astype(BF16)


def _bucket_np(dist):
    n = np.maximum(dist, 0)
    max_exact = N_BUCKETS // 2
    nf = np.maximum(n, 1).astype(np.float32)
    large = max_exact + (np.log(nf / np.float32(max_exact)) / np.float32(math.log(MAX_DISTANCE / max_exact))
                         * np.float32(N_BUCKETS - max_exact)).astype(np.int32)
    large = np.minimum(large, N_BUCKETS - 1)
    return np.where(n < max_exact, n, large)


_BUCKET_START = [int(np.argmax(_bucket_np(np.arange(MAX_DISTANCE + 1)) >= k)) for k in range(N_BUCKETS)]
CMP_DIST0 = CMP_STRIDE * CMP_CHUNK - (CMP_LEN - 1) - QB


def _bias_tables_kernel(rbt_ref, thi_ref, tlo_ref, tc_ref):
    head = pl.program_id(0) * HPG + pl.program_id(1)
    far = rbt_ref[head, N_BUCKETS - 1]

    def table(dist):
        v = jnp.zeros(dist.shape, F32)
        for k in range(N_BUCKETS - 2, -1, -1):
            v = jnp.where(dist < _BUCKET_START[k + 1], rbt_ref[head, k] - far, v)
        return jnp.where(dist >= 0, v, NEG)

    def dist(shape, row_mul, col_mul, off):
        return (lax.broadcasted_iota(I32, shape, 0) * row_mul
                + lax.broadcasted_iota(I32, shape, 1) * col_mul + off)

    thi_ref[0] = table(dist((QB, QB), 1, -1, 0))
    tlo_ref[0] = table(dist((QB, LANES), 1, -1, LANES))
    tc_ref[0] = table(dist((QB, CMP_CHUNK), 1, -CMP_STRIDE, CMP_DIST0))


def _bias_tables(rel_bias):
    return pl.pallas_call(
        _bias_tables_kernel,
        grid=(N_KV, HPG),
        in_specs=[pl.BlockSpec(memory_space=pltpu.SMEM)],
        out_specs=[pl.BlockSpec((1, QB, QB), lambda g, h: (g, h, 0)),
                   pl.BlockSpec((1, QB, LANES), lambda g, h: (g, h, 0)),
                   pl.BlockSpec((1, QB, CMP_CHUNK), lambda g, h: (g, h, 0))],
        out_shape=[jax.ShapeDtypeStruct((N_KV, ROWS_Q, QB), F32),
                   jax.ShapeDtypeStruct((N_KV, ROWS_Q, LANES), F32),
                   jax.ShapeDtypeStruct((N_KV, ROWS_Q, CMP_CHUNK), F32)],
        name="bias_tables",
    )(rel_bias.T)


def _cmpsel_kernel(q_ref, kc_ref, vc_ref, tab_ref, ovt_ref, ocmp_ref, pen_ref,
                   m_ref, acc_ref, imp_ref, score_ref, invt_ref, *, nsbp, n_sel):
    j = pl.program_id(2)
    q4 = q_ref[0].reshape(ROWS_Q, HEAD_DIM)
    nq16 = QB // CMP_STRIDE
    n_chunks = (nq16 * (j + 1) - 1) // CMP_CHUNK + 1
    ni = lax.broadcasted_iota(I32, (1, CMP_CHUNK), 1)

    def chunk_start(m):
        return pl.multiple_of(nq16 * (j + 1) - CMP_CHUNK * m, nq16)

    def logits(m):
        n0p = chunk_start(m)
        s = _dot_nt(q4, kc_ref[0, 0, pl.ds(n0p, CMP_CHUNK), :])
        return jnp.where(ni + n0p >= CMP_CHUNK, s, NEG)

    s0 = logits(0) + tab_ref[0]
    m_ref[...] = s0

    def stats(m, carry):
        m_ref[...] = jnp.maximum(m_ref[...], logits(m))
        return carry

    lax.fori_loop(1, n_chunks, stats, 0)
    m_fin = jnp.max(m_ref[...], axis=1, keepdims=True)
    m_ref[...] = jnp.broadcast_to(m_fin, (ROWS_Q, CMP_CHUNK))
    acc_ref[...] = jnp.zeros_like(acc_ref)
    imp_ref[...] = jnp.zeros_like(imp_ref)

    def accumulate(m, s):
        n0p = chunk_start(m)
        p = jnp.exp(s - m_ref[...]).astype(BF16)
        acc_ref[...] += _dot(p, vc_ref[0, 0, pl.ds(n0p, CMP_CHUNK), :])
        r0 = pl.multiple_of(n0p // 4, 8)
        for hd in range(HPG):
            imp_ref[hd, pl.ds(r0, OV_ROWS), :] += _dot_nt(ovt_ref[...], p[hd * QB:(hd + 1) * QB])

    accumulate(0, s0)

    def second(m, carry):
        accumulate(m, logits(m))
        return carry

    lax.fori_loop(1, n_chunks, second, 0)
    acc = acc_ref[...]
    inv = jnp.where(m_fin > 0.5 * NEG, 1.0 / acc[:, HEAD_DIM:HEAD_DIM + 1], 0.0)
    ocmp_ref[0, 0] = (acc[:, 0:HEAD_DIM] * inv).reshape(HPG, QB, HEAD_DIM)

    invt_ref[...] = jnp.broadcast_to(inv, (ROWS_Q, LANES)).T
    imp = jnp.zeros((nsbp, QB), F32)
    for hd in range(HPG):
        imp = imp + imp_ref[hd, IMP_PAD:IMP_PAD + nsbp, :] * invt_ref[0:1, hd * QB:(hd + 1) * QB]
    srow = lax.broadcasted_iota(I32, (nsbp, QB), 0)
    qcol = lax.broadcasted_iota(I32, (nsbp, QB), 1)
    cur = (QB // SLC_LEN) * j + qcol // SLC_LEN
    forced = (srow == 0) | (srow == cur) | (srow == cur - 1)
    score_ref[...] = jnp.where(forced, -1.0, jnp.where(srow <= cur, imp, -1.0))

    quarter = nsbp // 4
    case = ((QB // SLC_LEN) * (j + 1) - 1) // quarter
    for k in range(4):
        rows = quarter * (k + 1)

        @pl.when(case == k)
        def _(rows=rows):
            srow_k = lax.broadcasted_iota(I32, (rows, QB), 0)

            def pick(r, carry):
                sc = score_ref[0:rows]
                mx = jnp.max(sc, axis=0, keepdims=True)
                idx = jnp.min(jnp.where(sc == mx, srow_k, nsbp), axis=0, keepdims=True)
                score_ref[0:rows] = jnp.where(srow_k == idx, PICKED, sc)
                return carry

            lax.fori_loop(0, max(n_sel - 3, 0), pick, 0)
    pen_other = jnp.where(srow <= cur, jnp.where(score_ref[...] == PICKED, 0.0, NEG), NEG)
    pen_t = jnp.where(forced, 0.0, pen_other).T
    for hf in range(nsbp // PEN_BLOCKS):
        pen_ref[0, 0, hf] = pen_t[:, hf * PEN_BLOCKS:(hf + 1) * PEN_BLOCKS].astype(BF16)


IMP_PAD = CMP_CHUNK // 4
OV_ROWS = IMP_PAD + 8


def _overlap_t():
    n = np.arange(CMP_CHUNK)
    s = np.arange(OV_ROWS)
    first = (n * CMP_STRIDE) // SLC_LEN
    last = (n * CMP_STRIDE + CMP_LEN - 1) // SLC_LEN
    ov = (s[:, None] >= first[None, :]) & (s[:, None] <= last[None, :])
    return jnp.asarray(ov.astype(np.float32), dtype=BF16)


def _cmpsel(q, k_c, v_c, tab):
    b, _, s, _ = q.shape
    nr = k_c.shape[2]
    nq = s // QB
    nsb = s // SLC_LEN
    nh = -(-nsb // PEN_BLOCKS)
    nsbp = nh * PEN_BLOCKS
    n_sel = min(N_SELECT, nsb)
    pad = ((0, 0), (0, 0), (CMP_CHUNK, 0), (0, 0))
    kc_p = jnp.pad(k_c, pad)
    vc_p = jnp.pad(v_c, pad)
    ovt = _overlap_t()
    return pl.pallas_call(
        functools.partial(_cmpsel_kernel, nsbp=nsbp, n_sel=n_sel),
        grid=(b, N_KV, nq),
        in_specs=[pl.BlockSpec((1, HPG, QB, HEAD_DIM), lambda bi, g, j: (bi, g, j, 0)),
                  pl.BlockSpec((1, 1, nr + CMP_CHUNK, HEAD_DIM), lambda bi, g, j: (bi, g, 0, 0)),
                  pl.BlockSpec((1, 1, nr + CMP_CHUNK, LANES), lambda bi, g, j: (bi, g, 0, 0)),
                  pl.BlockSpec((1, ROWS_Q, CMP_CHUNK), lambda bi, g, j: (g, 0, 0)),
                  pl.BlockSpec(ovt.shape, lambda bi, g, j: (0, 0))],
        out_specs=[pl.BlockSpec((1, 1, HPG, QB, HEAD_DIM), lambda bi, g, j: (bi, g, 0, j, 0)),
                   pl.BlockSpec((1, 1, nh, QB, PEN_BLOCKS), lambda bi, g, j: (bi, g, 0, j, 0))],
        out_shape=[jax.ShapeDtypeStruct((b, N_KV, HPG, s, HEAD_DIM), F32),
                   jax.ShapeDtypeStruct((b, N_KV, nh, s, PEN_BLOCKS), BF16)],
        scratch_shapes=[pltpu.VMEM((ROWS_Q, CMP_CHUNK), F32),
                        pltpu.VMEM((ROWS_Q, LANES), F32),
                        pltpu.VMEM((HPG, IMP_PAD + nsbp + 8, QB), F32),
                        pltpu.VMEM((nsbp, QB), F32),
                        pltpu.VMEM((LANES, ROWS_Q), F32)],
        compiler_params=pltpu.CompilerParams(
            dimension_semantics=("parallel", "parallel", "arbitrary"), vmem_limit_bytes=VMEM_LIMIT),
        name="cmpsel",
    )(q, kc_p, vc_p, tab, ovt)


def _to_f8_kernel(v_ref, o_ref, sc_ref):
    v = v_ref[0, 0].astype(F32)
    lane = lax.broadcasted_iota(I32, (1, LANES), 1)
    cmax = jnp.max(jnp.abs(v), axis=0, keepdims=True)
    scale = jnp.where(lane < HEAD_DIM, jnp.maximum(cmax, 1e-30) * (1.0 / F8_TARGET), 1.0)
    o_ref[0, 0] = (v / scale).astype(F8)
    sc_ref[0, 0] = jnp.broadcast_to(scale, (8, LANES))


def _to_f8(vaug):
    b, g, s, w = vaug.shape
    return pl.pallas_call(
        _to_f8_kernel,
        grid=(b, g),
        in_specs=[pl.BlockSpec((1, 1, s, w), lambda bi, gi: (bi, gi, 0, 0))],
        out_specs=[pl.BlockSpec((1, 1, s, w), lambda bi, gi: (bi, gi, 0, 0)),
                   pl.BlockSpec((1, 1, 8, w), lambda bi, gi: (bi, gi, 0, 0))],
        out_shape=[jax.ShapeDtypeStruct((b, g, s, w), F8), jax.ShapeDtypeStruct((b, g, 8, w), F32)],
        compiler_params=pltpu.CompilerParams(vmem_limit_bytes=VMEM_LIMIT),
        name="to_f8",
    )(vaug)


def _slcwin_kernel(q_ref, pen_ref, kaug_ref, vaug_ref, vsc_ref, kwp_ref, kwc_ref, vwp_ref, vwc_ref,
                   vwsc_ref, thi_ref, tlo_ref, ocmp_ref, gates_ref, onw_ref, o_ref,
                   lhs_ref, m_ref, acc_ref):
    j = pl.program_id(2)
    q4 = q_ref[0].reshape(ROWS_Q, HEAD_DIM)
    near = QB - LANES

    lhs_ref[:, LANES:LANES + HEAD_DIM] = q4
    lhs_ref[:, LANES + HEAD_DIM:2 * LANES] = jnp.zeros((ROWS_Q, HEAD_DIM), BF16)

    def set_pen(half):
        p = pen_ref[0, 0, half]
        for hd in range(HPG):
            lhs_ref[hd * QB:(hd + 1) * QB, 0:LANES] = p

    def reset():
        m_ref[...] = jnp.full_like(m_ref, NEG)
        acc_ref[...] = jnp.zeros_like(acc_ref)

    def update(s, v):
        sb = s.astype(BF16)
        m_prev = m_ref[...]
        m_new = jnp.maximum(m_prev, jnp.max(sb, axis=1, keepdims=True).astype(F32))
        x = sb - jnp.concatenate([m_new.astype(BF16)] * (s.shape[1] // LANES), axis=1)
        p = (jnp.exp(x) * P_SCALE).astype(F8)
        acc_ref[...] = acc_ref[...] * jnp.exp(m_prev - m_new) + _dot(p, v)
        m_ref[...] = m_new

    def add_near_bias(s):
        return jnp.concatenate([s[:, :near], s[:, near:] + tlo_ref[0]], axis=1)

    def result(scale_ref):
        acc = acc_ref[...]
        return acc[:, 0:HEAD_DIM] * scale_ref[0, 0, 0:1, 0:HEAD_DIM] / acc[:, HEAD_DIM:HEAD_DIM + 1]

    def slc_chunk(c, bias):
        k0 = pl.multiple_of(c * QB, QB)
        s = _dot_nt(lhs_ref[...], kaug_ref[0, 0, pl.ds(k0, QB), :])
        update(bias(s), vaug_ref[0, 0, pl.ds(k0, QB), :])

    chunks_per_half = HALF_KEYS // QB
    assert chunks_per_half % FAR_GROUP == 0
    reset()
    half_hi = j // chunks_per_half
    set_pen(half_hi)
    slc_chunk(j, lambda s: s + thi_ref[0])

    @pl.when(j > 0)
    def _():
        half_lo = (j - 1) // chunks_per_half

        @pl.when(half_lo != half_hi)
        def _():
            set_pen(half_lo)

        slc_chunk(j - 1, add_near_bias)

    n_far = jnp.maximum(j - 1, 0)

    def enter_half(c):
        @pl.when(c % chunks_per_half == 0)
        def _():
            set_pen(c // chunks_per_half)

    def far_group(i4, carry):
        c = FAR_GROUP * i4
        enter_half(c)
        for u in range(FAR_GROUP):
            slc_chunk(c + u, lambda s: s)
        return carry

    n_groups = n_far // FAR_GROUP
    lax.fori_loop(0, n_groups, far_group, 0)

    def far_single(c, carry):
        enter_half(c)
        slc_chunk(c, lambda s: s)
        return carry

    lax.fori_loop(n_groups * FAR_GROUP, n_far, far_single, 0)

    o_slc = result(vsc_ref)

    reset()
    update(_dot_nt(q4, kwc_ref[0, 0]) + thi_ref[0], vwc_ref[0, 0])

    @pl.when(j > 0)
    def _():
        s = _dot_nt(q4, kwp_ref[0, 0])
        qi = lax.broadcasted_iota(I32, (ROWS_Q, QB), 0) & (QB - 1)
        kk = lax.broadcasted_iota(I32, (ROWS_Q, QB), 1)
        s = jnp.where(kk > qi, s, NEG)
        update(add_near_bias(s), vwp_ref[0, 0])

    o_win = result(vwsc_ref)

    o_cmp = ocmp_ref[0, 0].reshape(ROWS_Q, HEAD_DIM)
    gt = gates_ref[0]
    outs = []
    for hd in range(HPG):
        rs = slice(hd * QB, (hd + 1) * QB)
        o = (gt[:, 3 * hd:3 * hd + 1] * o_cmp[rs] + gt[:, 3 * hd + 1:3 * hd + 2] * o_slc[rs]
             + gt[:, 3 * hd + 2:3 * hd + 3] * o_win[rs])
        ms = jnp.mean(o * o, axis=-1, keepdims=True)
        outs.append((o * lax.rsqrt(ms + EPS) * onw_ref[0, hd:hd + 1, :]).astype(BF16))
    o_ref[0] = jnp.concatenate(outs, axis=1)


def _slcwin(q, pen, kaug, vaug, vscale, kwin, vwaug, vwscale, o_cmp, gates, t_hi, t_lo, out_norm_w):
    assert QB == WINDOW
    b, _, s, _ = q.shape
    nq = s // QB
    nh = pen.shape[2]
    onw = out_norm_w[:D_ATTN].reshape(N_KV, HPG, HEAD_DIM)
    prev = lambda bi, g, j: (bi, g, jnp.maximum(j - 1, 0), 0)
    curr = lambda bi, g, j: (bi, g, j, 0)
    per_group = lambda bi, g, j: (bi, g, 0, 0)
    return pl.pallas_call(
        _slcwin_kernel,
        grid=(b, N_KV, nq),
        in_specs=[pl.BlockSpec((1, HPG, QB, HEAD_DIM), lambda bi, g, j: (bi, g, j, 0)),
                  pl.BlockSpec((1, 1, nh, QB, PEN_BLOCKS), lambda bi, g, j: (bi, g, 0, j, 0)),
                  pl.BlockSpec((1, 1, s, 2 * LANES), lambda bi, g, j: (bi, g, 0, 0)),
                  pl.BlockSpec((1, 1, s, LANES), lambda bi, g, j: (bi, g, 0, 0)),
                  pl.BlockSpec((1, 1, 8, LANES), per_group),
                  pl.BlockSpec((1, 1, QB, HEAD_DIM), prev),
                  pl.BlockSpec((1, 1, QB, HEAD_DIM), curr),
                  pl.BlockSpec((1, 1, QB, LANES), prev),
                  pl.BlockSpec((1, 1, QB, LANES), curr),
                  pl.BlockSpec((1, 1, 8, LANES), per_group),
                  pl.BlockSpec((1, ROWS_Q, QB), lambda bi, g, j: (g, 0, 0)),
                  pl.BlockSpec((1, ROWS_Q, LANES), lambda bi, g, j: (g, 0, 0)),
                  pl.BlockSpec((1, 1, HPG, QB, HEAD_DIM), lambda bi, g, j: (bi, g, 0, j, 0)),
                  pl.BlockSpec((1, QB, LANES), lambda bi, g, j: (bi, j, g)),
                  pl.BlockSpec((1, HPG, HEAD_DIM), lambda bi, g, j: (g, 0, 0))],
        out_specs=pl.BlockSpec((1, QB, HPG * HEAD_DIM), lambda bi, g, j: (bi, j, g)),
        out_shape=jax.ShapeDtypeStruct((b, s, D_ATTN), BF16),
        scratch_shapes=[pltpu.VMEM((ROWS_Q, 2 * LANES), BF16),
                        pltpu.VMEM((ROWS_Q, LANES), F32),
                        pltpu.VMEM((ROWS_Q, LANES), F32)],
        compiler_params=pltpu.CompilerParams(
            dimension_semantics=("parallel", "parallel", "arbitrary"), vmem_limit_bytes=VMEM_LIMIT),
        name="slcwin",
    )(q, pen, kaug, vaug, vscale, kwin, kwin, vwaug, vwaug, vwscale, t_hi, t_lo, o_cmp, gates, onw)


ROUTE_E0 = N_GROUPS
U32 = jnp.uint32
HI_MASK = 0xFFFF0000


def _outproj_kernel(mixa_ref, yc_ref, wo_ref, x_ref, mod_ref, n2w_ref, wrh_ref, wrl_ref, br_ref, tri_ref,
                    x1_ref, h2p_ref, ri_ref, rw_ref, cnt_ref, carry_ref, rect_ref, *, tm):
    @pl.when((pl.program_id(0) == 0) & (pl.program_id(1) == 0))
    def _():
        carry_ref[...] = jnp.zeros_like(carry_ref)

    mod = mod_ref[0]
    g1 = mod[2:3]
    sh2 = mod[3:4]
    sc2 = mod[4:5]
    d_half = wo_ref.shape[0] // 2
    mix = _dot(mixa_ref[0], wo_ref[0:d_half]) + _dot(yc_ref[0], wo_ref[d_half:2 * d_half])
    x1 = x_ref[0] + g1 * mix
    x1_ref[0] = x1
    ms = jnp.mean(x1 * x1, axis=-1, keepdims=True)
    h2 = ((x1 * lax.rsqrt(ms + EPS)) * n2w_ref[...]) * (1.0 + sc2) + sh2

    dh = h2.shape[1] // 2
    lo_w = pltpu.bitcast(h2[:, :dh].astype(BF16).astype(F32), U32)
    hi_w = pltpu.bitcast(h2[:, dh:].astype(BF16).astype(F32), U32)
    h2p_ref[0] = lax.shift_right_logical(lo_w, jnp.uint32(16)) | (hi_w & jnp.uint32(HI_MASK))

    logits = _dot3(h2, wrh_ref[...], wrl_ref[...]) + br_ref[...]
    lane = lax.broadcasted_iota(I32, (tm, LANES), 1)
    glog = jnp.where(lane < N_GROUPS, logits, NEG)
    gmax = jnp.max(glog, axis=1, keepdims=True)
    g_p = 1.0 / jnp.sum(jnp.exp(glog - gmax), axis=1, keepdims=True)
    gidx = jnp.min(jnp.where(glog == gmax, lane, LANES), axis=1, keepdims=True)
    e_lo = ROUTE_E0 + EXPERTS_PER_GROUP * gidx
    el = jnp.where(lane >= e_lo, jnp.where(lane < e_lo + EXPERTS_PER_GROUP, logits, NEG), NEG)
    m1 = jnp.max(el, axis=1, keepdims=True)
    i1 = jnp.min(jnp.where(el == m1, lane, LANES), axis=1, keepdims=True)
    el2 = jnp.where(lane == i1, NEG, el)
    m2 = jnp.max(el2, axis=1, keepdims=True)
    i2 = jnp.min(jnp.where(el2 == m2, lane, LANES), axis=1, keepdims=True)
    e2 = jnp.exp(m2 - m1)
    w1 = g_p / (1.0 + e2)
    w2 = g_p * e2 / (1.0 + e2)

    oh1 = lane == i1
    oh2 = lane == i2
    both = jnp.where(oh1, 1.0, jnp.where(oh2, 1.0, 0.0))
    base = carry_ref[0:1] + _dot(tri_ref[...], both.astype(BF16))
    r1 = jnp.sum(jnp.where(oh1, base, 0.0), axis=1, keepdims=True)
    r2 = jnp.sum(jnp.where(oh2, base, 0.0), axis=1, keepdims=True)
    carry_ref[...] = carry_ref[...] + jnp.sum(both, axis=0, keepdims=True)
    cnt_ref[...] = carry_ref[...]

    rec = jnp.where(lane == 0, (i1 - ROUTE_E0).astype(F32),
                    jnp.where(lane == 1, (i2 - ROUTE_E0).astype(F32),
                              jnp.where(lane == 2, r1, jnp.where(lane == 3, r2, 0.0))))
    rect_ref[...] = rec.T
    ri_ref[0, 0] = rect_ref[0:8, :].astype(I32)
    rw_ref[0] = jnp.where(lane == 0, w1, jnp.where(lane == 1, w2, 0.0))


def _outproj(mixa, yconv, w_out, x, mod, norm2_w, w_group, b_group, w_expert, b_expert):
    b, s, d = x.shape
    tm = TOK_TILE
    wr = jnp.pad(jnp.concatenate([w_group, w_expert], axis=1), ((0, 0), (0, LANES - N_GROUPS - N_EXPERTS)))
    wr_hi = wr.astype(BF16)
    wr_lo = (wr - wr_hi.astype(F32)).astype(BF16)
    br = jnp.pad(jnp.concatenate([b_group, b_expert]), (0, LANES - N_GROUPS - N_EXPERTS)).reshape(1, LANES)
    tri = jnp.asarray(np.tril(np.ones((tm, tm), np.float32), -1), dtype=BF16)
    full = lambda shape: pl.BlockSpec(shape, lambda bi, i: (0,) * len(shape))
    row = lambda w: pl.BlockSpec((1, tm, w), lambda bi, i: (bi, i, 0))
    return pl.pallas_call(
        functools.partial(_outproj_kernel, tm=tm),
        grid=(b, s // tm),
        in_specs=[row(D_ATTN), row(D_CONV), full((d, d)), row(d),
                  pl.BlockSpec((1, 6, d), lambda bi, i: (bi, 0, 0)),
                  full((1, d)), full((d, LANES)), full((d, LANES)), full((1, LANES)), full((tm, tm))],
        out_specs=[row(d), row(d // 2), pl.BlockSpec((1, 1, 8, tm), lambda bi, i: (bi, i, 0, 0)),
                   row(LANES), full((8, LANES))],
        out_shape=[jax.ShapeDtypeStruct((b, s, d), F32),
                   jax.ShapeDtypeStruct((b, s, d // 2), U32),
                   jax.ShapeDtypeStruct((b, s // tm, 8, tm), I32),
                   jax.ShapeDtypeStruct((b, s, LANES), F32),
                   jax.ShapeDtypeStruct((8, LANES), F32)],
        scratch_shapes=[pltpu.VMEM((8, LANES), F32), pltpu.VMEM((LANES, tm), F32)],
        compiler_params=pltpu.CompilerParams(
            dimension_semantics=("arbitrary", "arbitrary"), vmem_limit_bytes=VMEM_LIMIT),
        name="outproj",
    )(mixa, yconv, w_out.astype(BF16), x, mod, norm2_w.reshape(1, d), wr_hi, wr_lo, br, tri)


TOK_TILE = 512
ROW_UNROLL = 8


def _dest_kernel(ps_ref, ri_ref, d_ref):
    e = ri_ref[0, 0]
    acc = pltpu.roll(e, 6, 0)
    for x in range(N_EXPERTS):
        acc = jnp.where(e == x, acc + ps_ref[x], acc)
    d_ref[0, 0] = acc


def _dest_rows(pad_start, ri_c):
    b, nt, _, tm = ri_c.shape
    return pl.pallas_call(
        _dest_kernel,
        grid_spec=pltpu.PrefetchScalarGridSpec(
            num_scalar_prefetch=1,
            grid=(b, nt),
            in_specs=[pl.BlockSpec((1, 1, 8, tm), lambda bi, i, ps: (bi, i, 0, 0))],
            out_specs=pl.BlockSpec((1, 1, 8, tm), lambda bi, i, ps: (bi, i, 0, 0))),
        out_shape=jax.ShapeDtypeStruct((b, nt, 8, tm), I32),
        name="dest_rows",
    )(pad_start, ri_c)


def _for_each_row(fn):
    for base in range(0, TOK_TILE, LANES):
        def group(g, carry, base=base):
            r0 = base + pl.multiple_of(g * ROW_UNROLL, ROW_UNROLL)
            for u in range(ROW_UNROLL):
                for k in range(2):
                    fn(r0 + u, k)
            return carry

        lax.fori_loop(0, LANES // ROW_UNROLL, group, 0)


def _dispatch_kernel(dest_ref, h2p_ref, xd_in_ref, xd_ref, sem):
    del xd_in_ref

    def row_copy(r, k):
        return pltpu.make_async_copy(h2p_ref.at[pl.ds(r, 1)], xd_ref.at[pl.ds(dest_ref[0, k, r], 1)], sem)

    _for_each_row(lambda r, k: row_copy(r, k).start())
    _for_each_row(lambda r, k: row_copy(r, k).wait())


def _dispatch(dest_t, h2p, n_rows):
    t, w = h2p.shape
    xd0 = jnp.zeros((n_rows, w), U32)
    return pl.pallas_call(
        _dispatch_kernel,
        grid=(t // TOK_TILE,),
        in_specs=[pl.BlockSpec((1, 8, TOK_TILE), lambda i: (i, 0, 0), memory_space=pltpu.SMEM),
                  pl.BlockSpec((TOK_TILE, w), lambda i: (i, 0)),
                  pl.BlockSpec(memory_space=pl.ANY)],
        out_specs=pl.BlockSpec(memory_space=pl.ANY),
        scratch_shapes=[pltpu.SemaphoreType.DMA(())],
        out_shape=jax.ShapeDtypeStruct((n_rows, w), U32),
        input_output_aliases={2: 0},
        compiler_params=pltpu.CompilerParams(
            dimension_semantics=("arbitrary",), vmem_limit_bytes=VMEM_LIMIT, has_side_effects=True),
        name="dispatch",
    )(dest_t, h2p, xd0)


def _experts_kernel(be_ref, nu_ref, xd_ref, w1_ref, w3_ref, w2_ref, y_ref, w1b_ref, w3b_ref, w2b_ref):
    i = pl.program_id(0)

    @pl.when((i == 0) | (be_ref[i] != be_ref[jnp.maximum(i - 1, 0)]))
    def _():
        w1b_ref[...] = w1_ref[0].astype(BF16)
        w3b_ref[...] = w3_ref[0].astype(BF16)
        w2b_ref[...] = w2_ref[0].astype(BF16)

    @pl.when(i < nu_ref[0])
    def _():
        word = xd_ref[...]
        xa = pltpu.bitcast(lax.shift_left(word, jnp.uint32(16)), F32).astype(BF16)
        xb = pltpu.bitcast(word & jnp.uint32(HI_MASK), F32).astype(BF16)
        dh = xa.shape[1]
        a = _dot(xa, w1b_ref[0:dh]) + _dot(xb, w1b_ref[dh:2 * dh])
        c = _dot(xa, w3b_ref[0:dh]) + _dot(xb, w3b_ref[dh:2 * dh])
        y_ref[...] = _dot((_silu(a) * c).astype(BF16), w2b_ref[...])

    @pl.when(i >= nu_ref[0])
    def _():
        y_ref[...] = jnp.zeros_like(y_ref)


def _experts(blk_expert, n_used, xd, w1, w3, w2):
    n_rows, dh = xd.shape
    _, d, de = w1.shape
    n_blocks = n_rows // MOE_BLOCK
    return pl.pallas_call(
        _experts_kernel,
        grid_spec=pltpu.PrefetchScalarGridSpec(
            num_scalar_prefetch=2,
            grid=(n_blocks,),
            in_specs=[pl.BlockSpec((MOE_BLOCK, dh), lambda i, be, nu: (i, 0)),
                      pl.BlockSpec((1, d, de), lambda i, be, nu: (be[i], 0, 0)),
                      pl.BlockSpec((1, d, de), lambda i, be, nu: (be[i], 0, 0)),
                      pl.BlockSpec((1, de, d), lambda i, be, nu: (be[i], 0, 0))],
            out_specs=pl.BlockSpec((MOE_BLOCK, d), lambda i, be, nu: (i, 0)),
            scratch_shapes=[pltpu.VMEM((d, de), BF16), pltpu.VMEM((d, de), BF16), pltpu.VMEM((de, d), BF16)]),
        out_shape=jax.ShapeDtypeStruct((n_rows, d), F32),
        compiler_params=pltpu.CompilerParams(
            dimension_semantics=("arbitrary",), vmem_limit_bytes=VMEM_LIMIT),
        name="experts",
    )(blk_expert, n_used, xd, w1, w3, w2)


def _combine_kernel(dest_ref, dnext_ref, rw_ref, x1_ref, mod_ref, y_ref, o_ref, buf_ref, sems):
    i = pl.program_id(0)
    slot = i % 2

    def gather(idx_ref, slot_, start):
        def one(r, k):
            cp = pltpu.make_async_copy(y_ref.at[pl.ds(idx_ref[0, k, r], 1)],
                                       buf_ref.at[slot_, k, pl.ds(r, 1)], sems.at[slot_])
            if start:
                cp.start()
            else:
                cp.wait()

        _for_each_row(one)

    @pl.when(i == 0)
    def _():
        gather(dest_ref, slot, True)

    @pl.when(i + 1 < pl.num_programs(0))
    def _():
        gather(dnext_ref, 1 - slot, True)

    gather(dest_ref, slot, False)
    w = rw_ref[...]
    g2 = mod_ref[0][5:6]
    o_ref[...] = x1_ref[...] + g2 * (w[:, 0:1] * buf_ref[slot, 0] + w[:, 1:2] * buf_ref[slot, 1])


def _combine(dest_t, rw, x1, mod, y_disp, seq):
    t, d = x1.shape
    tiles_per_seq = seq // TOK_TILE
    n_tiles = t // TOK_TILE
    return pl.pallas_call(
        _combine_kernel,
        grid=(n_tiles,),
        in_specs=[pl.BlockSpec((1, 8, TOK_TILE), lambda i: (i, 0, 0), memory_space=pltpu.SMEM),
                  pl.BlockSpec((1, 8, TOK_TILE), lambda i: (jnp.minimum(i + 1, n_tiles - 1), 0, 0),
                               memory_space=pltpu.SMEM),
                  pl.BlockSpec((TOK_TILE, LANES), lambda i: (i, 0)),
                  pl.BlockSpec((TOK_TILE, d), lambda i: (i, 0)),
                  pl.BlockSpec((1, 6, d), lambda i: (i // tiles_per_seq, 0, 0)),
                  pl.BlockSpec(memory_space=pl.ANY)],
        out_specs=pl.BlockSpec((TOK_TILE, d), lambda i: (i, 0)),
        scratch_shapes=[pltpu.VMEM((2, 2, TOK_TILE, d), F32), pltpu.SemaphoreType.DMA((2,))],
        out_shape=jax.ShapeDtypeStruct((t, d), F32),
        compiler_params=pltpu.CompilerParams(
            dimension_semantics=("arbitrary",), vmem_limit_bytes=VMEM_LIMIT),
        name="combine",
    )(dest_t, dest_t, rw, x1, mod, y_disp)


def _moe(x1, h2p, ri, rw, cnt, mod, w1, w3, w2):
    b, s, d = x1.shape
    t = b * s
    a = 2 * t
    counts = cnt[0, ROUTE_E0:ROUTE_E0 + N_EXPERTS].astype(I32)
    padded = (counts + MOE_BLOCK - 1) // MOE_BLOCK * MOE_BLOCK
    pad_end = jnp.cumsum(padded)
    pad_start = (pad_end - padded).astype(I32)
    n_blocks = (a + N_EXPERTS * (MOE_BLOCK - 1)) // MOE_BLOCK + 1
    blk_start = jnp.arange(n_blocks, dtype=I32) * MOE_BLOCK
    blk_expert = jnp.minimum(
        jnp.sum((pad_end[None, :] <= blk_start[:, None]).astype(I32), axis=1), N_EXPERTS - 1)
    n_used = (pad_end[-1:] // MOE_BLOCK).astype(I32)
    dest_t = _dest_rows(pad_start, ri).reshape(t // TOK_TILE, 8, TOK_TILE)
    xd = _dispatch(dest_t, h2p.reshape(t, d // 2), n_blocks * MOE_BLOCK)
    y_disp = _experts(blk_expert, n_used, xd, w1, w3, w2)
    out = _combine(dest_t, rw.reshape(t, LANES), x1.reshape(t, d), mod, y_disp, s)
    return out.reshape(b, s, d)


def kernel(x, c, w_ada, b_ada, norm1_w, w_in, q_norm_w, k_norm_w, cmp_pos_k, cmp_pos_v, cmp_k_w1, cmp_k_w2,
           cmp_v_w1, cmp_v_w2, conv_w, out_norm_w, w_out, rel_bias, norm2_w, w_group, b_group, w_expert,
           b_expert, w1, w3, w2):
    t_hi, t_lo, t_c = _bias_tables(rel_bias)
    for l in range(w_ada.shape[0]):
        mod = _ada(c, w_ada[l], b_ada[l])
        q, kvc, kaug, vaug, kwin, vwaug, gates, yconv = _inproj(
            x, mod, norm1_w[l], w_in[l], q_norm_w[l], k_norm_w[l], conv_w[l], out_norm_w[l])
        vaug, vscale = _to_f8(vaug)
        vwaug, vwscale = _to_f8(vwaug)
        rows = _kvc_rows(kvc)
        k_c = _compress(rows, 0, cmp_k_w1[l], cmp_k_w2[l], cmp_pos_k[l], k_norm_w[l, 0], True)
        v_c = _compress(rows, 1, cmp_v_w1[l], cmp_v_w2[l], cmp_pos_v[l], k_norm_w[l, 0], False)
        o_cmp, pen = _cmpsel(q, k_c, v_c, t_c)
        mixa = _slcwin(q, pen, kaug, vaug, vscale, kwin, vwaug, vwscale, o_cmp, gates, t_hi, t_lo,
                       out_norm_w[l])
        x1, h2p, ri, rw, cnt = _outproj(mixa, yconv, w_out[l], x, mod, norm2_w[l],
                                        w_group[l], b_group[l], w_expert[l], b_expert[l])
        x = _moe(x1, h2p, ri, rw, cnt, mod, w1[l], w3[l], w2[l])
    return x
```

```python
import functools
import math

import numpy as np
import jax
import jax.numpy as jnp
from jax import lax
from jax.experimental import pallas as pl
from jax.experimental.pallas import tpu as pltpu

F32 = jnp.float32
BF16 = jnp.bfloat16
F8 = jnp.float8_e4m3fn
P_SCALE = 256.0
F8_TARGET = 240.0
I32 = jnp.int32

HEAD_DIM = 64
N_HEADS = 8
N_KV = 2
HPG = N_HEADS // N_KV
D_ATTN = N_HEADS * HEAD_DIM
D_CONV = 512
KV_DIM = N_KV * HEAD_DIM
CMP_LEN = 32
CMP_STRIDE = 16
CMP_HIDDEN = 256
SLC_LEN = 64
N_SELECT = 16
WINDOW = 512
N_BUCKETS = 32
MAX_DISTANCE = 128
N_GROUPS = 8
EXPERTS_PER_GROUP = 8
N_EXPERTS = N_GROUPS * EXPERTS_PER_GROUP
MOE_BLOCK = 512
FORCED_SCORE = 1e4
EPS = 1e-6

LANES = 128
QB = 512
ROWS_Q = HPG * QB
CMP_CHUNK = 256
CMP_DIST0 = CMP_STRIDE * CMP_CHUNK - (CMP_LEN - 1) - QB
IMP_PAD = CMP_CHUNK // 4
OV_ROWS = IMP_PAD + 8
FAR_GROUP = 4
PEN_BLOCKS = 128
HALF_KEYS = PEN_BLOCKS * SLC_LEN
NEG = -1e30
PICKED = -3e38
ROUTE_E0 = N_GROUPS
U32 = jnp.uint32
HI_MASK = 0xFFFF0000
TOK_TILE = 512
ROW_UNROLL = 8
VMEM_LIMIT = 56 * 1024 * 1024


def _dot(a, b):
    return jnp.dot(a, b, preferred_element_type=F32)


def _dot_nt(a, b):
    return lax.dot_general(a, b, (((1,), (1,)), ((), ())), preferred_element_type=F32)


def _hi_lo(a):
    hi = a.astype(BF16)
    lo = (a - hi.astype(F32)).astype(BF16)
    return hi, lo


def _dot3(a, b_hi, b_lo):
    a_hi, a_lo = _hi_lo(a)
    return _dot(a_hi, b_hi) + (_dot(a_lo, b_hi) + _dot(a_hi, b_lo))


def _group_sumsq(v, gm):
    hi, lo = _hi_lo(v * v)
    return _dot(hi, gm) + _dot(lo, gm)


def _silu(v):
    return v * jax.nn.sigmoid(v)


def _ada_kernel(c_ref, w_ref, b_ref, o_ref):
    cond = _silu(c_ref[...])
    w_hi, w_lo = _hi_lo(w_ref[...])
    o_ref[...] = _dot3(cond, w_hi, w_lo) + b_ref[...]


def _ada(c, w_ada, b_ada):
    b, d = c.shape
    n = w_ada.shape[1]
    tn = n // 4
    c8 = jnp.pad(c, ((0, 8 - b), (0, 0)))
    out = pl.pallas_call(
        _ada_kernel,
        grid=(n // tn,),
        in_specs=[pl.BlockSpec((8, d), lambda i: (0, 0)),
                  pl.BlockSpec((d, tn), lambda i: (0, i)),
                  pl.BlockSpec((1, tn), lambda i: (0, i))],
        out_specs=pl.BlockSpec((8, tn), lambda i: (0, i)),
        out_shape=jax.ShapeDtypeStruct((8, n), F32),
        compiler_params=pltpu.CompilerParams(vmem_limit_bytes=VMEM_LIMIT),
        name="ada",
    )(c8, w_ada, b_ada.reshape(1, n))
    return out[:b].reshape(b, 6, d)


def _inproj_kernel(x_ref, xh_ref, mod_ref, n1w_ref, wq_ref, wkv_ref, wg_ref, wbcu_ref,
                   qnw_ref, knw_ref, convw_ref, onw_ref, gm_ref,
                   q_ref, kvc_ref, kaug_ref, vaug_ref, kwin_ref, vwaug_ref, gates_ref, yconv_ref,
                   ext_ref, *, tm):
    i = pl.program_id(1)
    mod = mod_ref[0]
    sh1 = mod[0:1]
    sc1 = mod[1:2]

    def norm_mod(v):
        ms = jnp.mean(v * v, axis=-1, keepdims=True)
        y = (v * lax.rsqrt(ms + EPS)) * n1w_ref[...]
        return (y * (1.0 + sc1) + sh1).astype(BF16)

    h = norm_mod(x_ref[0])
    gm = gm_ref[...]
    gm_kv = gm_ref[0:KV_DIM, 0:KV_DIM]

    q = _dot(h, wq_ref[...])
    qn = q * lax.rsqrt(_group_sumsq(q, gm) * (1.0 / HEAD_DIM) + EPS) * qnw_ref[...]
    qn = (qn * (HEAD_DIM ** -0.5)).astype(BF16)
    for hd in range(N_HEADS):
        q_ref[0, hd] = qn[:, hd * HEAD_DIM:(hd + 1) * HEAD_DIM]

    kv = _dot(h, wkv_ref[...])
    kvc_ref[0] = kv[:, 0:2 * KV_DIM].astype(BF16)
    ks = kv[:, 2 * KV_DIM:3 * KV_DIM]
    vs = kv[:, 3 * KV_DIM:4 * KV_DIM]
    kw = kv[:, 4 * KV_DIM:5 * KV_DIM]
    vw = kv[:, 5 * KV_DIM:6 * KV_DIM]
    ksn = ks * lax.rsqrt(_group_sumsq(ks, gm_kv) * (1.0 / HEAD_DIM) + EPS) * knw_ref[0:1]
    kwn = kw * lax.rsqrt(_group_sumsq(kw, gm_kv) * (1.0 / HEAD_DIM) + EPS) * knw_ref[1:2]

    lane = lax.broadcasted_iota(I32, (tm, LANES), 1)
    row = lax.broadcasted_iota(I32, (tm, LANES), 0)
    blk = ((i * tm + row) // SLC_LEN) % PEN_BLOCKS
    onehot = jnp.where(lane == blk, 1.0, 0.0).astype(BF16)
    ones_col = jnp.where(lax.broadcasted_iota(I32, (tm, HEAD_DIM), 1) == 0, 1.0, 0.0).astype(BF16)
    zeros64 = jnp.zeros((tm, HEAD_DIM), BF16)
    for g in range(N_KV):
        sl = slice(g * HEAD_DIM, (g + 1) * HEAD_DIM)
        kaug_ref[0, g, :, 0:LANES] = onehot
        kaug_ref[0, g, :, LANES:LANES + HEAD_DIM] = ksn[:, sl].astype(BF16)
        kaug_ref[0, g, :, LANES + HEAD_DIM:2 * LANES] = zeros64
        vaug_ref[0, g, :, 0:HEAD_DIM] = vs[:, sl].astype(BF16)
        vaug_ref[0, g, :, HEAD_DIM:LANES] = ones_col
        kwin_ref[0, g] = kwn[:, sl].astype(BF16)
        vwaug_ref[0, g, :, 0:HEAD_DIM] = vw[:, sl].astype(BF16)
        vwaug_ref[0, g, :, HEAD_DIM:LANES] = ones_col

    gates_ref[0] = jax.nn.sigmoid(_dot(h, wg_ref[...]))

    bcu = _dot(h, wbcu_ref[...])
    bg = bcu[:, 0:D_CONV]
    cu = bcu[:, D_CONV:2 * D_CONV] * bcu[:, 2 * D_CONV:3 * D_CONV]
    hh = norm_mod(xh_ref[0])
    cuh = _dot(hh, wbcu_ref[:, D_CONV:2 * D_CONV]) * _dot(hh, wbcu_ref[:, 2 * D_CONV:3 * D_CONV])
    cuh = jnp.where(i > 0, cuh, 0.0)
    ext_ref[0:8] = cuh
    ext_ref[8:8 + tm] = cu
    cw = convw_ref[...]
    y = bg * (cw[0:1] * ext_ref[6:6 + tm] + cw[1:2] * ext_ref[7:7 + tm] + cw[2:3] * cu)
    yn = y * lax.rsqrt(_group_sumsq(y, gm) * (1.0 / HEAD_DIM) + EPS) * onw_ref[...]
    yconv_ref[0] = yn.astype(BF16)


def _inproj(x, mod, norm1_w, w_in, q_norm_w, k_norm_w, conv_w, out_norm_w, tm=512):
    b, s, d = x.shape
    o_kv = D_ATTN
    o_g = o_kv + 6 * KV_DIM
    o_b = o_g + 3 * N_HEADS
    wq = w_in[:, 0:o_kv].astype(BF16)
    wkv = w_in[:, o_kv:o_g].astype(BF16)
    wg = jnp.pad(w_in[:, o_g:o_b].reshape(d, N_KV, 3 * HPG),
                 ((0, 0), (0, 0), (0, LANES - 3 * HPG))).reshape(d, N_KV * LANES).astype(BF16)
    wbcu = w_in[:, o_b:].astype(BF16)
    qnw = jnp.tile(q_norm_w, N_HEADS).reshape(1, D_ATTN)
    knw = jnp.stack([jnp.tile(k_norm_w[1], N_KV), jnp.tile(k_norm_w[2], N_KV)])
    onw = out_norm_w[D_ATTN:].reshape(1, D_CONV)
    gidx = np.arange(D_ATTN) // HEAD_DIM
    gm = jnp.asarray((gidx[:, None] == gidx[None, :]).astype(np.float32), dtype=BF16)

    full = lambda shape: pl.BlockSpec(shape, lambda bi, i: (0,) * len(shape))
    outs = pl.pallas_call(
        functools.partial(_inproj_kernel, tm=tm),
        grid=(b, s // tm),
        in_specs=[pl.BlockSpec((1, tm, d), lambda bi, i: (bi, i, 0)),
                  pl.BlockSpec((1, 8, d), lambda bi, i: (bi, jnp.maximum(i * (tm // 8) - 1, 0), 0)),
                  pl.BlockSpec((1, 6, d), lambda bi, i: (bi, 0, 0)),
                  full((1, d)), full(wq.shape), full(wkv.shape), full(wg.shape), full(wbcu.shape),
                  full(qnw.shape), full(knw.shape), full(conv_w.shape), full(onw.shape), full(gm.shape)],
        out_specs=[pl.BlockSpec((1, N_HEADS, tm, HEAD_DIM), lambda bi, i: (bi, 0, i, 0)),
                   pl.BlockSpec((1, tm, 2 * KV_DIM), lambda bi, i: (bi, i, 0)),
                   pl.BlockSpec((1, N_KV, tm, 2 * LANES), lambda bi, i: (bi, 0, i, 0)),
                   pl.BlockSpec((1, N_KV, tm, LANES), lambda bi, i: (bi, 0, i, 0)),
                   pl.BlockSpec((1, N_KV, tm, HEAD_DIM), lambda bi, i: (bi, 0, i, 0)),
                   pl.BlockSpec((1, N_KV, tm, LANES), lambda bi, i: (bi, 0, i, 0)),
                   pl.BlockSpec((1, tm, N_KV * LANES), lambda bi, i: (bi, i, 0)),
                   pl.BlockSpec((1, tm, D_CONV), lambda bi, i: (bi, i, 0))],
        out_shape=[jax.ShapeDtypeStruct((b, N_HEADS, s, HEAD_DIM), BF16),
                   jax.ShapeDtypeStruct((b, s, 2 * KV_DIM), BF16),
                   jax.ShapeDtypeStruct((b, N_KV, s, 2 * LANES), BF16),
                   jax.ShapeDtypeStruct((b, N_KV, s, LANES), BF16),
                   jax.ShapeDtypeStruct((b, N_KV, s, HEAD_DIM), BF16),
                   jax.ShapeDtypeStruct((b, N_KV, s, LANES), BF16),
                   jax.ShapeDtypeStruct((b, s, N_KV * LANES), F32),
                   jax.ShapeDtypeStruct((b, s, D_CONV), BF16)],
        scratch_shapes=[pltpu.VMEM((tm + 8, D_CONV), F32)],
        compiler_params=pltpu.CompilerParams(
            dimension_semantics=("parallel", "arbitrary"), vmem_limit_bytes=VMEM_LIMIT),
        name="inproj",
    )(x, x, mod, norm1_w.reshape(1, d), wq, wkv, wg, wbcu, qnw, knw, conv_w, onw, gm)
    return outs


def _compress_kernel(r_ref, w1_ref, w2_ref, pos_ref, nw_ref, o_ref, *, norm):
    r = r_ref[0, 0, 0]
    nr = r.shape[0]
    half = (CMP_LEN // 2) * HEAD_DIM
    w1a = w1_ref[0:half, :]
    w1b = w1_ref[half:2 * half, :]
    pos = pos_ref[...]
    cst = (_dot(jnp.broadcast_to(pos[0:1], (8, half)).astype(BF16), w1a)
           + _dot(jnp.broadcast_to(pos[1:2], (8, half)).astype(BF16), w1b))[0:1]
    upper = pltpu.roll(_dot(r, w1b), nr - 1, 0)
    hid = _dot(r, w1a) + upper + cst
    out = _dot(_silu(hid).astype(BF16), w2_ref[...])
    if norm:
        ms = jnp.mean(out * out, axis=-1, keepdims=True)
        out = out * lax.rsqrt(ms + EPS) * nw_ref[...]
        o_ref[0, 0] = out.astype(BF16)
    else:
        o_ref[0, 0, :, 0:HEAD_DIM] = out.astype(BF16)
        o_ref[0, 0, :, HEAD_DIM:LANES] = jnp.where(
            lax.broadcasted_iota(I32, (nr, HEAD_DIM), 1) == 0, 1.0, 0.0).astype(BF16)


def _compress(rows, kv_index, w1, w2, pos_emb, norm_w, norm):
    b, _, g, nr, width = rows.shape
    out_w = HEAD_DIM if norm else LANES
    return pl.pallas_call(
        functools.partial(_compress_kernel, norm=norm),
        grid=(b, g),
        in_specs=[pl.BlockSpec((1, 1, 1, nr, width), lambda bi, gi: (bi, kv_index, gi, 0, 0)),
                  pl.BlockSpec(w1.shape, lambda bi, gi: (0, 0)),
                  pl.BlockSpec(w2.shape, lambda bi, gi: (0, 0)),
                  pl.BlockSpec((2, width), lambda bi, gi: (0, 0)),
                  pl.BlockSpec((1, HEAD_DIM), lambda bi, gi: (0, 0))],
        out_specs=pl.BlockSpec((1, 1, nr, out_w), lambda bi, gi: (bi, gi, 0, 0)),
        out_shape=jax.ShapeDtypeStruct((b, g, nr, out_w), BF16),
        compiler_params=pltpu.CompilerParams(vmem_limit_bytes=VMEM_LIMIT),
        name="compress_k" if norm else "compress_v",
    )(rows, w1.astype(BF16), w2.astype(BF16), pos_emb.reshape(2, width), norm_w.reshape(1, HEAD_DIM))


def _kvc_rows(kvc):
    b, s, _ = kvc.shape
    r = kvc.reshape(b, s, 2, N_KV, HEAD_DIM).transpose(0, 2, 3, 1, 4)
    return r.reshape(b, 2, N_KV, s // CMP_STRIDE, CMP_STRIDE * HEAD_DIM)


def _bucket_np(dist):
    n = np.maximum(dist, 0)
    max_exact = N_BUCKETS // 2
    nf = np.maximum(n, 1).astype(np.float32)
    large = max_exact + (np.log(nf / np.float32(max_exact)) / np.float32(math.log(MAX_DISTANCE / max_exact))
                         * np.float32(N_BUCKETS - max_exact)).astype(np.int32)
    large = np.minimum(large, N_BUCKETS - 1)
    return np.where(n < max_exact, n, large)


_BUCKET_START = [int(np.argmax(_bucket_np(np.arange(MAX_DISTANCE + 1)) >= k)) for k in range(N_BUCKETS)]


def _bias_tables_kernel(rbt_ref, thi_ref, tlo_ref, tc_ref):
    head = pl.program_id(0) * HPG + pl.program_id(1)
    far = rbt_ref[head, N_BUCKETS - 1]

    def table(dist):
        v = jnp.zeros(dist.shape, F32)
        for k in range(N_BUCKETS - 2, -1, -1):
            v = jnp.where(dist < _BUCKET_START[k + 1], rbt_ref[head, k] - far, v)
        return jnp.where(dist >= 0, v, NEG)

    def dist(shape, row_mul, col_mul, off):
        return (lax.broadcasted_iota(I32, shape, 0) * row_mul
                + lax.broadcasted_iota(I32, shape, 1) * col_mul + off)

    thi_ref[0] = table(dist((QB, QB), 1, -1, 0))
    tlo_ref[0] = table(dist((QB, LANES), 1, -1, LANES))
    tc_ref[0] = table(dist((QB, CMP_CHUNK), 1, -CMP_STRIDE, CMP_DIST0))


def _bias_tables(rel_bias):
    return pl.pallas_call(
        _bias_tables_kernel,
        grid=(N_KV, HPG),
        in_specs=[pl.BlockSpec(memory_space=pltpu.SMEM)],
        out_specs=[pl.BlockSpec((1, QB, QB), lambda g, h: (g, h, 0)),
                   pl.BlockSpec((1, QB, LANES), lambda g, h: (g, h, 0)),
                   pl.BlockSpec((1, QB, CMP_CHUNK), lambda g, h: (g, h, 0))],
        out_shape=[jax.ShapeDtypeStruct((N_KV, ROWS_Q, QB), F32),
                   jax.ShapeDtypeStruct((N_KV, ROWS_Q, LANES), F32),
                   jax.ShapeDtypeStruct((N_KV, ROWS_Q, CMP_CHUNK), F32)],
        name="bias_tables",
    )(rel_bias.T)


def _cmpsel_kernel(q_ref, kc_ref, vc_ref, tab_ref, ovt_ref, ocmp_ref, pen_ref,
                   m_ref, acc_ref, imp_ref, score_ref, invt_ref, *, nsbp, n_sel):
    j = pl.program_id(2)
    q4 = q_ref[0].reshape(ROWS_Q, HEAD_DIM)
    nq16 = QB // CMP_STRIDE
    n_chunks = (nq16 * (j + 1) - 1) // CMP_CHUNK + 1
    ni = lax.broadcasted_iota(I32, (1, CMP_CHUNK), 1)

    def chunk_start(m):
        return pl.multiple_of(nq16 * (j + 1) - CMP_CHUNK * m, nq16)

    def logits(m):
        n0p = chunk_start(m)
        s = _dot_nt(q4, kc_ref[0, 0, pl.ds(n0p, CMP_CHUNK), :])
        return jnp.where(ni + n0p >= CMP_CHUNK, s, NEG)

    s0 = logits(0) + tab_ref[0]
    m_ref[...] = s0

    def stats(m, carry):
        m_ref[...] = jnp.maximum(m_ref[...], logits(m))
        return carry

    lax.fori_loop(1, n_chunks, stats, 0)
    m_fin = jnp.max(m_ref[...], axis=1, keepdims=True)
    m_ref[...] = jnp.broadcast_to(m_fin, (ROWS_Q, CMP_CHUNK))
    acc_ref[...] = jnp.zeros_like(acc_ref)
    imp_ref[...] = jnp.zeros_like(imp_ref)

    def accumulate(m, s):
        n0p = chunk_start(m)
        p = jnp.exp(s - m_ref[...]).astype(BF16)
        acc_ref[...] += _dot(p, vc_ref[0, 0, pl.ds(n0p, CMP_CHUNK), :])
        r0 = pl.multiple_of(n0p // 4, 8)
        for hd in range(HPG):
            imp_ref[hd, pl.ds(r0, OV_ROWS), :] += _dot_nt(ovt_ref[...], p[hd * QB:(hd + 1) * QB])

    accumulate(0, s0)

    def second(m, carry):
        accumulate(m, logits(m))
        return carry

    lax.fori_loop(1, n_chunks, second, 0)
    acc = acc_ref[...]
    inv = jnp.where(m_fin > 0.5 * NEG, 1.0 / acc[:, HEAD_DIM:HEAD_DIM + 1], 0.0)
    ocmp_ref[0, 0] = (acc[:, 0:HEAD_DIM] * inv).reshape(HPG, QB, HEAD_DIM)

    invt_ref[...] = jnp.broadcast_to(inv, (ROWS_Q, LANES)).T
    imp = jnp.zeros((nsbp, QB), F32)
    for hd in range(HPG):
        imp = imp + imp_ref[hd, IMP_PAD:IMP_PAD + nsbp, :] * invt_ref[0:1, hd * QB:(hd + 1) * QB]
    srow = lax.broadcasted_iota(I32, (nsbp, QB), 0)
    qcol = lax.broadcasted_iota(I32, (nsbp, QB), 1)
    cur = (QB // SLC_LEN) * j + qcol // SLC_LEN
    forced = (srow == 0) | (srow == cur) | (srow == cur - 1)
    score_ref[...] = jnp.where(forced, -1.0, jnp.where(srow <= cur, imp, -1.0))

    quarter = nsbp // 4
    case = ((QB // SLC_LEN) * (j + 1) - 1) // quarter
    for k in range(4):
        rows = quarter * (k + 1)

        @pl.when(case == k)
        def _(rows=rows):
            srow_k = lax.broadcasted_iota(I32, (rows, QB), 0)

            def pick(r, carry):
                sc = score_ref[0:rows]
                mx = jnp.max(sc, axis=0, keepdims=True)
                idx = jnp.min(jnp.where(sc == mx, srow_k, nsbp), axis=0, keepdims=True)
                score_ref[0:rows] = jnp.where(srow_k == idx, PICKED, sc)
                return carry

            lax.fori_loop(0, max(n_sel - 3, 0), pick, 0)
    pen_other = jnp.where(srow <= cur, jnp.where(score_ref[...] == PICKED, 0.0, NEG), NEG)
    pen_t = jnp.where(forced, 0.0, pen_other).T
    for hf in range(nsbp // PEN_BLOCKS):
        pen_ref[0, 0, hf] = pen_t[:, hf * PEN_BLOCKS:(hf + 1) * PEN_BLOCKS].astype(BF16)


def _overlap_t():
    n = np.arange(CMP_CHUNK)
    s = np.arange(OV_ROWS)
    first = (n * CMP_STRIDE) // SLC_LEN
    last = (n * CMP_STRIDE + CMP_LEN - 1) // SLC_LEN
    ov = (s[:, None] >= first[None, :]) & (s[:, None] <= last[None, :])
    return jnp.asarray(ov.astype(np.float32), dtype=BF16)


def _cmpsel(q, k_c, v_c, tab):
    b, _, s, _ = q.shape
    nr = k_c.shape[2]
    nq = s // QB
    nsb = s // SLC_LEN
    nh = -(-nsb // PEN_BLOCKS)
    nsbp = nh * PEN_BLOCKS
    n_sel = min(N_SELECT, nsb)
    pad = ((0, 0), (0, 0), (CMP_CHUNK, 0), (0, 0))
    kc_p = jnp.pad(k_c, pad)
    vc_p = jnp.pad(v_c, pad)
    ovt = _overlap_t()
    return pl.pallas_call(
        functools.partial(_cmpsel_kernel, nsbp=nsbp, n_sel=n_sel),
        grid=(b, N_KV, nq),
        in_specs=[pl.BlockSpec((1, HPG, QB, HEAD_DIM), lambda bi, g, j: (bi, g, j, 0)),
                  pl.BlockSpec((1, 1, nr + CMP_CHUNK, HEAD_DIM), lambda bi, g, j: (bi, g, 0, 0)),
                  pl.BlockSpec((1, 1, nr + CMP_CHUNK, LANES), lambda bi, g, j: (bi, g, 0, 0)),
                  pl.BlockSpec((1, ROWS_Q, CMP_CHUNK), lambda bi, g, j: (g, 0, 0)),
                  pl.BlockSpec(ovt.shape, lambda bi, g, j: (0, 0))],
        out_specs=[pl.BlockSpec((1, 1, HPG, QB, HEAD_DIM), lambda bi, g, j: (bi, g, 0, j, 0)),
                   pl.BlockSpec((1, 1, nh, QB, PEN_BLOCKS), lambda bi, g, j: (bi, g, 0, j, 0))],
        out_shape=[jax.ShapeDtypeStruct((b, N_KV, HPG, s, HEAD_DIM), F32),
                   jax.ShapeDtypeStruct((b, N_KV, nh, s, PEN_BLOCKS), BF16)],
        scratch_shapes=[pltpu.VMEM((ROWS_Q, CMP_CHUNK), F32),
                        pltpu.VMEM((ROWS_Q, LANES), F32),
                        pltpu.VMEM((HPG, IMP_PAD + nsbp + 8, QB), F32),
                        pltpu.VMEM((nsbp, QB), F32),
                        pltpu.VMEM((LANES, ROWS_Q), F32)],
        compiler_params=pltpu.CompilerParams(
            dimension_semantics=("parallel", "parallel", "arbitrary"), vmem_limit_bytes=VMEM_LIMIT),
        name="cmpsel",
    )(q, kc_p, vc_p, tab, ovt)


def _to_f8_kernel(v_ref, o_ref, sc_ref):
    v = v_ref[0, 0].astype(F32)
    lane = lax.broadcasted_iota(I32, (1, LANES), 1)
    cmax = jnp.max(jnp.abs(v), axis=0, keepdims=True)
    scale = jnp.where(lane < HEAD_DIM, jnp.maximum(cmax, 1e-30) * (1.0 / F8_TARGET), 1.0)
    o_ref[0, 0] = (v / scale).astype(F8)
    sc_ref[0, 0] = jnp.broadcast_to(scale, (8, LANES))


def _to_f8(vaug):
    b, g, s, w = vaug.shape
    return pl.pallas_call(
        _to_f8_kernel,
        grid=(b, g),
        in_specs=[pl.BlockSpec((1, 1, s, w), lambda bi, gi: (bi, gi, 0, 0))],
        out_specs=[pl.BlockSpec((1, 1, s, w), lambda bi, gi: (bi, gi, 0, 0)),
                   pl.BlockSpec((1, 1, 8, w), lambda bi, gi: (bi, gi, 0, 0))],
        out_shape=[jax.ShapeDtypeStruct((b, g, s, w), F8), jax.ShapeDtypeStruct((b, g, 8, w), F32)],
        compiler_params=pltpu.CompilerParams(vmem_limit_bytes=VMEM_LIMIT),
        name="to_f8",
    )(vaug)


def _slcwin_kernel(q_ref, pen_ref, kaug_ref, vaug_ref, vsc_ref, kwp_ref, kwc_ref, vwp_ref, vwc_ref,
                   vwsc_ref, thi_ref, tlo_ref, ocmp_ref, gates_ref, onw_ref, o_ref,
                   lhs_ref, m_ref, acc_ref):
    j = pl.program_id(2)
    q4 = q_ref[0].reshape(ROWS_Q, HEAD_DIM)
    near = QB - LANES

    lhs_ref[:, LANES:LANES + HEAD_DIM] = q4
    lhs_ref[:, LANES + HEAD_DIM:2 * LANES] = jnp.zeros((ROWS_Q, HEAD_DIM), BF16)

    def set_pen(half):
        p = pen_ref[0, 0, half]
        for hd in range(HPG):
            lhs_ref[hd * QB:(hd + 1) * QB, 0:LANES] = p

    def reset():
        m_ref[...] = jnp.full_like(m_ref, NEG)
        acc_ref[...] = jnp.zeros_like(acc_ref)

    def update(s, v):
        sb = s.astype(BF16)
        m_prev = m_ref[...]
        m_new = jnp.maximum(m_prev, jnp.max(sb, axis=1, keepdims=True).astype(F32))
        x = sb - jnp.concatenate([m_new.astype(BF16)] * (s.shape[1] // LANES), axis=1)
        p = (jnp.exp(x) * P_SCALE).astype(F8)
        acc_ref[...] = acc_ref[...] * jnp.exp(m_prev - m_new) + _dot(p, v)
        m_ref[...] = m_new

    def add_near_bias(s):
        return jnp.concatenate([s[:, :near], s[:, near:] + tlo_ref[0]], axis=1)

    def result(scale_ref):
        acc = acc_ref[...]
        return acc[:, 0:HEAD_DIM] * scale_ref[0, 0, 0:1, 0:HEAD_DIM] / acc[:, HEAD_DIM:HEAD_DIM + 1]

    def slc_chunk(c, bias):
        k0 = pl.multiple_of(c * QB, QB)
        s = _dot_nt(lhs_ref[...], kaug_ref[0, 0, pl.ds(k0, QB), :])
        update(bias(s), vaug_ref[0, 0, pl.ds(k0, QB), :])

    chunks_per_half = HALF_KEYS // QB
    assert chunks_per_half % FAR_GROUP == 0
    reset()
    half_hi = j // chunks_per_half
    set_pen(half_hi)
    slc_chunk(j, lambda s: s + thi_ref[0])

    @pl.when(j > 0)
    def _():
        half_lo = (j - 1) // chunks_per_half

        @pl.when(half_lo != half_hi)
        def _():
            set_pen(half_lo)

        slc_chunk(j - 1, add_near_bias)

    n_far = jnp.maximum(j - 1, 0)

    def enter_half(c):
        @pl.when(c % chunks_per_half == 0)
        def _():
            set_pen(c // chunks_per_half)

    def far_group(i4, carry):
        c = FAR_GROUP * i4
        enter_half(c)
        for u in range(FAR_GROUP):
            slc_chunk(c + u, lambda s: s)
        return carry

    n_groups = n_far // FAR_GROUP
    lax.fori_loop(0, n_groups, far_group, 0)

    def far_single(c, carry):
        enter_half(c)
        slc_chunk(c, lambda s: s)
        return carry

    lax.fori_loop(n_groups * FAR_GROUP, n_far, far_single, 0)

    o_slc = result(vsc_ref)

    reset()
    update(_dot_nt(q4, kwc_ref[0, 0]) + thi_ref[0], vwc_ref[0, 0])

    @pl.when(j > 0)
    def _():
        s = _dot_nt(q4, kwp_ref[0, 0])
        qi = lax.broadcasted_iota(I32, (ROWS_Q, QB), 0) & (QB - 1)
        kk = lax.broadcasted_iota(I32, (ROWS_Q, QB), 1)
        s = jnp.where(kk > qi, s, NEG)
        update(add_near_bias(s), vwp_ref[0, 0])

    o_win = result(vwsc_ref)

    o_cmp = ocmp_ref[0, 0].reshape(ROWS_Q, HEAD_DIM)
    gt = gates_ref[0]
    outs = []
    for hd in range(HPG):
        rs = slice(hd * QB, (hd + 1) * QB)
        o = (gt[:, 3 * hd:3 * hd + 1] * o_cmp[rs] + gt[:, 3 * hd + 1:3 * hd + 2] * o_slc[rs]
             + gt[:, 3 * hd + 2:3 * hd + 3] * o_win[rs])
        ms = jnp.mean(o * o, axis=-1, keepdims=True)
        outs.append((o * lax.rsqrt(ms + EPS) * onw_ref[0, hd:hd + 1, :]).astype(BF16))
    o_ref[0] = jnp.concatenate(outs, axis=1)


def _slcwin(q, pen, kaug, vaug, vscale, kwin, vwaug, vwscale, o_cmp, gates, t_hi, t_lo, out_norm_w):
    assert QB == WINDOW
    b, _, s, _ = q.shape
    nq = s // QB
    nh = pen.shape[2]
    onw = out_norm_w[:D_ATTN].reshape(N_KV, HPG, HEAD_DIM)
    prev = lambda bi, g, j: (bi, g, jnp.maximum(j - 1, 0), 0)
    curr = lambda bi, g, j: (bi, g, j, 0)
    per_group = lambda bi, g, j: (bi, g, 0, 0)
    return pl.pallas_call(
        _slcwin_kernel,
        grid=(b, N_KV, nq),
        in_specs=[pl.BlockSpec((1, HPG, QB, HEAD_DIM), lambda bi, g, j: (bi, g, j, 0)),
                  pl.BlockSpec((1, 1, nh, QB, PEN_BLOCKS), lambda bi, g, j: (bi, g, 0, j, 0)),
                  pl.BlockSpec((1, 1, s, 2 * LANES), lambda bi, g, j: (bi, g, 0, 0)),
                  pl.BlockSpec((1, 1, s, LANES), lambda bi, g, j: (bi, g, 0, 0)),
                  pl.BlockSpec((1, 1, 8, LANES), per_group),
                  pl.BlockSpec((1, 1, QB, HEAD_DIM), prev),
                  pl.BlockSpec((1, 1, QB, HEAD_DIM), curr),
                  pl.BlockSpec((1, 1, QB, LANES), prev),
                  pl.BlockSpec((1, 1, QB, LANES), curr),
                  pl.BlockSpec((1, 1, 8, LANES), per_group),
                  pl.BlockSpec((1, ROWS_Q, QB), lambda bi, g, j: (g, 0, 0)),
                  pl.BlockSpec((1, ROWS_Q, LANES), lambda bi, g, j: (g, 0, 0)),
                  pl.BlockSpec((1, 1, HPG, QB, HEAD_DIM), lambda bi, g, j: (bi, g, 0, j, 0)),
                  pl.BlockSpec((1, QB, LANES), lambda bi, g, j: (bi, j, g)),
                  pl.BlockSpec((1, HPG, HEAD_DIM), lambda bi, g, j: (g, 0, 0))],
        out_specs=pl.BlockSpec((1, QB, HPG * HEAD_DIM), lambda bi, g, j: (bi, j, g)),
        out_shape=jax.ShapeDtypeStruct((b, s, D_ATTN), BF16),
        scratch_shapes=[pltpu.VMEM((ROWS_Q, 2 * LANES), BF16),
                        pltpu.VMEM((ROWS_Q, LANES), F32),
                        pltpu.VMEM((ROWS_Q, LANES), F32)],
        compiler_params=pltpu.CompilerParams(
            dimension_semantics=("parallel", "parallel", "arbitrary"), vmem_limit_bytes=VMEM_LIMIT),
        name="slcwin",
    )(q, pen, kaug, vaug, vscale, kwin, kwin, vwaug, vwaug, vwscale, t_hi, t_lo, o_cmp, gates, onw)


def _outproj_kernel(mixa_ref, yc_ref, wo_ref, x_ref, mod_ref, n2w_ref, wrh_ref, wrl_ref, br_ref, tri_ref,
                    x1_ref, h2p_ref, ri_ref, rw_ref, cnt_ref, carry_ref, rect_ref, *, tm):
    @pl.when((pl.program_id(0) == 0) & (pl.program_id(1) == 0))
    def _():
        carry_ref[...] = jnp.zeros_like(carry_ref)

    mod = mod_ref[0]
    g1 = mod[2:3]
    sh2 = mod[3:4]
    sc2 = mod[4:5]
    d_half = wo_ref.shape[0] // 2
    mix = _dot(mixa_ref[0], wo_ref[0:d_half]) + _dot(yc_ref[0], wo_ref[d_half:2 * d_half])
    x1 = x_ref[0] + g1 * mix
    x1_ref[0] = x1
    ms = jnp.mean(x1 * x1, axis=-1, keepdims=True)
    h2 = ((x1 * lax.rsqrt(ms + EPS)) * n2w_ref[...]) * (1.0 + sc2) + sh2

    dh = h2.shape[1] // 2
    lo_w = pltpu.bitcast(h2[:, :dh].astype(BF16).astype(F32), U32)
    hi_w = pltpu.bitcast(h2[:, dh:].astype(BF16).astype(F32), U32)
    h2p_ref[0] = lax.shift_right_logical(lo_w, jnp.uint32(16)) | (hi_w & jnp.uint32(HI_MASK))

    logits = _dot3(h2, wrh_ref[...], wrl_ref[...]) + br_ref[...]
    lane = lax.broadcasted_iota(I32, (tm, LANES), 1)
    glog = jnp.where(lane < N_GROUPS, logits, NEG)
    gmax = jnp.max(glog, axis=1, keepdims=True)
    g_p = 1.0 / jnp.sum(jnp.exp(glog - gmax), axis=1, keepdims=True)
    gidx = jnp.min(jnp.where(glog == gmax, lane, LANES), axis=1, keepdims=True)
    e_lo = ROUTE_E0 + EXPERTS_PER_GROUP * gidx
    el = jnp.where(lane >= e_lo, jnp.where(lane < e_lo + EXPERTS_PER_GROUP, logits, NEG), NEG)
    m1 = jnp.max(el, axis=1, keepdims=True)
    i1 = jnp.min(jnp.where(el == m1, lane, LANES), axis=1, keepdims=True)
    el2 = jnp.where(lane == i1, NEG, el)
    m2 = jnp.max(el2, axis=1, keepdims=True)
    i2 = jnp.min(jnp.where(el2 == m2, lane, LANES), axis=1, keepdims=True)
    e2 = jnp.exp(m2 - m1)
    w1 = g_p / (1.0 + e2)
    w2 = g_p * e2 / (1.0 + e2)

    oh1 = lane == i1
    oh2 = lane == i2
    both = jnp.where(oh1, 1.0, jnp.where(oh2, 1.0, 0.0))
    base = carry_ref[0:1] + _dot(tri_ref[...], both.astype(BF16))
    r1 = jnp.sum(jnp.where(oh1, base, 0.0), axis=1, keepdims=True)
    r2 = jnp.sum(jnp.where(oh2, base, 0.0), axis=1, keepdims=True)
    carry_ref[...] = carry_ref[...] + jnp.sum(both, axis=0, keepdims=True)
    cnt_ref[...] = carry_ref[...]

    rec = jnp.where(lane == 0, (i1 - ROUTE_E0).astype(F32),
                    jnp.where(lane == 1, (i2 - ROUTE_E0).astype(F32),
                              jnp.where(lane == 2, r1, jnp.where(lane == 3, r2, 0.0))))
    rect_ref[...] = rec.T
    ri_ref[0, 0] = rect_ref[0:8, :].astype(I32)
    rw_ref[0] = jnp.where(lane == 0, w1, jnp.where(lane == 1, w2, 0.0))


def _outproj(mixa, yconv, w_out, x, mod, norm2_w, w_group, b_group, w_expert, b_expert):
    b, s, d = x.shape
    tm = TOK_TILE
    wr = jnp.pad(jnp.concatenate([w_group, w_expert], axis=1), ((0, 0), (0, LANES - N_GROUPS - N_EXPERTS)))
    wr_hi = wr.astype(BF16)
    wr_lo = (wr - wr_hi.astype(F32)).astype(BF16)
    br = jnp.pad(jnp.concatenate([b_group, b_expert]), (0, LANES - N_GROUPS - N_EXPERTS)).reshape(1, LANES)
    tri = jnp.asarray(np.tril(np.ones((tm, tm), np.float32), -1), dtype=BF16)
    full = lambda shape: pl.BlockSpec(shape, lambda bi, i: (0,) * len(shape))
    row = lambda w: pl.BlockSpec((1, tm, w), lambda bi, i: (bi, i, 0))
    return pl.pallas_call(
        functools.partial(_outproj_kernel, tm=tm),
        grid=(b, s // tm),
        in_specs=[row(D_ATTN), row(D_CONV), full((d, d)), row(d),
                  pl.BlockSpec((1, 6, d), lambda bi, i: (bi, 0, 0)),
                  full((1, d)), full((d, LANES)), full((d, LANES)), full((1, LANES)), full((tm, tm))],
        out_specs=[row(d), row(d // 2), pl.BlockSpec((1, 1, 8, tm), lambda bi, i: (bi, i, 0, 0)),
                   row(LANES), full((8, LANES))],
        out_shape=[jax.ShapeDtypeStruct((b, s, d), F32),
                   jax.ShapeDtypeStruct((b, s, d // 2), U32),
                   jax.ShapeDtypeStruct((b, s // tm, 8, tm), I32),
                   jax.ShapeDtypeStruct((b, s, LANES), F32),
                   jax.ShapeDtypeStruct((8, LANES), F32)],
        scratch_shapes=[pltpu.VMEM((8, LANES), F32), pltpu.VMEM((LANES, tm), F32)],
        compiler_params=pltpu.CompilerParams(
            dimension_semantics=("arbitrary", "arbitrary"), vmem_limit_bytes=VMEM_LIMIT),
        name="outproj",
    )(mixa, yconv, w_out.astype(BF16), x, mod, norm2_w.reshape(1, d), wr_hi, wr_lo, br, tri)


def _dest_kernel(ps_ref, ri_ref, d_ref):
    for t in range(ri_ref.shape[1]):
        e = ri_ref[0, t]
        acc = pltpu.roll(e, 6, 0)
        for x in range(N_EXPERTS):
            acc = jnp.where(e == x, acc + ps_ref[x], acc)
        d_ref[0, t] = acc


def _dest_rows(pad_start, ri_c):
    b, nt, _, tm = ri_c.shape
    return pl.pallas_call(
        _dest_kernel,
        grid_spec=pltpu.PrefetchScalarGridSpec(
            num_scalar_prefetch=1,
            grid=(b,),
            in_specs=[pl.BlockSpec((1, nt, 8, tm), lambda bi, ps: (bi, 0, 0, 0))],
            out_specs=pl.BlockSpec((1, nt, 8, tm), lambda bi, ps: (bi, 0, 0, 0))),
        out_shape=jax.ShapeDtypeStruct((b, nt, 8, tm), I32),
        name="dest_rows",
    )(pad_start, ri_c)


def _for_each_row(fn):
    for base in range(0, TOK_TILE, LANES):
        def group(g, carry, base=base):
            r0 = base + pl.multiple_of(g * ROW_UNROLL, ROW_UNROLL)
            for u in range(ROW_UNROLL):
                for k in range(2):
                    fn(r0 + u, k)
            return carry

        lax.fori_loop(0, LANES // ROW_UNROLL, group, 0)


def _dispatch_kernel(dest_ref, h2p_ref, xd_in_ref, xd_ref, sem):
    del xd_in_ref

    def row_copy(r, k):
        return pltpu.make_async_copy(h2p_ref.at[pl.ds(r, 1)], xd_ref.at[pl.ds(dest_ref[0, k, r], 1)], sem)

    _for_each_row(lambda r, k: row_copy(r, k).start())
    _for_each_row(lambda r, k: row_copy(r, k).wait())


def _dispatch(dest_t, h2p, n_rows):
    t, w = h2p.shape
    xd0 = jnp.zeros((n_rows, w), U32)
    return pl.pallas_call(
        _dispatch_kernel,
        grid=(t // TOK_TILE,),
        in_specs=[pl.BlockSpec((1, 8, TOK_TILE), lambda i: (i, 0, 0), memory_space=pltpu.SMEM),
                  pl.BlockSpec((TOK_TILE, w), lambda i: (i, 0)),
                  pl.BlockSpec(memory_space=pl.ANY)],
        out_specs=pl.BlockSpec(memory_space=pl.ANY),
        scratch_shapes=[pltpu.SemaphoreType.DMA(())],
        out_shape=jax.ShapeDtypeStruct((n_rows, w), U32),
        input_output_aliases={2: 0},
        compiler_params=pltpu.CompilerParams(
            dimension_semantics=("arbitrary",), vmem_limit_bytes=VMEM_LIMIT, has_side_effects=True),
        name="dispatch",
    )(dest_t, h2p, xd0)


def _experts_kernel(be_ref, nu_ref, xd_ref, w1_ref, w3_ref, w2_ref, y_ref, w1b_ref, w3b_ref, w2b_ref):
    i = pl.program_id(0)

    @pl.when((i == 0) | (be_ref[i] != be_ref[jnp.maximum(i - 1, 0)]))
    def _():
        w1b_ref[...] = w1_ref[0].astype(BF16)
        w3b_ref[...] = w3_ref[0].astype(BF16)
        w2b_ref[...] = w2_ref[0].astype(BF16)

    @pl.when(i < nu_ref[0])
    def _():
        word = xd_ref[...]
        xa = pltpu.bitcast(lax.shift_left(word, jnp.uint32(16)), F32).astype(BF16)
        xb = pltpu.bitcast(word & jnp.uint32(HI_MASK), F32).astype(BF16)
        dh = xa.shape[1]
        a = _dot(xa, w1b_ref[0:dh]) + _dot(xb, w1b_ref[dh:2 * dh])
        c = _dot(xa, w3b_ref[0:dh]) + _dot(xb, w3b_ref[dh:2 * dh])
        y_ref[...] = _dot((_silu(a) * c).astype(BF16), w2b_ref[...])

    @pl.when(i >= nu_ref[0])
    def _():
        y_ref[...] = jnp.zeros_like(y_ref)


def _experts(blk_expert, n_used, xd, w1, w3, w2):
    n_rows, dh = xd.shape
    _, d, de = w1.shape
    n_blocks = n_rows // MOE_BLOCK
    return pl.pallas_call(
        _experts_kernel,
        grid_spec=pltpu.PrefetchScalarGridSpec(
            num_scalar_prefetch=2,
            grid=(n_blocks,),
            in_specs=[pl.BlockSpec((MOE_BLOCK, dh), lambda i, be, nu: (i, 0)),
                      pl.BlockSpec((1, d, de), lambda i, be, nu: (be[i], 0, 0)),
                      pl.BlockSpec((1, d, de), lambda i, be, nu: (be[i], 0, 0)),
                      pl.BlockSpec((1, de, d), lambda i, be, nu: (be[i], 0, 0))],
            out_specs=pl.BlockSpec((MOE_BLOCK, d), lambda i, be, nu: (i, 0)),
            scratch_shapes=[pltpu.VMEM((d, de), BF16), pltpu.VMEM((d, de), BF16), pltpu.VMEM((de, d), BF16)]),
        out_shape=jax.ShapeDtypeStruct((n_rows, d), F32),
        compiler_params=pltpu.CompilerParams(
            dimension_semantics=("arbitrary",), vmem_limit_bytes=VMEM_LIMIT),
        name="experts",
    )(blk_expert, n_used, xd, w1, w3, w2)


def _combine_kernel(dest_ref, dnext_ref, rw_ref, x1_ref, mod_ref, y_ref, o_ref, buf_ref, sems):
    i = pl.program_id(0)
    slot = i % 2

    def gather(idx_ref, slot_, start):
        def one(r, k):
            cp = pltpu.make_async_copy(y_ref.at[pl.ds(idx_ref[0, k, r], 1)],
                                       buf_ref.at[slot_, k, pl.ds(r, 1)], sems.at[slot_])
            if start:
                cp.start()
            else:
                cp.wait()

        _for_each_row(one)

    @pl.when(i == 0)
    def _():
        gather(dest_ref, slot, True)

    @pl.when(i + 1 < pl.num_programs(0))
    def _():
        gather(dnext_ref, 1 - slot, True)

    gather(dest_ref, slot, False)
    w = rw_ref[...]
    g2 = mod_ref[0][5:6]
    o_ref[...] = x1_ref[...] + g2 * (w[:, 0:1] * buf_ref[slot, 0] + w[:, 1:2] * buf_ref[slot, 1])


def _combine(dest_t, rw, x1, mod, y_disp, seq):
    t, d = x1.shape
    tiles_per_seq = seq // TOK_TILE
    n_tiles = t // TOK_TILE
    return pl.pallas_call(
        _combine_kernel,
        grid=(n_tiles,),
        in_specs=[pl.BlockSpec((1, 8, TOK_TILE), lambda i: (i, 0, 0), memory_space=pltpu.SMEM),
                  pl.BlockSpec((1, 8, TOK_TILE), lambda i: (jnp.minimum(i + 1, n_tiles - 1), 0, 0),
                               memory_space=pltpu.SMEM),
                  pl.BlockSpec((TOK_TILE, LANES), lambda i: (i, 0)),
                  pl.BlockSpec((TOK_TILE, d), lambda i: (i, 0)),
                  pl.BlockSpec((1, 6, d), lambda i: (i // tiles_per_seq, 0, 0)),
                  pl.BlockSpec(memory_space=pl.ANY)],
        out_specs=pl.BlockSpec((TOK_TILE, d), lambda i: (i, 0)),
        scratch_shapes=[pltpu.VMEM((2, 2, TOK_TILE, d), F32), pltpu.SemaphoreType.DMA((2,))],
        out_shape=jax.ShapeDtypeStruct((t, d), F32),
        compiler_params=pltpu.CompilerParams(
            dimension_semantics=("arbitrary",), vmem_limit_bytes=VMEM_LIMIT),
        name="combine",
    )(dest_t, dest_t, rw, x1, mod, y_disp)


def _moe(x1, h2p, ri, rw, cnt, mod, w1, w3, w2):
    b, s, d = x1.shape
    t = b * s
    a = 2 * t
    counts = cnt[0, ROUTE_E0:ROUTE_E0 + N_EXPERTS].astype(I32)
    padded = (counts + MOE_BLOCK - 1) // MOE_BLOCK * MOE_BLOCK
    pad_end = jnp.cumsum(padded)
    pad_start = (pad_end - padded).astype(I32)
    n_blocks = (a + N_EXPERTS * (MOE_BLOCK - 1)) // MOE_BLOCK + 1
    blk_start = jnp.arange(n_blocks, dtype=I32) * MOE_BLOCK
    blk_expert = jnp.minimum(
        jnp.sum((pad_end[None, :] <= blk_start[:, None]).astype(I32), axis=1), N_EXPERTS - 1)
    n_used = (pad_end[-1:] // MOE_BLOCK).astype(I32)
    dest_t = _dest_rows(pad_start, ri).reshape(t // TOK_TILE, 8, TOK_TILE)
    xd = _dispatch(dest_t, h2p.reshape(t, d // 2), n_blocks * MOE_BLOCK)
    y_disp = _experts(blk_expert, n_used, xd, w1, w3, w2)
    out = _combine(dest_t, rw.reshape(t, LANES), x1.reshape(t, d), mod, y_disp, s)
    return out.reshape(b, s, d)


def kernel(x, c, w_ada, b_ada, norm1_w, w_in, q_norm_w, k_norm_w, cmp_pos_k, cmp_pos_v, cmp_k_w1, cmp_k_w2,
           cmp_v_w1, cmp_v_w2, conv_w, out_norm_w, w_out, rel_bias, norm2_w, w_group, b_group, w_expert,
           b_expert, w1, w3, w2):
    t_hi, t_lo, t_c = _bias_tables(rel_bias)
    for l in range(w_ada.shape[0]):
        mod = _ada(c, w_ada[l], b_ada[l])
        q, kvc, kaug, vaug, kwin, vwaug, gates, yconv = _inproj(
            x, mod, norm1_w[l], w_in[l], q_norm_w[l], k_norm_w[l], conv_w[l], out_norm_w[l])
        vaug, vscale = _to_f8(vaug)
        vwaug, vwscale = _to_f8(vwaug)
        rows = _kvc_rows(kvc)
        k_c = _compress(rows, 0, cmp_k_w1[l], cmp_k_w2[l], cmp_pos_k[l], k_norm_w[l, 0], True)
        v_c = _compress(rows, 1, cmp_v_w1[l], cmp_v_w2[l], cmp_pos_v[l], k_norm_w[l, 0], False)
        o_cmp, pen = _cmpsel(q, k_c, v_c, t_c)
        mixa = _slcwin(q, pen, kaug, vaug, vscale, kwin, vwaug, vwscale, o_cmp, gates, t_hi, t_lo,
                       out_norm_w[l])
        x1, h2p, ri, rw, cnt = _outproj(mixa, yconv, w_out[l], x, mod, norm2_w[l],
                                        w_group[l], b_group[l], w_expert[l], b_expert[l])
        x = _moe(x1, h2p, ri, rw, cnt, mod, w1[l], w3[l], w2[l])
    return x
```

```python
import functools
import math

import numpy as np
import jax
import jax.numpy as jnp
from jax import lax
from jax.experimental import pallas as pl
from jax.experimental.pallas import tpu as pltpu

F32 = jnp.float32
BF16 = jnp.bfloat16
F8 = jnp.float8_e4m3fn
P_SCALE = 256.0
F8_TARGET = 240.0
I32 = jnp.int32

HEAD_DIM = 64
N_HEADS = 8
N_KV = 2
HPG = N_HEADS // N_KV
D_ATTN = N_HEADS * HEAD_DIM
D_CONV = 512
KV_DIM = N_KV * HEAD_DIM
CMP_LEN = 32
CMP_STRIDE = 16
CMP_HIDDEN = 256
SLC_LEN = 64
N_SELECT = 16
WINDOW = 512
N_BUCKETS = 32
MAX_DISTANCE = 128
N_GROUPS = 8
EXPERTS_PER_GROUP = 8
N_EXPERTS = N_GROUPS * EXPERTS_PER_GROUP
MOE_BLOCK = 512
FORCED_SCORE = 1e4
EPS = 1e-6

LANES = 128
QB = 512
ROWS_Q = HPG * QB
CMP_CHUNK = 256
CMP_DIST0 = CMP_STRIDE * CMP_CHUNK - (CMP_LEN - 1) - QB
IMP_PAD = CMP_CHUNK // 4
OV_ROWS = IMP_PAD + 8
FAR_GROUP = 4
PEN_BLOCKS = 128
HALF_KEYS = PEN_BLOCKS * SLC_LEN
NEG = -1e30
PICKED = -3e38
ROUTE_E0 = N_GROUPS
U32 = jnp.uint32
HI_MASK = 0xFFFF0000
TOK_TILE = 512
ROW_UNROLL = 8
VMEM_LIMIT = 56 * 1024 * 1024


def _dot(a, b):
    return jnp.dot(a, b, preferred_element_type=F32)


def _dot_nt(a, b):
    return lax.dot_general(a, b, (((1,), (1,)), ((), ())), preferred_element_type=F32)


def _hi_lo(a):
    hi = a.astype(BF16)
    lo = (a - hi.astype(F32)).astype(BF16)
    return hi, lo


def _dot3(a, b_hi, b_lo):
    a_hi, a_lo = _hi_lo(a)
    return _dot(a_hi, b_hi) + (_dot(a_lo, b_hi) + _dot(a_hi, b_lo))


def _group_sumsq(v, gm):
    hi, lo = _hi_lo(v * v)
    return _dot(hi, gm) + _dot(lo, gm)


def _silu(v):
    return v * jax.nn.sigmoid(v)


def _pack_halves(v):
    n = v.shape[1] // 2
    lo_w = pltpu.bitcast(v[:, :n].astype(BF16).astype(F32), U32)
    hi_w = pltpu.bitcast(v[:, n:].astype(BF16).astype(F32), U32)
    return lax.shift_right_logical(lo_w, jnp.uint32(16)) | (hi_w & jnp.uint32(HI_MASK))


def _unpack_halves(word):
    left = pltpu.bitcast(lax.shift_left(word, jnp.uint32(16)), F32)
    right = pltpu.bitcast(word & jnp.uint32(HI_MASK), F32)
    return left, right


def _ada_kernel(c_ref, w_ref, b_ref, o_ref):
    cond = _silu(c_ref[...])
    w_hi, w_lo = _hi_lo(w_ref[...])
    o_ref[...] = _dot3(cond, w_hi, w_lo) + b_ref[...]


def _ada(c, w_ada, b_ada):
    b, d = c.shape
    n = w_ada.shape[1]
    tn = n // 4
    c8 = jnp.pad(c, ((0, 8 - b), (0, 0)))
    out = pl.pallas_call(
        _ada_kernel,
        grid=(n // tn,),
        in_specs=[pl.BlockSpec((8, d), lambda i: (0, 0)),
                  pl.BlockSpec((d, tn), lambda i: (0, i)),
                  pl.BlockSpec((1, tn), lambda i: (0, i))],
        out_specs=pl.BlockSpec((8, tn), lambda i: (0, i)),
        out_shape=jax.ShapeDtypeStruct((8, n), F32),
        compiler_params=pltpu.CompilerParams(vmem_limit_bytes=VMEM_LIMIT),
        name="ada",
    )(c8, w_ada, b_ada.reshape(1, n))
    return out[:b].reshape(b, 6, d)


def _inproj_kernel(x_ref, xh_ref, mod_ref, n1w_ref, wq_ref, wkv_ref, wg_ref, wbcu_ref,
                   qnw_ref, knw_ref, convw_ref, onw_ref, gm_ref,
                   q_ref, kvc_ref, kaug_ref, vaug_ref, kwin_ref, vwaug_ref, gates_ref, yconv_ref,
                   ext_ref, *, tm):
    i = pl.program_id(1)
    mod = mod_ref[0]
    sh1 = mod[0:1]
    sc1 = mod[1:2]

    def norm_mod(v):
        ms = jnp.mean(v * v, axis=-1, keepdims=True)
        y = (v * lax.rsqrt(ms + EPS)) * n1w_ref[...]
        return (y * (1.0 + sc1) + sh1).astype(BF16)

    h = norm_mod(x_ref[0])
    gm = gm_ref[...]
    gm_kv = gm_ref[0:KV_DIM, 0:KV_DIM]

    q = _dot(h, wq_ref[...])
    qn = q * lax.rsqrt(_group_sumsq(q, gm) * (1.0 / HEAD_DIM) + EPS) * qnw_ref[...]
    qn = (qn * (HEAD_DIM ** -0.5)).astype(BF16)
    for hd in range(N_HEADS):
        q_ref[0, hd] = qn[:, hd * HEAD_DIM:(hd + 1) * HEAD_DIM]

    kv = _dot(h, wkv_ref[...])
    kvc_ref[0] = kv[:, 0:2 * KV_DIM].astype(BF16)
    ks = kv[:, 2 * KV_DIM:3 * KV_DIM]
    vs = kv[:, 3 * KV_DIM:4 * KV_DIM]
    kw = kv[:, 4 * KV_DIM:5 * KV_DIM]
    vw = kv[:, 5 * KV_DIM:6 * KV_DIM]
    ksn = ks * lax.rsqrt(_group_sumsq(ks, gm_kv) * (1.0 / HEAD_DIM) + EPS) * knw_ref[0:1]
    kwn = kw * lax.rsqrt(_group_sumsq(kw, gm_kv) * (1.0 / HEAD_DIM) + EPS) * knw_ref[1:2]

    lane = lax.broadcasted_iota(I32, (tm, LANES), 1)
    row = lax.broadcasted_iota(I32, (tm, LANES), 0)
    blk = ((i * tm + row) // SLC_LEN) % PEN_BLOCKS
    onehot = jnp.where(lane == blk, 1.0, 0.0).astype(BF16)
    ones_col = jnp.where(lax.broadcasted_iota(I32, (tm, HEAD_DIM), 1) == 0, 1.0, 0.0).astype(BF16)
    zeros64 = jnp.zeros((tm, HEAD_DIM), BF16)
    for g in range(N_KV):
        sl = slice(g * HEAD_DIM, (g + 1) * HEAD_DIM)
        kaug_ref[0, g, :, 0:LANES] = onehot
        kaug_ref[0, g, :, LANES:LANES + HEAD_DIM] = ksn[:, sl].astype(BF16)
        kaug_ref[0, g, :, LANES + HEAD_DIM:2 * LANES] = zeros64
        vaug_ref[0, g, :, 0:HEAD_DIM] = vs[:, sl].astype(BF16)
        vaug_ref[0, g, :, HEAD_DIM:LANES] = ones_col
        kwin_ref[0, g] = kwn[:, sl].astype(BF16)
        vwaug_ref[0, g, :, 0:HEAD_DIM] = vw[:, sl].astype(BF16)
        vwaug_ref[0, g, :, HEAD_DIM:LANES] = ones_col

    gates_ref[0] = jax.nn.sigmoid(_dot(h, wg_ref[...]))

    bcu = _dot(h, wbcu_ref[...])
    bg = bcu[:, 0:D_CONV]
    cu = bcu[:, D_CONV:2 * D_CONV] * bcu[:, 2 * D_CONV:3 * D_CONV]
    hh = norm_mod(xh_ref[0])
    cuh = _dot(hh, wbcu_ref[:, D_CONV:2 * D_CONV]) * _dot(hh, wbcu_ref[:, 2 * D_CONV:3 * D_CONV])
    cuh = jnp.where(i > 0, cuh, 0.0)
    ext_ref[0:8] = cuh
    ext_ref[8:8 + tm] = cu
    cw = convw_ref[...]
    y = bg * (cw[0:1] * ext_ref[6:6 + tm] + cw[1:2] * ext_ref[7:7 + tm] + cw[2:3] * cu)
    yn = y * lax.rsqrt(_group_sumsq(y, gm) * (1.0 / HEAD_DIM) + EPS) * onw_ref[...]
    yconv_ref[0] = yn.astype(BF16)


def _inproj(x, mod, norm1_w, w_in, q_norm_w, k_norm_w, conv_w, out_norm_w, tm=512):
    b, s, d = x.shape
    o_kv = D_ATTN
    o_g = o_kv + 6 * KV_DIM
    o_b = o_g + 3 * N_HEADS
    wq = w_in[:, 0:o_kv].astype(BF16)
    wkv = w_in[:, o_kv:o_g].astype(BF16)
    wg = jnp.pad(w_in[:, o_g:o_b].reshape(d, N_KV, 3 * HPG),
                 ((0, 0), (0, 0), (0, LANES - 3 * HPG))).reshape(d, N_KV * LANES).astype(BF16)
    wbcu = w_in[:, o_b:].astype(BF16)
    qnw = jnp.tile(q_norm_w, N_HEADS).reshape(1, D_ATTN)
    knw = jnp.stack([jnp.tile(k_norm_w[1], N_KV), jnp.tile(k_norm_w[2], N_KV)])
    onw = out_norm_w[D_ATTN:].reshape(1, D_CONV)
    gidx = np.arange(D_ATTN) // HEAD_DIM
    gm = jnp.asarray((gidx[:, None] == gidx[None, :]).astype(np.float32), dtype=BF16)

    full = lambda shape: pl.BlockSpec(shape, lambda bi, i: (0,) * len(shape))
    outs = pl.pallas_call(
        functools.partial(_inproj_kernel, tm=tm),
        grid=(b, s // tm),
        in_specs=[pl.BlockSpec((1, tm, d), lambda bi, i: (bi, i, 0)),
                  pl.BlockSpec((1, 8, d), lambda bi, i: (bi, jnp.maximum(i * (tm // 8) - 1, 0), 0)),
                  pl.BlockSpec((1, 6, d), lambda bi, i: (bi, 0, 0)),
                  full((1, d)), full(wq.shape), full(wkv.shape), full(wg.shape), full(wbcu.shape),
                  full(qnw.shape), full(knw.shape), full(conv_w.shape), full(onw.shape), full(gm.shape)],
        out_specs=[pl.BlockSpec((1, N_HEADS, tm, HEAD_DIM), lambda bi, i: (bi, 0, i, 0)),
                   pl.BlockSpec((1, tm, 2 * KV_DIM), lambda bi, i: (bi, i, 0)),
                   pl.BlockSpec((1, N_KV, tm, 2 * LANES), lambda bi, i: (bi, 0, i, 0)),
                   pl.BlockSpec((1, N_KV, tm, LANES), lambda bi, i: (bi, 0, i, 0)),
                   pl.BlockSpec((1, N_KV, tm, HEAD_DIM), lambda bi, i: (bi, 0, i, 0)),
                   pl.BlockSpec((1, N_KV, tm, LANES), lambda bi, i: (bi, 0, i, 0)),
                   pl.BlockSpec((1, tm, N_KV * LANES), lambda bi, i: (bi, i, 0)),
                   pl.BlockSpec((1, tm, D_CONV), lambda bi, i: (bi, i, 0))],
        out_shape=[jax.ShapeDtypeStruct((b, N_HEADS, s, HEAD_DIM), BF16),
                   jax.ShapeDtypeStruct((b, s, 2 * KV_DIM), BF16),
                   jax.ShapeDtypeStruct((b, N_KV, s, 2 * LANES), BF16),
                   jax.ShapeDtypeStruct((b, N_KV, s, LANES), BF16),
                   jax.ShapeDtypeStruct((b, N_KV, s, HEAD_DIM), BF16),
                   jax.ShapeDtypeStruct((b, N_KV, s, LANES), BF16),
                   jax.ShapeDtypeStruct((b, s, N_KV * LANES), F32),
                   jax.ShapeDtypeStruct((b, s, D_CONV), BF16)],
        scratch_shapes=[pltpu.VMEM((tm + 8, D_CONV), F32)],
        compiler_params=pltpu.CompilerParams(
            dimension_semantics=("parallel", "arbitrary"), vmem_limit_bytes=VMEM_LIMIT),
        name="inproj",
    )(x, x, mod, norm1_w.reshape(1, d), wq, wkv, wg, wbcu, qnw, knw, conv_w, onw, gm)
    return outs


def _compress_kernel(r_ref, w1_ref, w2_ref, pos_ref, nw_ref, o_ref, *, norm):
    r = r_ref[0, 0, 0]
    nr = r.shape[0]
    half = (CMP_LEN // 2) * HEAD_DIM
    w1a = w1_ref[0:half, :]
    w1b = w1_ref[half:2 * half, :]
    pos = pos_ref[...]
    cst = (_dot(jnp.broadcast_to(pos[0:1], (8, half)).astype(BF16), w1a)
           + _dot(jnp.broadcast_to(pos[1:2], (8, half)).astype(BF16), w1b))[0:1]
    upper = pltpu.roll(_dot(r, w1b), nr - 1, 0)
    hid = _dot(r, w1a) + upper + cst
    out = _dot(_silu(hid).astype(BF16), w2_ref[...])
    if norm:
        ms = jnp.mean(out * out, axis=-1, keepdims=True)
        out = out * lax.rsqrt(ms + EPS) * nw_ref[...]
        o_ref[0, 0] = out.astype(BF16)
    else:
        o_ref[0, 0, :, 0:HEAD_DIM] = out.astype(BF16)
        o_ref[0, 0, :, HEAD_DIM:LANES] = jnp.where(
            lax.broadcasted_iota(I32, (nr, HEAD_DIM), 1) == 0, 1.0, 0.0).astype(BF16)


def _compress(rows, kv_index, w1, w2, pos_emb, norm_w, norm):
    b, _, g, nr, width = rows.shape
    out_w = HEAD_DIM if norm else LANES
    return pl.pallas_call(
        functools.partial(_compress_kernel, norm=norm),
        grid=(b, g),
        in_specs=[pl.BlockSpec((1, 1, 1, nr, width), lambda bi, gi: (bi, kv_index, gi, 0, 0)),
                  pl.BlockSpec(w1.shape, lambda bi, gi: (0, 0)),
                  pl.BlockSpec(w2.shape, lambda bi, gi: (0, 0)),
                  pl.BlockSpec((2, width), lambda bi, gi: (0, 0)),
                  pl.BlockSpec((1, HEAD_DIM), lambda bi, gi: (0, 0))],
        out_specs=pl.BlockSpec((1, 1, nr, out_w), lambda bi, gi: (bi, gi, 0, 0)),
        out_shape=jax.ShapeDtypeStruct((b, g, nr, out_w), BF16),
        compiler_params=pltpu.CompilerParams(vmem_limit_bytes=VMEM_LIMIT),
        name="compress_k" if norm else "compress_v",
    )(rows, w1.astype(BF16), w2.astype(BF16), pos_emb.reshape(2, width), norm_w.reshape(1, HEAD_DIM))


def _kvc_rows(kvc):
    b, s, _ = kvc.shape
    r = kvc.reshape(b, s, 2, N_KV, HEAD_DIM).transpose(0, 2, 3, 1, 4)
    return r.reshape(b, 2, N_KV, s // CMP_STRIDE, CMP_STRIDE * HEAD_DIM)


def _bucket_np(dist):
    n = np.maximum(dist, 0)
    max_exact = N_BUCKETS // 2
    nf = np.maximum(n, 1).astype(np.float32)
    large = max_exact + (np.log(nf / np.float32(max_exact)) / np.float32(math.log(MAX_DISTANCE / max_exact))
                         * np.float32(N_BUCKETS - max_exact)).astype(np.int32)
    large = np.minimum(large, N_BUCKETS - 1)
    return np.where(n < max_exact, n, large)


_BUCKET_START = [int(np.argmax(_bucket_np(np.arange(MAX_DISTANCE + 1)) >= k)) for k in range(N_BUCKETS)]


def _bias_tables_kernel(rbt_ref, thi_ref, tlo_ref, tc_ref):
    head = pl.program_id(0) * HPG + pl.program_id(1)
    far = rbt_ref[head, N_BUCKETS - 1]

    def table(dist):
        v = jnp.zeros(dist.shape, F32)
        for k in range(N_BUCKETS - 2, -1, -1):
            v = jnp.where(dist < _BUCKET_START[k + 1], rbt_ref[head, k] - far, v)
        return jnp.where(dist >= 0, v, NEG)

    def dist(shape, row_mul, col_mul, off):
        return (lax.broadcasted_iota(I32, shape, 0) * row_mul
                + lax.broadcasted_iota(I32, shape, 1) * col_mul + off)

    thi_ref[0] = table(dist((QB, QB), 1, -1, 0))
    tlo_ref[0] = table(dist((QB, LANES), 1, -1, LANES))
    tc_ref[0] = table(dist((QB, CMP_CHUNK), 1, -CMP_STRIDE, CMP_DIST0))


def _bias_tables(rel_bias):
    return pl.pallas_call(
        _bias_tables_kernel,
        grid=(N_KV, HPG),
        in_specs=[pl.BlockSpec(memory_space=pltpu.SMEM)],
        out_specs=[pl.BlockSpec((1, QB, QB), lambda g, h: (g, h, 0)),
                   pl.BlockSpec((1, QB, LANES), lambda g, h: (g, h, 0)),
                   pl.BlockSpec((1, QB, CMP_CHUNK), lambda g, h: (g, h, 0))],
        out_shape=[jax.ShapeDtypeStruct((N_KV, ROWS_Q, QB), F32),
                   jax.ShapeDtypeStruct((N_KV, ROWS_Q, LANES), F32),
                   jax.ShapeDtypeStruct((N_KV, ROWS_Q, CMP_CHUNK), F32)],
        name="bias_tables",
    )(rel_bias.T)


def _cmpsel_kernel(q_ref, kc_ref, vc_ref, tab_ref, ovt_ref, ocmp_ref, pen_ref,
                   m_ref, acc_ref, imp_ref, score_ref, invt_ref, *, nsbp, n_sel):
    j = pl.program_id(2)
    q4 = q_ref[0].reshape(ROWS_Q, HEAD_DIM)
    nq16 = QB // CMP_STRIDE
    n_chunks = (nq16 * (j + 1) - 1) // CMP_CHUNK + 1
    ni = lax.broadcasted_iota(I32, (1, CMP_CHUNK), 1)

    def chunk_start(m):
        return pl.multiple_of(nq16 * (j + 1) - CMP_CHUNK * m, nq16)

    def logits(m):
        n0p = chunk_start(m)
        s = _dot_nt(q4, kc_ref[0, 0, pl.ds(n0p, CMP_CHUNK), :])
        return jnp.where(ni + n0p >= CMP_CHUNK, s, NEG)

    s0 = logits(0) + tab_ref[0]
    m_ref[...] = s0

    def stats(m, carry):
        m_ref[...] = jnp.maximum(m_ref[...], logits(m))
        return carry

    lax.fori_loop(1, n_chunks, stats, 0)
    m_fin = jnp.max(m_ref[...], axis=1, keepdims=True)
    m_ref[...] = jnp.broadcast_to(m_fin, (ROWS_Q, CMP_CHUNK))
    acc_ref[...] = jnp.zeros_like(acc_ref)
    imp_ref[...] = jnp.zeros_like(imp_ref)

    def accumulate(m, s):
        n0p = chunk_start(m)
        p = jnp.exp(s - m_ref[...]).astype(BF16)
        acc_ref[...] += _dot(p, vc_ref[0, 0, pl.ds(n0p, CMP_CHUNK), :])
        r0 = pl.multiple_of(n0p // 4, 8)
        for hd in range(HPG):
            imp_ref[hd, pl.ds(r0, OV_ROWS), :] += _dot_nt(ovt_ref[...], p[hd * QB:(hd + 1) * QB])

    accumulate(0, s0)

    def second(m, carry):
        accumulate(m, logits(m))
        return carry

    lax.fori_loop(1, n_chunks, second, 0)
    acc = acc_ref[...]
    inv = jnp.where(m_fin > 0.5 * NEG, 1.0 / acc[:, HEAD_DIM:HEAD_DIM + 1], 0.0)
    ocmp_ref[0, 0] = (acc[:, 0:HEAD_DIM] * inv).reshape(HPG, QB, HEAD_DIM)

    invt_ref[...] = jnp.broadcast_to(inv, (ROWS_Q, LANES)).T
    imp = jnp.zeros((nsbp, QB), F32)
    for hd in range(HPG):
        imp = imp + imp_ref[hd, IMP_PAD:IMP_PAD + nsbp, :] * invt_ref[0:1, hd * QB:(hd + 1) * QB]
    srow = lax.broadcasted_iota(I32, (nsbp, QB), 0)
    qcol = lax.broadcasted_iota(I32, (nsbp, QB), 1)
    cur = (QB // SLC_LEN) * j + qcol // SLC_LEN
    forced = (srow == 0) | (srow == cur) | (srow == cur - 1)
    score_ref[...] = jnp.where(forced, -1.0, jnp.where(srow <= cur, imp, -1.0))

    quarter = nsbp // 4
    case = ((QB // SLC_LEN) * (j + 1) - 1) // quarter
    for k in range(4):
        rows = quarter * (k + 1)

        @pl.when(case == k)
        def _(rows=rows):
            srow_k = lax.broadcasted_iota(I32, (rows, QB), 0)

            def pick(r, carry):
                sc = score_ref[0:rows]
                mx = jnp.max(sc, axis=0, keepdims=True)
                idx = jnp.min(jnp.where(sc == mx, srow_k, nsbp), axis=0, keepdims=True)
                score_ref[0:rows] = jnp.where(srow_k == idx, PICKED, sc)
                return carry

            lax.fori_loop(0, max(n_sel - 3, 0), pick, 0)
    pen_other = jnp.where(srow <= cur, jnp.where(score_ref[...] == PICKED, 0.0, NEG), NEG)
    pen_t = jnp.where(forced, 0.0, pen_other).T
    for hf in range(nsbp // PEN_BLOCKS):
        pen_ref[0, 0, hf] = pen_t[:, hf * PEN_BLOCKS:(hf + 1) * PEN_BLOCKS].astype(BF16)


def _overlap_t():
    n = np.arange(CMP_CHUNK)
    s = np.arange(OV_ROWS)
    first = (n * CMP_STRIDE) // SLC_LEN
    last = (n * CMP_STRIDE + CMP_LEN - 1) // SLC_LEN
    ov = (s[:, None] >= first[None, :]) & (s[:, None] <= last[None, :])
    return jnp.asarray(ov.astype(np.float32), dtype=BF16)


def _cmpsel(q, k_c, v_c, tab):
    b, _, s, _ = q.shape
    nr = k_c.shape[2]
    nq = s // QB
    nsb = s // SLC_LEN
    nh = -(-nsb // PEN_BLOCKS)
    nsbp = nh * PEN_BLOCKS
    n_sel = min(N_SELECT, nsb)
    pad = ((0, 0), (0, 0), (CMP_CHUNK, 0), (0, 0))
    kc_p = jnp.pad(k_c, pad)
    vc_p = jnp.pad(v_c, pad)
    ovt = _overlap_t()
    return pl.pallas_call(
        functools.partial(_cmpsel_kernel, nsbp=nsbp, n_sel=n_sel),
        grid=(b, N_KV, nq),
        in_specs=[pl.BlockSpec((1, HPG, QB, HEAD_DIM), lambda bi, g, j: (bi, g, j, 0)),
                  pl.BlockSpec((1, 1, nr + CMP_CHUNK, HEAD_DIM), lambda bi, g, j: (bi, g, 0, 0)),
                  pl.BlockSpec((1, 1, nr + CMP_CHUNK, LANES), lambda bi, g, j: (bi, g, 0, 0)),
                  pl.BlockSpec((1, ROWS_Q, CMP_CHUNK), lambda bi, g, j: (g, 0, 0)),
                  pl.BlockSpec(ovt.shape, lambda bi, g, j: (0, 0))],
        out_specs=[pl.BlockSpec((1, 1, HPG, QB, HEAD_DIM), lambda bi, g, j: (bi, g, 0, j, 0)),
                   pl.BlockSpec((1, 1, nh, QB, PEN_BLOCKS), lambda bi, g, j: (bi, g, 0, j, 0))],
        out_shape=[jax.ShapeDtypeStruct((b, N_KV, HPG, s, HEAD_DIM), F32),
                   jax.ShapeDtypeStruct((b, N_KV, nh, s, PEN_BLOCKS), BF16)],
        scratch_shapes=[pltpu.VMEM((ROWS_Q, CMP_CHUNK), F32),
                        pltpu.VMEM((ROWS_Q, LANES), F32),
                        pltpu.VMEM((HPG, IMP_PAD + nsbp + 8, QB), F32),
                        pltpu.VMEM((nsbp, QB), F32),
                        pltpu.VMEM((LANES, ROWS_Q), F32)],
        compiler_params=pltpu.CompilerParams(
            dimension_semantics=("parallel", "parallel", "arbitrary"), vmem_limit_bytes=VMEM_LIMIT),
        name="cmpsel",
    )(q, kc_p, vc_p, tab, ovt)


def _to_f8_kernel(v_ref, o_ref, sc_ref):
    v = v_ref[0, 0].astype(F32)
    lane = lax.broadcasted_iota(I32, (1, LANES), 1)
    cmax = jnp.max(jnp.abs(v), axis=0, keepdims=True)
    scale = jnp.where(lane < HEAD_DIM, jnp.maximum(cmax, 1e-30) * (1.0 / F8_TARGET), 1.0)
    o_ref[0, 0] = (v / scale).astype(F8)
    sc_ref[0, 0] = jnp.broadcast_to(scale, (8, LANES))


def _to_f8(vaug):
    b, g, s, w = vaug.shape
    return pl.pallas_call(
        _to_f8_kernel,
        grid=(b, g),
        in_specs=[pl.BlockSpec((1, 1, s, w), lambda bi, gi: (bi, gi, 0, 0))],
        out_specs=[pl.BlockSpec((1, 1, s, w), lambda bi, gi: (bi, gi, 0, 0)),
                   pl.BlockSpec((1, 1, 8, w), lambda bi, gi: (bi, gi, 0, 0))],
        out_shape=[jax.ShapeDtypeStruct((b, g, s, w), F8), jax.ShapeDtypeStruct((b, g, 8, w), F32)],
        compiler_params=pltpu.CompilerParams(vmem_limit_bytes=VMEM_LIMIT),
        name="to_f8",
    )(vaug)


def _slcwin_kernel(q_ref, pen_ref, kaug_ref, vaug_ref, vsc_ref, kwp_ref, kwc_ref, vwp_ref, vwc_ref,
                   vwsc_ref, thi_ref, tlo_ref, ocmp_ref, gates_ref, onw_ref, o_ref,
                   lhs_ref, m_ref, acc_ref):
    j = pl.program_id(2)
    q4 = q_ref[0].reshape(ROWS_Q, HEAD_DIM)
    near = QB - LANES

    lhs_ref[:, LANES:LANES + HEAD_DIM] = q4
    lhs_ref[:, LANES + HEAD_DIM:2 * LANES] = jnp.zeros((ROWS_Q, HEAD_DIM), BF16)

    def set_pen(half):
        p = pen_ref[0, 0, half]
        for hd in range(HPG):
            lhs_ref[hd * QB:(hd + 1) * QB, 0:LANES] = p

    def reset():
        m_ref[...] = jnp.full_like(m_ref, NEG)
        acc_ref[...] = jnp.zeros_like(acc_ref)

    def update(s, v):
        sb = s.astype(BF16)
        m_prev = m_ref[...]
        m_new = jnp.maximum(m_prev, jnp.max(sb, axis=1, keepdims=True).astype(F32))
        x = sb - jnp.concatenate([m_new.astype(BF16)] * (s.shape[1] // LANES), axis=1)
        p = (jnp.exp(x) * P_SCALE).astype(F8)
        acc_ref[...] = acc_ref[...] * jnp.exp(m_prev - m_new) + _dot(p, v)
        m_ref[...] = m_new

    def add_near_bias(s):
        return jnp.concatenate([s[:, :near], s[:, near:] + tlo_ref[0]], axis=1)

    def result(scale_ref):
        acc = acc_ref[...]
        return acc[:, 0:HEAD_DIM] * scale_ref[0, 0, 0:1, 0:HEAD_DIM] / acc[:, HEAD_DIM:HEAD_DIM + 1]

    def slc_chunk(c, bias):
        k0 = pl.multiple_of(c * QB, QB)
        s = _dot_nt(lhs_ref[...], kaug_ref[0, 0, pl.ds(k0, QB), :])
        update(bias(s), vaug_ref[0, 0, pl.ds(k0, QB), :])

    chunks_per_half = HALF_KEYS // QB
    assert chunks_per_half % FAR_GROUP == 0
    reset()
    half_hi = j // chunks_per_half
    set_pen(half_hi)
    slc_chunk(j, lambda s: s + thi_ref[0])

    @pl.when(j > 0)
    def _():
        half_lo = (j - 1) // chunks_per_half

        @pl.when(half_lo != half_hi)
        def _():
            set_pen(half_lo)

        slc_chunk(j - 1, add_near_bias)

    n_far = jnp.maximum(j - 1, 0)

    def enter_half(c):
        @pl.when(c % chunks_per_half == 0)
        def _():
            set_pen(c // chunks_per_half)

    def far_group(i4, carry):
        c = FAR_GROUP * i4
        enter_half(c)
        for u in range(FAR_GROUP):
            slc_chunk(c + u, lambda s: s)
        return carry

    n_groups = n_far // FAR_GROUP
    lax.fori_loop(0, n_groups, far_group, 0)

    def far_single(c, carry):
        enter_half(c)
        slc_chunk(c, lambda s: s)
        return carry

    lax.fori_loop(n_groups * FAR_GROUP, n_far, far_single, 0)

    o_slc = result(vsc_ref)

    reset()
    update(_dot_nt(q4, kwc_ref[0, 0]) + thi_ref[0], vwc_ref[0, 0])

    @pl.when(j > 0)
    def _():
        s = _dot_nt(q4, kwp_ref[0, 0])
        qi = lax.broadcasted_iota(I32, (ROWS_Q, QB), 0) & (QB - 1)
        kk = lax.broadcasted_iota(I32, (ROWS_Q, QB), 1)
        s = jnp.where(kk > qi, s, NEG)
        update(add_near_bias(s), vwp_ref[0, 0])

    o_win = result(vwsc_ref)

    o_cmp = ocmp_ref[0, 0].reshape(ROWS_Q, HEAD_DIM)
    gt = gates_ref[0]
    outs = []
    for hd in range(HPG):
        rs = slice(hd * QB, (hd + 1) * QB)
        o = (gt[:, 3 * hd:3 * hd + 1] * o_cmp[rs] + gt[:, 3 * hd + 1:3 * hd + 2] * o_slc[rs]
             + gt[:, 3 * hd + 2:3 * hd + 3] * o_win[rs])
        ms = jnp.mean(o * o, axis=-1, keepdims=True)
        outs.append((o * lax.rsqrt(ms + EPS) * onw_ref[0, hd:hd + 1, :]).astype(BF16))
    o_ref[0] = jnp.concatenate(outs, axis=1)


def _slcwin(q, pen, kaug, vaug, vscale, kwin, vwaug, vwscale, o_cmp, gates, t_hi, t_lo, out_norm_w):
    assert QB == WINDOW
    b, _, s, _ = q.shape
    nq = s // QB
    nh = pen.shape[2]
    onw = out_norm_w[:D_ATTN].reshape(N_KV, HPG, HEAD_DIM)
    prev = lambda bi, g, j: (bi, g, jnp.maximum(j - 1, 0), 0)
    curr = lambda bi, g, j: (bi, g, j, 0)
    per_group = lambda bi, g, j: (bi, g, 0, 0)
    return pl.pallas_call(
        _slcwin_kernel,
        grid=(b, N_KV, nq),
        in_specs=[pl.BlockSpec((1, HPG, QB, HEAD_DIM), lambda bi, g, j: (bi, g, j, 0)),
                  pl.BlockSpec((1, 1, nh, QB, PEN_BLOCKS), lambda bi, g, j: (bi, g, 0, j, 0)),
                  pl.BlockSpec((1, 1, s, 2 * LANES), lambda bi, g, j: (bi, g, 0, 0)),
                  pl.BlockSpec((1, 1, s, LANES), lambda bi, g, j: (bi, g, 0, 0)),
                  pl.BlockSpec((1, 1, 8, LANES), per_group),
                  pl.BlockSpec((1, 1, QB, HEAD_DIM), prev),
                  pl.BlockSpec((1, 1, QB, HEAD_DIM), curr),
                  pl.BlockSpec((1, 1, QB, LANES), prev),
                  pl.BlockSpec((1, 1, QB, LANES), curr),
                  pl.BlockSpec((1, 1, 8, LANES), per_group),
                  pl.BlockSpec((1, ROWS_Q, QB), lambda bi, g, j: (g, 0, 0)),
                  pl.BlockSpec((1, ROWS_Q, LANES), lambda bi, g, j: (g, 0, 0)),
                  pl.BlockSpec((1, 1, HPG, QB, HEAD_DIM), lambda bi, g, j: (bi, g, 0, j, 0)),
                  pl.BlockSpec((1, QB, LANES), lambda bi, g, j: (bi, j, g)),
                  pl.BlockSpec((1, HPG, HEAD_DIM), lambda bi, g, j: (g, 0, 0))],
        out_specs=pl.BlockSpec((1, QB, HPG * HEAD_DIM), lambda bi, g, j: (bi, j, g)),
        out_shape=jax.ShapeDtypeStruct((b, s, D_ATTN), BF16),
        scratch_shapes=[pltpu.VMEM((ROWS_Q, 2 * LANES), BF16),
                        pltpu.VMEM((ROWS_Q, LANES), F32),
                        pltpu.VMEM((ROWS_Q, LANES), F32)],
        compiler_params=pltpu.CompilerParams(
            dimension_semantics=("parallel", "parallel", "arbitrary"), vmem_limit_bytes=VMEM_LIMIT),
        name="slcwin",
    )(q, pen, kaug, vaug, vscale, kwin, kwin, vwaug, vwaug, vwscale, t_hi, t_lo, o_cmp, gates, onw)


def _outproj_kernel(mixa_ref, yc_ref, wo_ref, x_ref, mod_ref, n2w_ref, wrh_ref, wrl_ref, br_ref, tri_ref,
                    x1_ref, h2p_ref, ri_ref, rw_ref, cnt_ref, carry_ref, rect_ref, *, tm):
    @pl.when((pl.program_id(0) == 0) & (pl.program_id(1) == 0))
    def _():
        carry_ref[...] = jnp.zeros_like(carry_ref)

    mod = mod_ref[0]
    g1 = mod[2:3]
    sh2 = mod[3:4]
    sc2 = mod[4:5]
    d_half = wo_ref.shape[0] // 2
    mix = _dot(mixa_ref[0], wo_ref[0:d_half]) + _dot(yc_ref[0], wo_ref[d_half:2 * d_half])
    x1 = x_ref[0] + g1 * mix
    x1_ref[0] = x1
    ms = jnp.mean(x1 * x1, axis=-1, keepdims=True)
    h2 = ((x1 * lax.rsqrt(ms + EPS)) * n2w_ref[...]) * (1.0 + sc2) + sh2

    h2p_ref[0] = _pack_halves(h2)

    logits = _dot3(h2, wrh_ref[...], wrl_ref[...]) + br_ref[...]
    lane = lax.broadcasted_iota(I32, (tm, LANES), 1)
    glog = jnp.where(lane < N_GROUPS, logits, NEG)
    gmax = jnp.max(glog, axis=1, keepdims=True)
    g_p = 1.0 / jnp.sum(jnp.exp(glog - gmax), axis=1, keepdims=True)
    gidx = jnp.min(jnp.where(glog == gmax, lane, LANES), axis=1, keepdims=True)
    e_lo = ROUTE_E0 + EXPERTS_PER_GROUP * gidx
    el = jnp.where(lane >= e_lo, jnp.where(lane < e_lo + EXPERTS_PER_GROUP, logits, NEG), NEG)
    m1 = jnp.max(el, axis=1, keepdims=True)
    i1 = jnp.min(jnp.where(el == m1, lane, LANES), axis=1, keepdims=True)
    el2 = jnp.where(lane == i1, NEG, el)
    m2 = jnp.max(el2, axis=1, keepdims=True)
    i2 = jnp.min(jnp.where(el2 == m2, lane, LANES), axis=1, keepdims=True)
    e2 = jnp.exp(m2 - m1)
    w1 = g_p / (1.0 + e2)
    w2 = g_p * e2 / (1.0 + e2)

    oh1 = lane == i1
    oh2 = lane == i2
    both = jnp.where(oh1, 1.0, jnp.where(oh2, 1.0, 0.0))
    base = carry_ref[0:1] + _dot(tri_ref[...], both.astype(BF16))
    r1 = jnp.sum(jnp.where(oh1, base, 0.0), axis=1, keepdims=True)
    r2 = jnp.sum(jnp.where(oh2, base, 0.0), axis=1, keepdims=True)
    carry_ref[...] = carry_ref[...] + jnp.sum(both, axis=0, keepdims=True)
    cnt_ref[...] = carry_ref[...]

    rec = jnp.where(lane == 0, (i1 - ROUTE_E0).astype(F32),
                    jnp.where(lane == 1, (i2 - ROUTE_E0).astype(F32),
                              jnp.where(lane == 2, r1, jnp.where(lane == 3, r2, 0.0))))
    rect_ref[...] = rec.T
    ri_ref[0, 0] = rect_ref[0:8, :].astype(I32)
    rw_ref[0] = jnp.where(lane == 0, w1, jnp.where(lane == 1, w2, 0.0))


def _outproj(mixa, yconv, w_out, x, mod, norm2_w, w_group, b_group, w_expert, b_expert):
    b, s, d = x.shape
    tm = TOK_TILE
    wr = jnp.pad(jnp.concatenate([w_group, w_expert], axis=1), ((0, 0), (0, LANES - N_GROUPS - N_EXPERTS)))
    wr_hi = wr.astype(BF16)
    wr_lo = (wr - wr_hi.astype(F32)).astype(BF16)
    br = jnp.pad(jnp.concatenate([b_group, b_expert]), (0, LANES - N_GROUPS - N_EXPERTS)).reshape(1, LANES)
    tri = jnp.asarray(np.tril(np.ones((tm, tm), np.float32), -1), dtype=BF16)
    full = lambda shape: pl.BlockSpec(shape, lambda bi, i: (0,) * len(shape))
    row = lambda w: pl.BlockSpec((1, tm, w), lambda bi, i: (bi, i, 0))
    return pl.pallas_call(
        functools.partial(_outproj_kernel, tm=tm),
        grid=(b, s // tm),
        in_specs=[row(D_ATTN), row(D_CONV), full((d, d)), row(d),
                  pl.BlockSpec((1, 6, d), lambda bi, i: (bi, 0, 0)),
                  full((1, d)), full((d, LANES)), full((d, LANES)), full((1, LANES)), full((tm, tm))],
        out_specs=[row(d), row(d // 2), pl.BlockSpec((1, 1, 8, tm), lambda bi, i: (bi, i, 0, 0)),
                   row(LANES), full((8, LANES))],
        out_shape=[jax.ShapeDtypeStruct((b, s, d), F32),
                   jax.ShapeDtypeStruct((b, s, d // 2), U32),
                   jax.ShapeDtypeStruct((b, s // tm, 8, tm), I32),
                   jax.ShapeDtypeStruct((b, s, LANES), F32),
                   jax.ShapeDtypeStruct((8, LANES), F32)],
        scratch_shapes=[pltpu.VMEM((8, LANES), F32), pltpu.VMEM((LANES, tm), F32)],
        compiler_params=pltpu.CompilerParams(
            dimension_semantics=("arbitrary", "arbitrary"), vmem_limit_bytes=VMEM_LIMIT),
        name="outproj",
    )(mixa, yconv, w_out.astype(BF16), x, mod, norm2_w.reshape(1, d), wr_hi, wr_lo, br, tri)


def _dest_kernel(ps_ref, ri_ref, d_ref):
    for t in range(ri_ref.shape[1]):
        e = ri_ref[0, t]
        acc = pltpu.roll(e, 6, 0)
        for x in range(N_EXPERTS):
            acc = jnp.where(e == x, acc + ps_ref[x], acc)
        d_ref[0, t] = acc


def _dest_rows(pad_start, ri_c):
    b, nt, _, tm = ri_c.shape
    return pl.pallas_call(
        _dest_kernel,
        grid_spec=pltpu.PrefetchScalarGridSpec(
            num_scalar_prefetch=1,
            grid=(b,),
            in_specs=[pl.BlockSpec((1, nt, 8, tm), lambda bi, ps: (bi, 0, 0, 0))],
            out_specs=pl.BlockSpec((1, nt, 8, tm), lambda bi, ps: (bi, 0, 0, 0))),
        out_shape=jax.ShapeDtypeStruct((b, nt, 8, tm), I32),
        name="dest_rows",
    )(pad_start, ri_c)


def _for_each_row(fn):
    for base in range(0, TOK_TILE, LANES):
        def group(g, carry, base=base):
            r0 = base + pl.multiple_of(g * ROW_UNROLL, ROW_UNROLL)
            for u in range(ROW_UNROLL):
                for k in range(2):
                    fn(r0 + u, k)
            return carry

        lax.fori_loop(0, LANES // ROW_UNROLL, group, 0)


def _dispatch_kernel(dest_ref, h2p_ref, xd_in_ref, xd_ref, sem):
    del xd_in_ref

    def row_copy(r, k):
        return pltpu.make_async_copy(h2p_ref.at[pl.ds(r, 1)], xd_ref.at[pl.ds(dest_ref[0, k, r], 1)], sem)

    _for_each_row(lambda r, k: row_copy(r, k).start())
    _for_each_row(lambda r, k: row_copy(r, k).wait())


def _dispatch(dest_t, h2p, n_rows):
    t, w = h2p.shape
    xd0 = jnp.zeros((n_rows, w), U32)
    return pl.pallas_call(
        _dispatch_kernel,
        grid=(t // TOK_TILE,),
        in_specs=[pl.BlockSpec((1, 8, TOK_TILE), lambda i: (i, 0, 0), memory_space=pltpu.SMEM),
                  pl.BlockSpec((TOK_TILE, w), lambda i: (i, 0)),
                  pl.BlockSpec(memory_space=pl.ANY)],
        out_specs=pl.BlockSpec(memory_space=pl.ANY),
        scratch_shapes=[pltpu.SemaphoreType.DMA(())],
        out_shape=jax.ShapeDtypeStruct((n_rows, w), U32),
        input_output_aliases={2: 0},
        compiler_params=pltpu.CompilerParams(
            dimension_semantics=("arbitrary",), vmem_limit_bytes=VMEM_LIMIT, has_side_effects=True),
        name="dispatch",
    )(dest_t, h2p, xd0)


def _experts_kernel(be_ref, nu_ref, xd_ref, w1_ref, w3_ref, w2_ref, y_ref, w1b_ref, w3b_ref, w2b_ref):
    i = pl.program_id(0)

    @pl.when((i == 0) | (be_ref[i] != be_ref[jnp.maximum(i - 1, 0)]))
    def _():
        w1b_ref[...] = w1_ref[0].astype(BF16)
        w3b_ref[...] = w3_ref[0].astype(BF16)
        w2b_ref[...] = w2_ref[0].astype(BF16)

    @pl.when(i < nu_ref[0])
    def _():
        word = xd_ref[...]
        xa, xb = (half.astype(BF16) for half in _unpack_halves(word))
        dh = xa.shape[1]
        a = _dot(xa, w1b_ref[0:dh]) + _dot(xb, w1b_ref[dh:2 * dh])
        c = _dot(xa, w3b_ref[0:dh]) + _dot(xb, w3b_ref[dh:2 * dh])
        y_ref[...] = _pack_halves(_dot((_silu(a) * c).astype(BF16), w2b_ref[...]))

    @pl.when(i >= nu_ref[0])
    def _():
        y_ref[...] = jnp.zeros_like(y_ref)


def _experts(blk_expert, n_used, xd, w1, w3, w2):
    n_rows, dh = xd.shape
    _, d, de = w1.shape
    n_blocks = n_rows // MOE_BLOCK
    return pl.pallas_call(
        _experts_kernel,
        grid_spec=pltpu.PrefetchScalarGridSpec(
            num_scalar_prefetch=2,
            grid=(n_blocks,),
            in_specs=[pl.BlockSpec((MOE_BLOCK, dh), lambda i, be, nu: (i, 0)),
                      pl.BlockSpec((1, d, de), lambda i, be, nu: (be[i], 0, 0)),
                      pl.BlockSpec((1, d, de), lambda i, be, nu: (be[i], 0, 0)),
                      pl.BlockSpec((1, de, d), lambda i, be, nu: (be[i], 0, 0))],
            out_specs=pl.BlockSpec((MOE_BLOCK, d // 2), lambda i, be, nu: (i, 0)),
            scratch_shapes=[pltpu.VMEM((d, de), BF16), pltpu.VMEM((d, de), BF16), pltpu.VMEM((de, d), BF16)]),
        out_shape=jax.ShapeDtypeStruct((n_rows, d // 2), U32),
        compiler_params=pltpu.CompilerParams(
            dimension_semantics=("arbitrary",), vmem_limit_bytes=VMEM_LIMIT),
        name="experts",
    )(blk_expert, n_used, xd, w1, w3, w2)


def _combine_kernel(dest_ref, dnext_ref, rw_ref, x1_ref, mod_ref, y_ref, o_ref, buf_ref, sems):
    i = pl.program_id(0)
    slot = i % 2

    def gather(idx_ref, slot_, start):
        def one(r, k):
            cp = pltpu.make_async_copy(y_ref.at[pl.ds(idx_ref[0, k, r], 1)],
                                       buf_ref.at[slot_, k, pl.ds(r, 1)], sems.at[slot_])
            if start:
                cp.start()
            else:
                cp.wait()

        _for_each_row(one)

    @pl.when(i == 0)
    def _():
        gather(dest_ref, slot, True)

    @pl.when(i + 1 < pl.num_programs(0))
    def _():
        gather(dnext_ref, 1 - slot, True)

    gather(dest_ref, slot, False)
    w = rw_ref[...]
    g2 = mod_ref[0][5:6]
    y0 = jnp.concatenate(_unpack_halves(buf_ref[slot, 0]), axis=1)
    y1 = jnp.concatenate(_unpack_halves(buf_ref[slot, 1]), axis=1)
    o_ref[...] = x1_ref[...] + g2 * (w[:, 0:1] * y0 + w[:, 1:2] * y1)


def _combine(dest_t, rw, x1, mod, y_disp, seq):
    t, d = x1.shape
    tiles_per_seq = seq // TOK_TILE
    n_tiles = t // TOK_TILE
    return pl.pallas_call(
        _combine_kernel,
        grid=(n_tiles,),
        in_specs=[pl.BlockSpec((1, 8, TOK_TILE), lambda i: (i, 0, 0), memory_space=pltpu.SMEM),
                  pl.BlockSpec((1, 8, TOK_TILE), lambda i: (jnp.minimum(i + 1, n_tiles - 1), 0, 0),
                               memory_space=pltpu.SMEM),
                  pl.BlockSpec((TOK_TILE, LANES), lambda i: (i, 0)),
                  pl.BlockSpec((TOK_TILE, d), lambda i: (i, 0)),
                  pl.BlockSpec((1, 6, d), lambda i: (i // tiles_per_seq, 0, 0)),
                  pl.BlockSpec(memory_space=pl.ANY)],
        out_specs=pl.BlockSpec((TOK_TILE, d), lambda i: (i, 0)),
        scratch_shapes=[pltpu.VMEM((2, 2, TOK_TILE, d // 2), U32), pltpu.SemaphoreType.DMA((2,))],
        out_shape=jax.ShapeDtypeStruct((t, d), F32),
        compiler_params=pltpu.CompilerParams(
            dimension_semantics=("arbitrary",), vmem_limit_bytes=VMEM_LIMIT),
        name="combine",
    )(dest_t, dest_t, rw, x1, mod, y_disp)


def _moe(x1, h2p, ri, rw, cnt, mod, w1, w3, w2):
    b, s, d = x1.shape
    t = b * s
    a = 2 * t
    counts = cnt[0, ROUTE_E0:ROUTE_E0 + N_EXPERTS].astype(I32)
    padded = (counts + MOE_BLOCK - 1) // MOE_BLOCK * MOE_BLOCK
    pad_end = jnp.cumsum(padded)
    pad_start = (pad_end - padded).astype(I32)
    n_blocks = (a + N_EXPERTS * (MOE_BLOCK - 1)) // MOE_BLOCK + 1
    blk_start = jnp.arange(n_blocks, dtype=I32) * MOE_BLOCK
    blk_expert = jnp.minimum(
        jnp.sum((pad_end[None, :] <= blk_start[:, None]).astype(I32), axis=1), N_EXPERTS - 1)
    n_used = (pad_end[-1:] // MOE_BLOCK).astype(I32)
    dest_t = _dest_rows(pad_start, ri).reshape(t // TOK_TILE, 8, TOK_TILE)
    xd = _dispatch(dest_t, h2p.reshape(t, d // 2), n_blocks * MOE_BLOCK)
    y_disp = _experts(blk_expert, n_used, xd, w1, w3, w2)
    out = _combine(dest_t, rw.reshape(t, LANES), x1.reshape(t, d), mod, y_disp, s)
    return out.reshape(b, s, d)


def kernel(x, c, w_ada, b_ada, norm1_w, w_in, q_norm_w, k_norm_w, cmp_pos_k, cmp_pos_v, cmp_k_w1, cmp_k_w2,
           cmp_v_w1, cmp_v_w2, conv_w, out_norm_w, w_out, rel_bias, norm2_w, w_group, b_group, w_expert,
           b_expert, w1, w3, w2):
    t_hi, t_lo, t_c = _bias_tables(rel_bias)
    for l in range(w_ada.shape[0]):
        mod = _ada(c, w_ada[l], b_ada[l])
        q, kvc, kaug, vaug, kwin, vwaug, gates, yconv = _inproj(
            x, mod, norm1_w[l], w_in[l], q_norm_w[l], k_norm_w[l], conv_w[l], out_norm_w[l])
        vaug, vscale = _to_f8(vaug)
        vwaug, vwscale = _to_f8(vwaug)
        rows = _kvc_rows(kvc)
        k_c = _compress(rows, 0, cmp_k_w1[l], cmp_k_w2[l], cmp_pos_k[l], k_norm_w[l, 0], True)
        v_c = _compress(rows, 1, cmp_v_w1[l], cmp_v_w2[l], cmp_pos_v[l], k_norm_w[l, 0], False)
        o_cmp, pen = _cmpsel(q, k_c, v_c, t_c)
        mixa = _slcwin(q, pen, kaug, vaug, vscale, kwin, vwaug, vwscale, o_cmp, gates, t_hi, t_lo,
                       out_norm_w[l])
        x1, h2p, ri, rw, cnt = _outproj(mixa, yconv, w_out[l], x, mod, norm2_w[l],
                                        w_group[l], b_group[l], w_expert[l], b_expert[l])
        x = _moe(x1, h2p, ri, rw, cnt, mod, w1[l], w3[l], w2[l])
    return x
```

```python
import functools
import math

import numpy as np
import jax
import jax.numpy as jnp
from jax import lax
from jax.experimental import pallas as pl
from jax.experimental.pallas import tpu as pltpu

F32 = jnp.float32
BF16 = jnp.bfloat16
F8 = jnp.float8_e4m3fn
P_SCALE = 256.0
F8_TARGET = 240.0
I32 = jnp.int32

HEAD_DIM = 64
N_HEADS = 8
N_KV = 2
HPG = N_HEADS // N_KV
D_ATTN = N_HEADS * HEAD_DIM
D_CONV = 512
KV_DIM = N_KV * HEAD_DIM
CMP_LEN = 32
CMP_STRIDE = 16
CMP_HIDDEN = 256
SLC_LEN = 64
N_SELECT = 16
WINDOW = 512
N_BUCKETS = 32
MAX_DISTANCE = 128
N_GROUPS = 8
EXPERTS_PER_GROUP = 8
N_EXPERTS = N_GROUPS * EXPERTS_PER_GROUP
MOE_BLOCK = 512
FORCED_SCORE = 1e4
EPS = 1e-6

LANES = 128
QB = 512
ROWS_Q = HPG * QB
CMP_CHUNK = 256
CMP_DIST0 = CMP_STRIDE * CMP_CHUNK - (CMP_LEN - 1) - QB
IMP_PAD = CMP_CHUNK // 4
OV_ROWS = IMP_PAD + 8
FAR_GROUP = 4
PEN_BLOCKS = 128
HALF_KEYS = PEN_BLOCKS * SLC_LEN
NEG = -1e30
PICKED = -3e38
ROUTE_E0 = N_GROUPS
U32 = jnp.uint32
HI_MASK = 0xFFFF0000
TOK_TILE = 512
ROW_UNROLL = 8
VMEM_LIMIT = 56 * 1024 * 1024


def _dot(a, b):
    return jnp.dot(a, b, preferred_element_type=F32)


def _dot_nt(a, b):
    return lax.dot_general(a, b, (((1,), (1,)), ((), ())), preferred_element_type=F32)


def _hi_lo(a):
    hi = a.astype(BF16)
    lo = (a - hi.astype(F32)).astype(BF16)
    return hi, lo


def _dot3(a, b_hi, b_lo):
    a_hi, a_lo = _hi_lo(a)
    return _dot(a_hi, b_hi) + (_dot(a_lo, b_hi) + _dot(a_hi, b_lo))


def _group_sumsq(v, gm):
    hi, lo = _hi_lo(v * v)
    return _dot(hi, gm) + _dot(lo, gm)


def _silu(v):
    return v * jax.nn.sigmoid(v)


def _pack_halves(v):
    n = v.shape[1] // 2
    lo_w = pltpu.bitcast(v[:, :n].astype(BF16).astype(F32), U32)
    hi_w = pltpu.bitcast(v[:, n:].astype(BF16).astype(F32), U32)
    return lax.shift_right_logical(lo_w, jnp.uint32(16)) | (hi_w & jnp.uint32(HI_MASK))


def _unpack_halves(word):
    left = pltpu.bitcast(lax.shift_left(word, jnp.uint32(16)), F32)
    right = pltpu.bitcast(word & jnp.uint32(HI_MASK), F32)
    return left, right


def _ada_kernel(c_ref, w_ref, b_ref, o_ref):
    cond = _silu(c_ref[...])
    w_hi, w_lo = _hi_lo(w_ref[...])
    o_ref[...] = _dot3(cond, w_hi, w_lo) + b_ref[...]


def _ada(c, w_ada, b_ada):
    b, d = c.shape
    n = w_ada.shape[1]
    tn = n // 4
    c8 = jnp.pad(c, ((0, 8 - b), (0, 0)))
    out = pl.pallas_call(
        _ada_kernel,
        grid=(n // tn,),
        in_specs=[pl.BlockSpec((8, d), lambda i: (0, 0)),
                  pl.BlockSpec((d, tn), lambda i: (0, i)),
                  pl.BlockSpec((1, tn), lambda i: (0, i))],
        out_specs=pl.BlockSpec((8, tn), lambda i: (0, i)),
        out_shape=jax.ShapeDtypeStruct((8, n), F32),
        compiler_params=pltpu.CompilerParams(vmem_limit_bytes=VMEM_LIMIT),
        name="ada",
    )(c8, w_ada, b_ada.reshape(1, n))
    return out[:b].reshape(b, 6, d)


def _inproj_kernel(x_ref, xh_ref, mod_ref, n1w_ref, wq_ref, wkv_ref, wg_ref, wbcu_ref,
                   qnw_ref, knw_ref, convw_ref, onw_ref, gm_ref,
                   q_ref, kvc_ref, kaug_ref, vaug_ref, kwin_ref, vwaug_ref, gates_ref, yconv_ref,
                   ext_ref, *, tm):
    i = pl.program_id(1)
    mod = mod_ref[0]
    sh1 = mod[0:1]
    sc1 = mod[1:2]

    def norm_mod(v):
        ms = jnp.mean(v * v, axis=-1, keepdims=True)
        y = (v * lax.rsqrt(ms + EPS)) * n1w_ref[...]
        return (y * (1.0 + sc1) + sh1).astype(BF16)

    h = norm_mod(x_ref[0])
    gm = gm_ref[...]
    gm_kv = gm_ref[0:KV_DIM, 0:KV_DIM]

    q = _dot(h, wq_ref[...])
    qn = q * lax.rsqrt(_group_sumsq(q, gm) * (1.0 / HEAD_DIM) + EPS) * qnw_ref[...]
    qn = (qn * (HEAD_DIM ** -0.5)).astype(BF16)
    for hd in range(N_HEADS):
        q_ref[0, hd] = qn[:, hd * HEAD_DIM:(hd + 1) * HEAD_DIM]

    kv = _dot(h, wkv_ref[...])
    kvc_ref[0] = kv[:, 0:2 * KV_DIM].astype(BF16)
    ks = kv[:, 2 * KV_DIM:3 * KV_DIM]
    vs = kv[:, 3 * KV_DIM:4 * KV_DIM]
    kw = kv[:, 4 * KV_DIM:5 * KV_DIM]
    vw = kv[:, 5 * KV_DIM:6 * KV_DIM]
    ksn = ks * lax.rsqrt(_group_sumsq(ks, gm_kv) * (1.0 / HEAD_DIM) + EPS) * knw_ref[0:1]
    kwn = kw * lax.rsqrt(_group_sumsq(kw, gm_kv) * (1.0 / HEAD_DIM) + EPS) * knw_ref[1:2]

    lane = lax.broadcasted_iota(I32, (tm, LANES), 1)
    row = lax.broadcasted_iota(I32, (tm, LANES), 0)
    blk = ((i * tm + row) // SLC_LEN) % PEN_BLOCKS
    onehot = jnp.where(lane == blk, 1.0, 0.0).astype(BF16)
    ones_col = jnp.where(lax.broadcasted_iota(I32, (tm, HEAD_DIM), 1) == 0, 1.0, 0.0).astype(BF16)
    zeros64 = jnp.zeros((tm, HEAD_DIM), BF16)
    for g in range(N_KV):
        sl = slice(g * HEAD_DIM, (g + 1) * HEAD_DIM)
        kaug_ref[0, g, :, 0:LANES] = onehot
        kaug_ref[0, g, :, LANES:LANES + HEAD_DIM] = ksn[:, sl].astype(BF16)
        kaug_ref[0, g, :, LANES + HEAD_DIM:2 * LANES] = zeros64
        vaug_ref[0, g, :, 0:HEAD_DIM] = vs[:, sl].astype(BF16)
        vaug_ref[0, g, :, HEAD_DIM:LANES] = ones_col
        kwin_ref[0, g] = kwn[:, sl].astype(BF16)
        vwaug_ref[0, g, :, 0:HEAD_DIM] = vw[:, sl].astype(BF16)
        vwaug_ref[0, g, :, HEAD_DIM:LANES] = ones_col

    gates_ref[0] = jax.nn.sigmoid(_dot(h, wg_ref[...]))

    bcu = _dot(h, wbcu_ref[...])
    bg = bcu[:, 0:D_CONV]
    cu = bcu[:, D_CONV:2 * D_CONV] * bcu[:, 2 * D_CONV:3 * D_CONV]
    hh = norm_mod(xh_ref[0])
    cuh = _dot(hh, wbcu_ref[:, D_CONV:2 * D_CONV]) * _dot(hh, wbcu_ref[:, 2 * D_CONV:3 * D_CONV])
    cuh = jnp.where(i > 0, cuh, 0.0)
    ext_ref[0:8] = cuh
    ext_ref[8:8 + tm] = cu
    cw = convw_ref[...]
    y = bg * (cw[0:1] * ext_ref[6:6 + tm] + cw[1:2] * ext_ref[7:7 + tm] + cw[2:3] * cu)
    yn = y * lax.rsqrt(_group_sumsq(y, gm) * (1.0 / HEAD_DIM) + EPS) * onw_ref[...]
    yconv_ref[0] = yn.astype(BF16)


def _inproj(x, mod, norm1_w, w_in, q_norm_w, k_norm_w, conv_w, out_norm_w, tm=512):
    b, s, d = x.shape
    o_kv = D_ATTN
    o_g = o_kv + 6 * KV_DIM
    o_b = o_g + 3 * N_HEADS
    wq = w_in[:, 0:o_kv].astype(BF16)
    wkv = w_in[:, o_kv:o_g].astype(BF16)
    wg = jnp.pad(w_in[:, o_g:o_b].reshape(d, N_KV, 3 * HPG),
                 ((0, 0), (0, 0), (0, LANES - 3 * HPG))).reshape(d, N_KV * LANES).astype(BF16)
    wbcu = w_in[:, o_b:].astype(BF16)
    qnw = jnp.tile(q_norm_w, N_HEADS).reshape(1, D_ATTN)
    knw = jnp.stack([jnp.tile(k_norm_w[1], N_KV), jnp.tile(k_norm_w[2], N_KV)])
    onw = out_norm_w[D_ATTN:].reshape(1, D_CONV)
    gidx = np.arange(D_ATTN) // HEAD_DIM
    gm = jnp.asarray((gidx[:, None] == gidx[None, :]).astype(np.float32), dtype=BF16)

    full = lambda shape: pl.BlockSpec(shape, lambda bi, i: (0,) * len(shape))
    outs = pl.pallas_call(
        functools.partial(_inproj_kernel, tm=tm),
        grid=(b, s // tm),
        in_specs=[pl.BlockSpec((1, tm, d), lambda bi, i: (bi, i, 0)),
                  pl.BlockSpec((1, 8, d), lambda bi, i: (bi, jnp.maximum(i * (tm // 8) - 1, 0), 0)),
                  pl.BlockSpec((1, 6, d), lambda bi, i: (bi, 0, 0)),
                  full((1, d)), full(wq.shape), full(wkv.shape), full(wg.shape), full(wbcu.shape),
                  full(qnw.shape), full(knw.shape), full(conv_w.shape), full(onw.shape), full(gm.shape)],
        out_specs=[pl.BlockSpec((1, N_HEADS, tm, HEAD_DIM), lambda bi, i: (bi, 0, i, 0)),
                   pl.BlockSpec((1, tm, 2 * KV_DIM), lambda bi, i: (bi, i, 0)),
                   pl.BlockSpec((1, N_KV, tm, 2 * LANES), lambda bi, i: (bi, 0, i, 0)),
                   pl.BlockSpec((1, N_KV, tm, LANES), lambda bi, i: (bi, 0, i, 0)),
                   pl.BlockSpec((1, N_KV, tm, HEAD_DIM), lambda bi, i: (bi, 0, i, 0)),
                   pl.BlockSpec((1, N_KV, tm, LANES), lambda bi, i: (bi, 0, i, 0)),
                   pl.BlockSpec((1, tm, N_KV * LANES), lambda bi, i: (bi, i, 0)),
                   pl.BlockSpec((1, tm, D_CONV), lambda bi, i: (bi, i, 0))],
        out_shape=[jax.ShapeDtypeStruct((b, N_HEADS, s, HEAD_DIM), BF16),
                   jax.ShapeDtypeStruct((b, s, 2 * KV_DIM), BF16),
                   jax.ShapeDtypeStruct((b, N_KV, s, 2 * LANES), BF16),
                   jax.ShapeDtypeStruct((b, N_KV, s, LANES), BF16),
                   jax.ShapeDtypeStruct((b, N_KV, s, HEAD_DIM), BF16),
                   jax.ShapeDtypeStruct((b, N_KV, s, LANES), BF16),
                   jax.ShapeDtypeStruct((b, s, N_KV * LANES), F32),
                   jax.ShapeDtypeStruct((b, s, D_CONV), BF16)],
        scratch_shapes=[pltpu.VMEM((tm + 8, D_CONV), F32)],
        compiler_params=pltpu.CompilerParams(
            dimension_semantics=("parallel", "arbitrary"), vmem_limit_bytes=VMEM_LIMIT),
        name="inproj",
    )(x, x, mod, norm1_w.reshape(1, d), wq, wkv, wg, wbcu, qnw, knw, conv_w, onw, gm)
    return outs


def _compress_kernel(r_ref, w1_ref, w2_ref, pos_ref, nw_ref, o_ref, *, norm):
    r = r_ref[0, 0, 0]
    nr = r.shape[0]
    half = (CMP_LEN // 2) * HEAD_DIM
    w1a = w1_ref[0:half, :]
    w1b = w1_ref[half:2 * half, :]
    pos = pos_ref[...]
    cst = (_dot(jnp.broadcast_to(pos[0:1], (8, half)).astype(BF16), w1a)
           + _dot(jnp.broadcast_to(pos[1:2], (8, half)).astype(BF16), w1b))[0:1]
    upper = pltpu.roll(_dot(r, w1b), nr - 1, 0)
    hid = _dot(r, w1a) + upper + cst
    out = _dot(_silu(hid).astype(BF16), w2_ref[...])
    if norm:
        ms = jnp.mean(out * out, axis=-1, keepdims=True)
        out = out * lax.rsqrt(ms + EPS) * nw_ref[...]
        o_ref[0, 0] = out.astype(BF16)
    else:
        o_ref[0, 0, :, 0:HEAD_DIM] = out.astype(BF16)
        o_ref[0, 0, :, HEAD_DIM:LANES] = jnp.where(
            lax.broadcasted_iota(I32, (nr, HEAD_DIM), 1) == 0, 1.0, 0.0).astype(BF16)


def _compress(rows, kv_index, w1, w2, pos_emb, norm_w, norm):
    b, _, g, nr, width = rows.shape
    out_w = HEAD_DIM if norm else LANES
    return pl.pallas_call(
        functools.partial(_compress_kernel, norm=norm),
        grid=(b, g),
        in_specs=[pl.BlockSpec((1, 1, 1, nr, width), lambda bi, gi: (bi, kv_index, gi, 0, 0)),
                  pl.BlockSpec(w1.shape, lambda bi, gi: (0, 0)),
                  pl.BlockSpec(w2.shape, lambda bi, gi: (0, 0)),
                  pl.BlockSpec((2, width), lambda bi, gi: (0, 0)),
                  pl.BlockSpec((1, HEAD_DIM), lambda bi, gi: (0, 0))],
        out_specs=pl.BlockSpec((1, 1, nr, out_w), lambda bi, gi: (bi, gi, 0, 0)),
        out_shape=jax.ShapeDtypeStruct((b, g, nr, out_w), BF16),
        compiler_params=pltpu.CompilerParams(vmem_limit_bytes=VMEM_LIMIT),
        name="compress_k" if norm else "compress_v",
    )(rows, w1.astype(BF16), w2.astype(BF16), pos_emb.reshape(2, width), norm_w.reshape(1, HEAD_DIM))


def _kvc_rows(kvc):
    b, s, _ = kvc.shape
    r = kvc.reshape(b, s, 2, N_KV, HEAD_DIM).transpose(0, 2, 3, 1, 4)
    return r.reshape(b, 2, N_KV, s // CMP_STRIDE, CMP_STRIDE * HEAD_DIM)


def _bucket_np(dist):
    n = np.maximum(dist, 0)
    max_exact = N_BUCKETS // 2
    nf = np.maximum(n, 1).astype(np.float32)
    large = max_exact + (np.log(nf / np.float32(max_exact)) / np.float32(math.log(MAX_DISTANCE / max_exact))
                         * np.float32(N_BUCKETS - max_exact)).astype(np.int32)
    large = np.minimum(large, N_BUCKETS - 1)
    return np.where(n < max_exact, n, large)


_BUCKET_START = [int(np.argmax(_bucket_np(np.arange(MAX_DISTANCE + 1)) >= k)) for k in range(N_BUCKETS)]


def _bias_tables_kernel(rbt_ref, thi_ref, tlo_ref, tc_ref):
    head = pl.program_id(0) * HPG + pl.program_id(1)
    far = rbt_ref[head, N_BUCKETS - 1]

    def table(dist):
        v = jnp.zeros(dist.shape, F32)
        for k in range(N_BUCKETS - 2, -1, -1):
            v = jnp.where(dist < _BUCKET_START[k + 1], rbt_ref[head, k] - far, v)
        return jnp.where(dist >= 0, v, NEG)

    def dist(shape, row_mul, col_mul, off):
        return (lax.broadcasted_iota(I32, shape, 0) * row_mul
                + lax.broadcasted_iota(I32, shape, 1) * col_mul + off)

    thi_ref[0] = table(dist((QB, QB), 1, -1, 0))
    tlo_ref[0] = table(dist((QB, LANES), 1, -1, LANES))
    tc_ref[0] = table(dist((QB, CMP_CHUNK), 1, -CMP_STRIDE, CMP_DIST0))


def _bias_tables(rel_bias):
    return pl.pallas_call(
        _bias_tables_kernel,
        grid=(N_KV, HPG),
        in_specs=[pl.BlockSpec(memory_space=pltpu.SMEM)],
        out_specs=[pl.BlockSpec((1, QB, QB), lambda g, h: (g, h, 0)),
                   pl.BlockSpec((1, QB, LANES), lambda g, h: (g, h, 0)),
                   pl.BlockSpec((1, QB, CMP_CHUNK), lambda g, h: (g, h, 0))],
        out_shape=[jax.ShapeDtypeStruct((N_KV, ROWS_Q, QB), F32),
                   jax.ShapeDtypeStruct((N_KV, ROWS_Q, LANES), F32),
                   jax.ShapeDtypeStruct((N_KV, ROWS_Q, CMP_CHUNK), F32)],
        name="bias_tables",
    )(rel_bias.T)


def _cmpsel_kernel(q_ref, kc_ref, vc_ref, tab_ref, ovt_ref, ocmp_ref, pen_ref,
                   m_ref, acc_ref, imp_ref, score_ref, invt_ref, *, nsbp, n_sel):
    j = pl.program_id(2)
    q4 = q_ref[0].reshape(ROWS_Q, HEAD_DIM)
    nq16 = QB // CMP_STRIDE
    n_chunks = (nq16 * (j + 1) - 1) // CMP_CHUNK + 1
    ni = lax.broadcasted_iota(I32, (1, CMP_CHUNK), 1)

    def chunk_start(m):
        return pl.multiple_of(nq16 * (j + 1) - CMP_CHUNK * m, nq16)

    def logits(m):
        n0p = chunk_start(m)
        s = _dot_nt(q4, kc_ref[0, 0, pl.ds(n0p, CMP_CHUNK), :])
        return jnp.where(ni + n0p >= CMP_CHUNK, s, NEG)

    s0 = logits(0) + tab_ref[0]
    m_ref[...] = s0

    def stats(m, carry):
        m_ref[...] = jnp.maximum(m_ref[...], logits(m))
        return carry

    lax.fori_loop(1, n_chunks, stats, 0)
    m_fin = jnp.max(m_ref[...], axis=1, keepdims=True)
    m_ref[...] = jnp.broadcast_to(m_fin, (ROWS_Q, CMP_CHUNK))
    acc_ref[...] = jnp.zeros_like(acc_ref)
    imp_ref[...] = jnp.zeros_like(imp_ref)

    def accumulate(m, s):
        n0p = chunk_start(m)
        p = jnp.exp(s - m_ref[...]).astype(BF16)
        acc_ref[...] += _dot(p, vc_ref[0, 0, pl.ds(n0p, CMP_CHUNK), :])
        r0 = pl.multiple_of(n0p // 4, 8)
        for hd in range(HPG):
            imp_ref[hd, pl.ds(r0, OV_ROWS), :] += _dot_nt(ovt_ref[...], p[hd * QB:(hd + 1) * QB])

    accumulate(0, s0)

    def second(m, carry):
        accumulate(m, logits(m))
        return carry

    lax.fori_loop(1, n_chunks, second, 0)
    acc = acc_ref[...]
    inv = jnp.where(m_fin > 0.5 * NEG, 1.0 / acc[:, HEAD_DIM:HEAD_DIM + 1], 0.0)
    ocmp_ref[0, 0] = (acc[:, 0:HEAD_DIM] * inv).reshape(HPG, QB, HEAD_DIM)

    invt_ref[...] = jnp.broadcast_to(inv, (ROWS_Q, LANES)).T
    imp = jnp.zeros((nsbp, QB), F32)
    for hd in range(HPG):
        imp = imp + imp_ref[hd, IMP_PAD:IMP_PAD + nsbp, :] * invt_ref[0:1, hd * QB:(hd + 1) * QB]
    srow = lax.broadcasted_iota(I32, (nsbp, QB), 0)
    qcol = lax.broadcasted_iota(I32, (nsbp, QB), 1)
    cur = (QB // SLC_LEN) * j + qcol // SLC_LEN
    forced = (srow == 0) | (srow == cur) | (srow == cur - 1)
    score_ref[...] = jnp.where(forced, -1.0, jnp.where(srow <= cur, imp, -1.0))

    quarter = nsbp // 4
    case = ((QB // SLC_LEN) * (j + 1) - 1) // quarter
    for k in range(4):
        rows = quarter * (k + 1)

        @pl.when(case == k)
        def _(rows=rows):
            srow_k = lax.broadcasted_iota(I32, (rows, QB), 0)

            def pick(r, carry):
                sc = score_ref[0:rows]
                mx = jnp.max(sc, axis=0, keepdims=True)
                idx = jnp.min(jnp.where(sc == mx, srow_k, nsbp), axis=0, keepdims=True)
                score_ref[0:rows] = jnp.where(srow_k == idx, PICKED, sc)
                return carry

            lax.fori_loop(0, max(n_sel - 3, 0), pick, 0)
    pen_other = jnp.where(srow <= cur, jnp.where(score_ref[...] == PICKED, 0.0, NEG), NEG)
    pen_t = jnp.where(forced, 0.0, pen_other).T
    for hf in range(nsbp // PEN_BLOCKS):
        pen_ref[0, 0, hf] = pen_t[:, hf * PEN_BLOCKS:(hf + 1) * PEN_BLOCKS].astype(BF16)


def _overlap_t():
    n = np.arange(CMP_CHUNK)
    s = np.arange(OV_ROWS)
    first = (n * CMP_STRIDE) // SLC_LEN
    last = (n * CMP_STRIDE + CMP_LEN - 1) // SLC_LEN
    ov = (s[:, None] >= first[None, :]) & (s[:, None] <= last[None, :])
    return jnp.asarray(ov.astype(np.float32), dtype=BF16)


def _cmpsel(q, k_c, v_c, tab):
    b, _, s, _ = q.shape
    nr = k_c.shape[2]
    nq = s // QB
    nsb = s // SLC_LEN
    nh = -(-nsb // PEN_BLOCKS)
    nsbp = nh * PEN_BLOCKS
    n_sel = min(N_SELECT, nsb)
    pad = ((0, 0), (0, 0), (CMP_CHUNK, 0), (0, 0))
    kc_p = jnp.pad(k_c, pad)
    vc_p = jnp.pad(v_c, pad)
    ovt = _overlap_t()
    return pl.pallas_call(
        functools.partial(_cmpsel_kernel, nsbp=nsbp, n_sel=n_sel),
        grid=(b, N_KV, nq),
        in_specs=[pl.BlockSpec((1, HPG, QB, HEAD_DIM), lambda bi, g, j: (bi, g, j, 0)),
                  pl.BlockSpec((1, 1, nr + CMP_CHUNK, HEAD_DIM), lambda bi, g, j: (bi, g, 0, 0)),
                  pl.BlockSpec((1, 1, nr + CMP_CHUNK, LANES), lambda bi, g, j: (bi, g, 0, 0)),
                  pl.BlockSpec((1, ROWS_Q, CMP_CHUNK), lambda bi, g, j: (g, 0, 0)),
                  pl.BlockSpec(ovt.shape, lambda bi, g, j: (0, 0))],
        out_specs=[pl.BlockSpec((1, 1, HPG, QB, HEAD_DIM), lambda bi, g, j: (bi, g, 0, j, 0)),
                   pl.BlockSpec((1, 1, nh, QB, PEN_BLOCKS), lambda bi, g, j: (bi, g, 0, j, 0))],
        out_shape=[jax.ShapeDtypeStruct((b, N_KV, HPG, s, HEAD_DIM), F32),
                   jax.ShapeDtypeStruct((b, N_KV, nh, s, PEN_BLOCKS), BF16)],
        scratch_shapes=[pltpu.VMEM((ROWS_Q, CMP_CHUNK), F32),
                        pltpu.VMEM((ROWS_Q, LANES), F32),
                        pltpu.VMEM((HPG, IMP_PAD + nsbp + 8, QB), F32),
                        pltpu.VMEM((nsbp, QB), F32),
                        pltpu.VMEM((LANES, ROWS_Q), F32)],
        compiler_params=pltpu.CompilerParams(
            dimension_semantics=("parallel", "parallel", "arbitrary"), vmem_limit_bytes=VMEM_LIMIT),
        name="cmpsel",
    )(q, kc_p, vc_p, tab, ovt)


def _to_f8_kernel(v_ref, o_ref, sc_ref):
    v = v_ref[0, 0].astype(F32)
    lane = lax.broadcasted_iota(I32, (1, LANES), 1)
    cmax = jnp.max(jnp.abs(v), axis=0, keepdims=True)
    scale = jnp.where(lane < HEAD_DIM, jnp.maximum(cmax, 1e-30) * (1.0 / F8_TARGET), 1.0)
    o_ref[0, 0] = (v / scale).astype(F8)
    sc_ref[0, 0] = jnp.broadcast_to(scale, (8, LANES))


def _to_f8(vaug):
    b, g, s, w = vaug.shape
    return pl.pallas_call(
        _to_f8_kernel,
        grid=(b, g),
        in_specs=[pl.BlockSpec((1, 1, s, w), lambda bi, gi: (bi, gi, 0, 0))],
        out_specs=[pl.BlockSpec((1, 1, s, w), lambda bi, gi: (bi, gi, 0, 0)),
                   pl.BlockSpec((1, 1, 8, w), lambda bi, gi: (bi, gi, 0, 0))],
        out_shape=[jax.ShapeDtypeStruct((b, g, s, w), F8), jax.ShapeDtypeStruct((b, g, 8, w), F32)],
        compiler_params=pltpu.CompilerParams(vmem_limit_bytes=VMEM_LIMIT),
        name="to_f8",
    )(vaug)


def _slcwin_kernel(q_ref, pen_ref, kaug_ref, vaug_ref, vsc_ref, kwp_ref, kwc_ref, vwp_ref, vwc_ref,
                   vwsc_ref, thi_ref, tlo_ref, ocmp_ref, gates_ref, onw_ref, o_ref,
                   lhs_ref, m_ref, acc_ref):
    j = pl.program_id(2)
    q4 = q_ref[0].reshape(ROWS_Q, HEAD_DIM)
    near = QB - LANES

    lhs_ref[:, LANES:LANES + HEAD_DIM] = q4
    lhs_ref[:, LANES + HEAD_DIM:2 * LANES] = jnp.zeros((ROWS_Q, HEAD_DIM), BF16)

    def set_pen(half):
        p = pen_ref[0, 0, half]
        for hd in range(HPG):
            lhs_ref[hd * QB:(hd + 1) * QB, 0:LANES] = p

    def reset():
        m_ref[...] = jnp.full_like(m_ref, NEG)
        acc_ref[...] = jnp.zeros_like(acc_ref)

    def update(s, v):
        sb = s.astype(BF16)
        m_prev = m_ref[...]
        m_new = jnp.maximum(m_prev, jnp.max(sb, axis=1, keepdims=True).astype(F32))
        x = sb - jnp.concatenate([m_new.astype(BF16)] * (s.shape[1] // LANES), axis=1)
        p = (jnp.exp(x) * P_SCALE).astype(F8)
        acc_ref[...] = acc_ref[...] * jnp.exp(m_prev - m_new) + _dot(p, v)
        m_ref[...] = m_new

    def add_near_bias(s):
        return jnp.concatenate([s[:, :near], s[:, near:] + tlo_ref[0]], axis=1)

    def result(scale_ref):
        acc = acc_ref[...]
        return acc[:, 0:HEAD_DIM] * scale_ref[0, 0, 0:1, 0:HEAD_DIM] / acc[:, HEAD_DIM:HEAD_DIM + 1]

    def slc_chunk(c, bias):
        k0 = pl.multiple_of(c * QB, QB)
        s = _dot_nt(lhs_ref[...], kaug_ref[0, 0, pl.ds(k0, QB), :])
        update(bias(s), vaug_ref[0, 0, pl.ds(k0, QB), :])

    chunks_per_half = HALF_KEYS // QB
    assert chunks_per_half % FAR_GROUP == 0
    reset()
    half_hi = j // chunks_per_half
    set_pen(half_hi)
    slc_chunk(j, lambda s: s + thi_ref[0])

    @pl.when(j > 0)
    def _():
        half_lo = (j - 1) // chunks_per_half

        @pl.when(half_lo != half_hi)
        def _():
            set_pen(half_lo)

        slc_chunk(j - 1, add_near_bias)

    n_far = jnp.maximum(j - 1, 0)

    def enter_half(c):
        @pl.when(c % chunks_per_half == 0)
        def _():
            set_pen(c // chunks_per_half)

    def far_group(i4, carry):
        c = FAR_GROUP * i4
        enter_half(c)
        for u in range(FAR_GROUP):
            slc_chunk(c + u, lambda s: s)
        return carry

    n_groups = n_far // FAR_GROUP
    lax.fori_loop(0, n_groups, far_group, 0)

    def far_single(c, carry):
        enter_half(c)
        slc_chunk(c, lambda s: s)
        return carry

    lax.fori_loop(n_groups * FAR_GROUP, n_far, far_single, 0)

    o_slc = result(vsc_ref)

    reset()
    update(_dot_nt(q4, kwc_ref[0, 0]) + thi_ref[0], vwc_ref[0, 0])

    @pl.when(j > 0)
    def _():
        s = _dot_nt(q4, kwp_ref[0, 0])
        qi = lax.broadcasted_iota(I32, (ROWS_Q, QB), 0) & (QB - 1)
        kk = lax.broadcasted_iota(I32, (ROWS_Q, QB), 1)
        s = jnp.where(kk > qi, s, NEG)
        update(add_near_bias(s), vwp_ref[0, 0])

    o_win = result(vwsc_ref)

    o_cmp = ocmp_ref[0, 0].reshape(ROWS_Q, HEAD_DIM)
    gt = gates_ref[0]
    outs = []
    for hd in range(HPG):
        rs = slice(hd * QB, (hd + 1) * QB)
        o = (gt[:, 3 * hd:3 * hd + 1] * o_cmp[rs] + gt[:, 3 * hd + 1:3 * hd + 2] * o_slc[rs]
             + gt[:, 3 * hd + 2:3 * hd + 3] * o_win[rs])
        ms = jnp.mean(o * o, axis=-1, keepdims=True)
        outs.append((o * lax.rsqrt(ms + EPS) * onw_ref[0, hd:hd + 1, :]).astype(BF16))
    o_ref[0] = jnp.concatenate(outs, axis=1)


def _slcwin(q, pen, kaug, vaug, vscale, kwin, vwaug, vwscale, o_cmp, gates, t_hi, t_lo, out_norm_w):
    assert QB == WINDOW
    b, _, s, _ = q.shape
    nq = s // QB
    nh = pen.shape[2]
    onw = out_norm_w[:D_ATTN].reshape(N_KV, HPG, HEAD_DIM)
    prev = lambda bi, g, j: (bi, g, jnp.maximum(j - 1, 0), 0)
    curr = lambda bi, g, j: (bi, g, j, 0)
    per_group = lambda bi, g, j: (bi, g, 0, 0)
    return pl.pallas_call(
        _slcwin_kernel,
        grid=(b, N_KV, nq),
        in_specs=[pl.BlockSpec((1, HPG, QB, HEAD_DIM), lambda bi, g, j: (bi, g, j, 0)),
                  pl.BlockSpec((1, 1, nh, QB, PEN_BLOCKS), lambda bi, g, j: (bi, g, 0, j, 0)),
                  pl.BlockSpec((1, 1, s, 2 * LANES), lambda bi, g, j: (bi, g, 0, 0)),
                  pl.BlockSpec((1, 1, s, LANES), lambda bi, g, j: (bi, g, 0, 0)),
                  pl.BlockSpec((1, 1, 8, LANES), per_group),
                  pl.BlockSpec((1, 1, QB, HEAD_DIM), prev),
                  pl.BlockSpec((1, 1, QB, HEAD_DIM), curr),
                  pl.BlockSpec((1, 1, QB, LANES), prev),
                  pl.BlockSpec((1, 1, QB, LANES), curr),
                  pl.BlockSpec((1, 1, 8, LANES), per_group),
                  pl.BlockSpec((1, ROWS_Q, QB), lambda bi, g, j: (g, 0, 0)),
                  pl.BlockSpec((1, ROWS_Q, LANES), lambda bi, g, j: (g, 0, 0)),
                  pl.BlockSpec((1, 1, HPG, QB, HEAD_DIM), lambda bi, g, j: (bi, g, 0, j, 0)),
                  pl.BlockSpec((1, QB, LANES), lambda bi, g, j: (bi, j, g)),
                  pl.BlockSpec((1, HPG, HEAD_DIM), lambda bi, g, j: (g, 0, 0))],
        out_specs=pl.BlockSpec((1, QB, HPG * HEAD_DIM), lambda bi, g, j: (bi, j, g)),
        out_shape=jax.ShapeDtypeStruct((b, s, D_ATTN), BF16),
        scratch_shapes=[pltpu.VMEM((ROWS_Q, 2 * LANES), BF16),
                        pltpu.VMEM((ROWS_Q, LANES), F32),
                        pltpu.VMEM((ROWS_Q, LANES), F32)],
        compiler_params=pltpu.CompilerParams(
            dimension_semantics=("parallel", "parallel", "arbitrary"), vmem_limit_bytes=VMEM_LIMIT),
        name="slcwin",
    )(q, pen, kaug, vaug, vscale, kwin, kwin, vwaug, vwaug, vwscale, t_hi, t_lo, o_cmp, gates, onw)


def _outproj_kernel(mixa_ref, yc_ref, wo_ref, x_ref, mod_ref, n2w_ref, wrh_ref, wrl_ref, br_ref, tri_ref,
                    x1_ref, h2p_ref, ri_ref, rw_ref, cnt_ref, carry_ref, rect_ref, *, tm):
    @pl.when((pl.program_id(0) == 0) & (pl.program_id(1) == 0))
    def _():
        carry_ref[...] = jnp.zeros_like(carry_ref)

    mod = mod_ref[0]
    g1 = mod[2:3]
    sh2 = mod[3:4]
    sc2 = mod[4:5]
    d_half = wo_ref.shape[0] // 2
    mix = _dot(mixa_ref[0], wo_ref[0:d_half]) + _dot(yc_ref[0], wo_ref[d_half:2 * d_half])
    x1 = x_ref[0] + g1 * mix
    x1_ref[0] = x1
    ms = jnp.mean(x1 * x1, axis=-1, keepdims=True)
    h2 = ((x1 * lax.rsqrt(ms + EPS)) * n2w_ref[...]) * (1.0 + sc2) + sh2

    h2p_ref[0] = _pack_halves(h2)

    logits = _dot3(h2, wrh_ref[...], wrl_ref[...]) + br_ref[...]
    lane = lax.broadcasted_iota(I32, (tm, LANES), 1)
    glog = jnp.where(lane < N_GROUPS, logits, NEG)
    gmax = jnp.max(glog, axis=1, keepdims=True)
    g_p = 1.0 / jnp.sum(jnp.exp(glog - gmax), axis=1, keepdims=True)
    gidx = jnp.min(jnp.where(glog == gmax, lane, LANES), axis=1, keepdims=True)
    e_lo = ROUTE_E0 + EXPERTS_PER_GROUP * gidx
    el = jnp.where(lane >= e_lo, jnp.where(lane < e_lo + EXPERTS_PER_GROUP, logits, NEG), NEG)
    m1 = jnp.max(el, axis=1, keepdims=True)
    i1 = jnp.min(jnp.where(el == m1, lane, LANES), axis=1, keepdims=True)
    el2 = jnp.where(lane == i1, NEG, el)
    m2 = jnp.max(el2, axis=1, keepdims=True)
    i2 = jnp.min(jnp.where(el2 == m2, lane, LANES), axis=1, keepdims=True)
    e2 = jnp.exp(m2 - m1)
    w1 = g_p / (1.0 + e2)
    w2 = g_p * e2 / (1.0 + e2)

    oh1 = lane == i1
    oh2 = lane == i2
    both = jnp.where(oh1, 1.0, jnp.where(oh2, 1.0, 0.0))
    base = carry_ref[0:1] + _dot(tri_ref[...], both.astype(BF16))
    r1 = jnp.sum(jnp.where(oh1, base, 0.0), axis=1, keepdims=True)
    r2 = jnp.sum(jnp.where(oh2, base, 0.0), axis=1, keepdims=True)
    carry_ref[...] = carry_ref[...] + jnp.sum(both, axis=0, keepdims=True)
    cnt_ref[...] = carry_ref[...]

    rec = jnp.where(lane == 0, (i1 - ROUTE_E0).astype(F32),
                    jnp.where(lane == 1, (i2 - ROUTE_E0).astype(F32),
                              jnp.where(lane == 2, r1, jnp.where(lane == 3, r2, 0.0))))
    rect_ref[...] = rec.T
    ri_ref[0, 0] = rect_ref[0:8, :].astype(I32)
    rw_ref[0] = jnp.where(lane == 0, w1, jnp.where(lane == 1, w2, 0.0))


def _outproj(mixa, yconv, w_out, x, mod, norm2_w, w_group, b_group, w_expert, b_expert):
    b, s, d = x.shape
    tm = TOK_TILE
    wr = jnp.pad(jnp.concatenate([w_group, w_expert], axis=1), ((0, 0), (0, LANES - N_GROUPS - N_EXPERTS)))
    wr_hi = wr.astype(BF16)
    wr_lo = (wr - wr_hi.astype(F32)).astype(BF16)
    br = jnp.pad(jnp.concatenate([b_group, b_expert]), (0, LANES - N_GROUPS - N_EXPERTS)).reshape(1, LANES)
    tri = jnp.asarray(np.tril(np.ones((tm, tm), np.float32), -1), dtype=BF16)
    full = lambda shape: pl.BlockSpec(shape, lambda bi, i: (0,) * len(shape))
    row = lambda w: pl.BlockSpec((1, tm, w), lambda bi, i: (bi, i, 0))
    return pl.pallas_call(
        functools.partial(_outproj_kernel, tm=tm),
        grid=(b, s // tm),
        in_specs=[row(D_ATTN), row(D_CONV), full((d, d)), row(d),
                  pl.BlockSpec((1, 6, d), lambda bi, i: (bi, 0, 0)),
                  full((1, d)), full((d, LANES)), full((d, LANES)), full((1, LANES)), full((tm, tm))],
        out_specs=[row(d), row(d // 2), pl.BlockSpec((1, 1, 8, tm), lambda bi, i: (bi, i, 0, 0)),
                   row(LANES), full((8, LANES))],
        out_shape=[jax.ShapeDtypeStruct((b, s, d), F32),
                   jax.ShapeDtypeStruct((b, s, d // 2), U32),
                   jax.ShapeDtypeStruct((b, s // tm, 8, tm), I32),
                   jax.ShapeDtypeStruct((b, s, LANES), F32),
                   jax.ShapeDtypeStruct((8, LANES), F32)],
        scratch_shapes=[pltpu.VMEM((8, LANES), F32), pltpu.VMEM((LANES, tm), F32)],
        compiler_params=pltpu.CompilerParams(
            dimension_semantics=("arbitrary", "arbitrary"), vmem_limit_bytes=VMEM_LIMIT),
        name="outproj",
    )(mixa, yconv, w_out.astype(BF16), x, mod, norm2_w.reshape(1, d), wr_hi, wr_lo, br, tri)


def _dest_kernel(ps_ref, ri_ref, d_ref):
    for t in range(ri_ref.shape[1]):
        e = ri_ref[0, t]
        acc = pltpu.roll(e, 6, 0)
        for x in range(N_EXPERTS):
            acc = jnp.where(e == x, acc + ps_ref[x], acc)
        d_ref[0, t] = acc


def _dest_rows(pad_start, ri_c):
    b, nt, _, tm = ri_c.shape
    return pl.pallas_call(
        _dest_kernel,
        grid_spec=pltpu.PrefetchScalarGridSpec(
            num_scalar_prefetch=1,
            grid=(b,),
            in_specs=[pl.BlockSpec((1, nt, 8, tm), lambda bi, ps: (bi, 0, 0, 0))],
            out_specs=pl.BlockSpec((1, nt, 8, tm), lambda bi, ps: (bi, 0, 0, 0))),
        out_shape=jax.ShapeDtypeStruct((b, nt, 8, tm), I32),
        name="dest_rows",
    )(pad_start, ri_c)


def _for_each_row(fn):
    for base in range(0, TOK_TILE, LANES):
        def group(g, carry, base=base):
            r0 = base + pl.multiple_of(g * ROW_UNROLL, ROW_UNROLL)
            for u in range(ROW_UNROLL):
                for k in range(2):
                    fn(r0 + u, k)
            return carry

        lax.fori_loop(0, LANES // ROW_UNROLL, group, 0)


def _dispatch_kernel(dest_ref, h2p_ref, xd_in_ref, xd_ref, sem):
    del xd_in_ref

    def row_copy(r, k):
        return pltpu.make_async_copy(h2p_ref.at[pl.ds(r, 1)], xd_ref.at[pl.ds(dest_ref[0, k, r], 1)], sem)

    _for_each_row(lambda r, k: row_copy(r, k).start(priority=k))
    _for_each_row(lambda r, k: row_copy(r, k).wait())


def _dispatch(dest_t, h2p, n_rows):
    t, w = h2p.shape
    xd0 = jnp.zeros((n_rows, w), U32)
    return pl.pallas_call(
        _dispatch_kernel,
        grid=(t // TOK_TILE,),
        in_specs=[pl.BlockSpec((1, 8, TOK_TILE), lambda i: (i, 0, 0), memory_space=pltpu.SMEM),
                  pl.BlockSpec((TOK_TILE, w), lambda i: (i, 0)),
                  pl.BlockSpec(memory_space=pl.ANY)],
        out_specs=pl.BlockSpec(memory_space=pl.ANY),
        scratch_shapes=[pltpu.SemaphoreType.DMA(())],
        out_shape=jax.ShapeDtypeStruct((n_rows, w), U32),
        input_output_aliases={2: 0},
        compiler_params=pltpu.CompilerParams(
            dimension_semantics=("arbitrary",), vmem_limit_bytes=VMEM_LIMIT, has_side_effects=True),
        name="dispatch",
    )(dest_t, h2p, xd0)


def _experts_kernel(be_ref, nu_ref, xd_ref, w1_ref, w3_ref, w2_ref, y_ref, w1b_ref, w3b_ref, w2b_ref):
    i = pl.program_id(0)

    @pl.when((i == 0) | (be_ref[i] != be_ref[jnp.maximum(i - 1, 0)]))
    def _():
        w1b_ref[...] = w1_ref[0].astype(BF16)
        w3b_ref[...] = w3_ref[0].astype(BF16)
        w2b_ref[...] = w2_ref[0].astype(BF16)

    @pl.when(i < nu_ref[0])
    def _():
        word = xd_ref[...]
        xa, xb = (half.astype(BF16) for half in _unpack_halves(word))
        dh = xa.shape[1]
        a = _dot(xa, w1b_ref[0:dh]) + _dot(xb, w1b_ref[dh:2 * dh])
        c = _dot(xa, w3b_ref[0:dh]) + _dot(xb, w3b_ref[dh:2 * dh])
        y_ref[...] = _pack_halves(_dot((_silu(a) * c).astype(BF16), w2b_ref[...]))

    @pl.when(i >= nu_ref[0])
    def _():
        y_ref[...] = jnp.zeros_like(y_ref)


def _experts(blk_expert, n_used, xd, w1, w3, w2):
    n_rows, dh = xd.shape
    _, d, de = w1.shape
    n_blocks = n_rows // MOE_BLOCK
    return pl.pallas_call(
        _experts_kernel,
        grid_spec=pltpu.PrefetchScalarGridSpec(
            num_scalar_prefetch=2,
            grid=(n_blocks,),
            in_specs=[pl.BlockSpec((MOE_BLOCK, dh), lambda i, be, nu: (i, 0)),
                      pl.BlockSpec((1, d, de), lambda i, be, nu: (be[i], 0, 0)),
                      pl.BlockSpec((1, d, de), lambda i, be, nu: (be[i], 0, 0)),
                      pl.BlockSpec((1, de, d), lambda i, be, nu: (be[i], 0, 0))],
            out_specs=pl.BlockSpec((MOE_BLOCK, d // 2), lambda i, be, nu: (i, 0)),
            scratch_shapes=[pltpu.VMEM((d, de), BF16), pltpu.VMEM((d, de), BF16), pltpu.VMEM((de, d), BF16)]),
        out_shape=jax.ShapeDtypeStruct((n_rows, d // 2), U32),
        compiler_params=pltpu.CompilerParams(
            dimension_semantics=("arbitrary",), vmem_limit_bytes=VMEM_LIMIT),
        name="experts",
    )(blk_expert, n_used, xd, w1, w3, w2)


def _combine_kernel(dest_ref, dnext_ref, rw_ref, x1_ref, mod_ref, y_ref, o_ref, buf_ref, sems):
    i = pl.program_id(0)
    slot = i % 2

    def gather(idx_ref, slot_, start):
        def one(r, k):
            cp = pltpu.make_async_copy(y_ref.at[pl.ds(idx_ref[0, k, r], 1)],
                                       buf_ref.at[slot_, k, pl.ds(r, 1)], sems.at[slot_])
            if start:
                cp.start(priority=k)
            else:
                cp.wait()

        _for_each_row(one)

    @pl.when(i == 0)
    def _():
        gather(dest_ref, slot, True)

    @pl.when(i + 1 < pl.num_programs(0))
    def _():
        gather(dnext_ref, 1 - slot, True)

    gather(dest_ref, slot, False)
    w = rw_ref[...]
    g2 = mod_ref[0][5:6]
    y0 = jnp.concatenate(_unpack_halves(buf_ref[slot, 0]), axis=1)
    y1 = jnp.concatenate(_unpack_halves(buf_ref[slot, 1]), axis=1)
    o_ref[...] = x1_ref[...] + g2 * (w[:, 0:1] * y0 + w[:, 1:2] * y1)


def _combine(dest_t, rw, x1, mod, y_disp, seq):
    t, d = x1.shape
    tiles_per_seq = seq // TOK_TILE
    n_tiles = t // TOK_TILE
    return pl.pallas_call(
        _combine_kernel,
        grid=(n_tiles,),
        in_specs=[pl.BlockSpec((1, 8, TOK_TILE), lambda i: (i, 0, 0), memory_space=pltpu.SMEM),
                  pl.BlockSpec((1, 8, TOK_TILE), lambda i: (jnp.minimum(i + 1, n_tiles - 1), 0, 0),
                               memory_space=pltpu.SMEM),
                  pl.BlockSpec((TOK_TILE, LANES), lambda i: (i, 0)),
                  pl.BlockSpec((TOK_TILE, d), lambda i: (i, 0)),
                  pl.BlockSpec((1, 6, d), lambda i: (i // tiles_per_seq, 0, 0)),
                  pl.BlockSpec(memory_space=pl.ANY)],
        out_specs=pl.BlockSpec((TOK_TILE, d), lambda i: (i, 0)),
        scratch_shapes=[pltpu.VMEM((2, 2, TOK_TILE, d // 2), U32), pltpu.SemaphoreType.DMA((2,))],
        out_shape=jax.ShapeDtypeStruct((t, d), F32),
        compiler_params=pltpu.CompilerParams(
            dimension_semantics=("arbitrary",), vmem_limit_bytes=VMEM_LIMIT),
        name="combine",
    )(dest_t, dest_t, rw, x1, mod, y_disp)


def _moe(x1, h2p, ri, rw, cnt, mod, w1, w3, w2):
    b, s, d = x1.shape
    t = b * s
    a = 2 * t
    counts = cnt[0, ROUTE_E0:ROUTE_E0 + N_EXPERTS].astype(I32)
    padded = (counts + MOE_BLOCK - 1) // MOE_BLOCK * MOE_BLOCK
    pad_end = jnp.cumsum(padded)
    pad_start = (pad_end - padded).astype(I32)
    n_blocks = (a + N_EXPERTS * (MOE_BLOCK - 1)) // MOE_BLOCK + 1
    blk_start = jnp.arange(n_blocks, dtype=I32) * MOE_BLOCK
    blk_expert = jnp.minimum(
        jnp.sum((pad_end[None, :] <= blk_start[:, None]).astype(I32), axis=1), N_EXPERTS - 1)
    n_used = (pad_end[-1:] // MOE_BLOCK).astype(I32)
    dest_t = _dest_rows(pad_start, ri).reshape(t // TOK_TILE, 8, TOK_TILE)
    xd = _dispatch(dest_t, h2p.reshape(t, d // 2), n_blocks * MOE_BLOCK)
    y_disp = _experts(blk_expert, n_used, xd, w1, w3, w2)
    out = _combine(dest_t, rw.reshape(t, LANES), x1.reshape(t, d), mod, y_disp, s)
    return out.reshape(b, s, d)


def kernel(x, c, w_ada, b_ada, norm1_w, w_in, q_norm_w, k_norm_w, cmp_pos_k, cmp_pos_v, cmp_k_w1, cmp_k_w2,
           cmp_v_w1, cmp_v_w2, conv_w, out_norm_w, w_out, rel_bias, norm2_w, w_group, b_group, w_expert,
           b_expert, w1, w3, w2):
    t_hi, t_lo, t_c = _bias_tables(rel_bias)
    for l in range(w_ada.shape[0]):
        mod = _ada(c, w_ada[l], b_ada[l])
        q, kvc, kaug, vaug, kwin, vwaug, gates, yconv = _inproj(
            x, mod, norm1_w[l], w_in[l], q_norm_w[l], k_norm_w[l], conv_w[l], out_norm_w[l])
        vaug, vscale = _to_f8(vaug)
        vwaug, vwscale = _to_f8(vwaug)
        rows = _kvc_rows(kvc)
        k_c = _compress(rows, 0, cmp_k_w1[l], cmp_k_w2[l], cmp_pos_k[l], k_norm_w[l, 0], True)
        v_c = _compress(rows, 1, cmp_v_w1[l], cmp_v_w2[l], cmp_pos_v[l], k_norm_w[l, 0], False)
        o_cmp, pen = _cmpsel(q, k_c, v_c, t_c)
        mixa = _slcwin(q, pen, kaug, vaug, vscale, kwin, vwaug, vwscale, o_cmp, gates, t_hi, t_lo,
                       out_norm_w[l])
        x1, h2p, ri, rw, cnt = _outproj(mixa, yconv, w_out[l], x, mod, norm2_w[l],
                                        w_group[l], b_group[l], w_expert[l], b_expert[l])
        x = _moe(x1, h2p, ri, rw, cnt, mod, w1[l], w3[l], w2[l])
    return x
```
